```python
import math
import jax, jax.numpy as jnp
from jax import lax
import numpy as np

D_MODEL = 1024
BATCH = 2
SEQ = 8192
DEPTH = 4

HA = 8
DA = 64
HI = 8
DI = 64
TOPK_MAX = 256
HB = 4
DB = 64
CONV_W = 31
D_FF = 2816
NUM_BUCKETS = 32
MAX_DISTANCE = 128
QBLK = 128
ALPHA = (2 * DEPTH) ** 0.25
BETA = (8 * DEPTH) ** -0.25
LN_EPS = 1e-5
SPLITS = (HA * DA, HA * DA, HA * DA, HI * DI, DI, HI, HB * 2 * DB, HB * 2 * DB, HB * 2 * DB)
P_MIX = sum(SPLITS)

kernel_name = "hybrid_dsa_diffattn_conformer_macaron_deepnorm"


def layer_norm(x, g, b):
    xf = x.astype(jnp.float32)
    mu = jnp.mean(xf, axis=-1, keepdims=True)
    var = jnp.mean(jnp.square(xf - mu), axis=-1, keepdims=True)
    y = (xf - mu) * lax.rsqrt(var + LN_EPS)
    return (y * g.astype(jnp.float32) + b.astype(jnp.float32)).astype(x.dtype)


def rms_norm(x, g):
    xf = x.astype(jnp.float32)
    y = xf * lax.rsqrt(jnp.mean(jnp.square(xf), axis=-1, keepdims=True) + LN_EPS)
    return (y * g.astype(jnp.float32)).astype(x.dtype)


def swiglu_ffn(x, w_in, w_out):
    a, u = jnp.split(x @ w_in, 2, axis=-1)
    return (jax.nn.silu(a) * u) @ w_out


def t5_bucket(n):
    n = jnp.maximum(n, 0)
    max_exact = NUM_BUCKETS // 2
    nf = jnp.maximum(n, 1).astype(jnp.float32)
    large = max_exact + (jnp.log(nf / max_exact) / math.log(MAX_DISTANCE / max_exact)
                         * (NUM_BUCKETS - max_exact)).astype(jnp.int32)
    large = jnp.minimum(large, NUM_BUCKETS - 1)
    return jnp.where(n < max_exact, n, large)


def dsa_attention(q, k, v, iq, ik, iw, rel_bias):
    B, S = q.shape[0], q.shape[1]
    top_k = min(TOPK_MAX, S // 4)
    n_blocks = S // QBLK
    key_pos = jnp.arange(S)

    def block(i):
        t0 = i * QBLK
        qb = lax.dynamic_slice_in_dim(q, t0, QBLK, axis=1)
        iqb = lax.dynamic_slice_in_dim(iq, t0, QBLK, axis=1)
        iwb = lax.dynamic_slice_in_dim(iw, t0, QBLK, axis=1)
        qpos = t0 + jnp.arange(QBLK)
        causal = key_pos[None, :] <= qpos[:, None]
        idx_logits = jnp.einsum('bqhd,bsd->bqhs', iqb, ik) * (DI ** -0.5)
        score = jnp.einsum('bqh,bqhs->bqs', iwb, jax.nn.relu(idx_logits))
        score = jnp.where(causal[None], score.astype(jnp.float32), -jnp.inf)
        _, sel = lax.top_k(score, top_k)
        valid = sel <= qpos[None, :, None]
        k_sel = jax.vmap(lambda kk, ss: kk[ss])(k, sel)
        v_sel = jax.vmap(lambda vv, ss: vv[ss])(v, sel)
        logits = jnp.einsum('bqhd,bqkhd->bhqk', qb, k_sel).astype(jnp.float32) * (DA ** -0.5)
        bias = rel_bias[t5_bucket(qpos[None, :, None] - sel)]
        logits = logits + jnp.transpose(bias, (0, 3, 1, 2)).astype(jnp.float32)
        logits = jnp.where(valid[:, None], logits, -jnp.inf)
        p = jax.nn.softmax(logits, axis=-1).astype(v.dtype)
        return jnp.einsum('bhqk,bqkhd->bqhd', p, v_sel)

    out = lax.map(block, jnp.arange(n_blocks))
    return jnp.transpose(out, (1, 0, 2, 3, 4)).reshape(B, S, HA, DA)


def diff_attention(q, k, v, lam_full, rel_bias):
    B, S = q.shape[0], q.shape[1]
    n_blocks = S // QBLK
    key_pos = jnp.arange(S)

    def block(i):
        t0 = i * QBLK
        qb = lax.dynamic_slice_in_dim(q, t0, QBLK, axis=1)
        qpos = t0 + jnp.arange(QBLK)
        causal = key_pos[None, :] <= qpos[:, None]
        logits = jnp.einsum('bqhmd,bshmd->bhmqs', qb, k).astype(jnp.float32) * (DB ** -0.5)
        bias = rel_bias[t5_bucket(qpos[:, None] - key_pos[None, :])]
        logits = logits + jnp.transpose(bias, (2, 0, 1))[None, :, None].astype(jnp.float32)
        logits = jnp.where(causal[None, None, None], logits, -jnp.inf)
        p = jax.nn.softmax(logits, axis=-1)
        attn = (p[:, :, 0] - lam_full * p[:, :, 1]).astype(v.dtype)
        return jnp.einsum('bhqs,bshe->bqhe', attn, v)

    out = lax.map(block, jnp.arange(n_blocks))
    return jnp.transpose(out, (1, 0, 2, 3, 4)).reshape(B, S, HB, 2 * DB)


def attention_mixer(x, w_in, ik_g, ik_b, lam, subln_g, w_out, rel_bias, layer_idx):
    B, S, _ = x.shape
    h = x @ w_in
    qa, ka, va, iq, ik, iw, qb, kb, vb = jnp.split(h, np.cumsum(SPLITS)[:-1].tolist(), axis=-1)
    ik = layer_norm(ik, ik_g, ik_b)
    iw = iw * (HI ** -0.5)
    o_a = dsa_attention(qa.reshape(B, S, HA, DA), ka.reshape(B, S, HA, DA), va.reshape(B, S, HA, DA),
                        iq.reshape(B, S, HI, DI), ik, iw, rel_bias[:, :HA])
    lam_init = 0.8 - 0.6 * math.exp(-0.3 * layer_idx)
    lamf = lam.astype(jnp.float32)
    lam_full = jnp.exp(jnp.sum(lamf[0] * lamf[1])) - jnp.exp(jnp.sum(lamf[2] * lamf[3])) + lam_init
    o_b = diff_attention(qb.reshape(B, S, HB, 2, DB), kb.reshape(B, S, HB, 2, DB),
                         vb.reshape(B, S, HB, 2 * DB), lam_full, rel_bias[:, HA:])
    o_b = rms_norm(o_b, subln_g) * (1.0 - lam_init)
    o = jnp.concatenate([o_a.reshape(B, S, HA * DA), o_b.reshape(B, S, HB * 2 * DB)], axis=-1)
    return o @ w_out


def conv_module(x, w_in, dw, dw_b, g, b, w_out):
    h = x @ w_in
    a, gate = jnp.split(h, 2, axis=-1)
    h = a * jax.nn.sigmoid(gate)
    h = lax.conv_general_dilated(h, dw[:, None, :].astype(h.dtype), window_strides=(1,),
                                 padding=[(CONV_W - 1, 0)],
                                 dimension_numbers=('NWC', 'WIO', 'NWC'),
                                 feature_group_count=D_MODEL) + dw_b
    h = jax.nn.silu(layer_norm(h, g, b))
    return h @ w_out


def setup_inputs(seed: int = 0) -> dict:
    key = jax.random.key(seed)
    ks = jax.random.split(key, 24)
    n_even = (DEPTH + 1) // 2
    n_odd = DEPTH // 2
    D = D_MODEL
    f32 = jnp.float32
    nrm = lambda k, shape, s: (jax.random.normal(k, shape, f32) * s)
    return {
        "x": nrm(ks[0], (BATCH, SEQ, D), 1.0),
        "ffn_in": nrm(ks[1], (DEPTH, 2, D, 2 * D_FF), D ** -0.5),
        "ffn_out": nrm(ks[2], (DEPTH, 2, D_FF, D), BETA * D_FF ** -0.5),
        "ln_g": 1.0 + nrm(ks[3], (DEPTH, 3, D), 0.02),
        "ln_b": nrm(ks[4], (DEPTH, 3, D), 0.02),
        "rel_bias": nrm(ks[5], (NUM_BUCKETS, HA + HB), 0.5),
        "mix_w_in": nrm(ks[6], (n_even, D, P_MIX), D ** -0.5),
        "idx_k_g": 1.0 + nrm(ks[7], (n_even, DI), 0.02),
        "idx_k_b": nrm(ks[8], (n_even, DI), 0.02),
        "diff_lambda": nrm(ks[9], (n_even, 4, DB), 0.1),
        "diff_subln_g": 1.0 + nrm(ks[10], (n_even, 2 * DB), 0.02),
        "mix_w_out": nrm(ks[11], (n_even, D, D), BETA * D ** -0.5),
        "conv_w_in": nrm(ks[12], (n_odd, D, 2 * D), D ** -0.5),
        "conv_dw": nrm(ks[13], (n_odd, CONV_W, D), CONV_W ** -0.5),
        "conv_dw_b": nrm(ks[14], (n_odd, D), 0.02),
        "conv_ln_g": 1.0 + nrm(ks[15], (n_odd, D), 0.02),
        "conv_ln_b": nrm(ks[16], (n_odd, D), 0.02),
        "conv_w_out": nrm(ks[17], (n_odd, D, D), BETA * D ** -0.5),
    }


def reference(x, ffn_in, ffn_out, ln_g, ln_b, rel_bias, mix_w_in, idx_k_g, idx_k_b,
              diff_lambda, diff_subln_g, mix_w_out, conv_w_in, conv_dw, conv_dw_b,
              conv_ln_g, conv_ln_b, conv_w_out):
    for l in range(DEPTH):
        j = l // 2
        x = layer_norm(ALPHA * x + 0.5 * swiglu_ffn(x, ffn_in[l, 0], ffn_out[l, 0]), ln_g[l, 0], ln_b[l, 0])
        if l % 2 == 0:
            m = attention_mixer(x, mix_w_in[j], idx_k_g[j], idx_k_b[j], diff_lambda[j],
                                diff_subln_g[j], mix_w_out[j], rel_bias, l)
        else:
            m = conv_module(x, conv_w_in[j], conv_dw[j], conv_dw_b[j], conv_ln_g[j],
                            conv_ln_b[j], conv_w_out[j])
        x = layer_norm(ALPHA * x + m, ln_g[l, 1], ln_b[l, 1])
        x = layer_norm(ALPHA * x + 0.5 * swiglu_ffn(x, ffn_in[l, 1], ffn_out[l, 1]), ln_g[l, 2], ln_b[l, 2])
    return x
```

```python
import functools
import math

import jax
import jax.numpy as jnp
from jax import lax
from jax.experimental import pallas as pl
from jax.experimental.pallas import tpu as pltpu

F32 = jnp.float32
BF16 = jnp.bfloat16

D_MODEL = 1024
DEPTH = 4
HA, DA = 8, 64
HI, DI = 8, 64
TOPK_MAX = 256
HB, DB = 4, 64
CONV_W = 31
D_FF = 2816
NUM_BUCKETS = 32
MAX_DISTANCE = 128
ALPHA = (2 * DEPTH) ** 0.25
LN_EPS = 1e-5

LANES = 128
SUBLANES = 8
VMEM_LIMIT = 56 * 1024 * 1024

TQ = 128
TK = 512
KB = TK // LANES
TM_FFN = 512
TF_FFN = 1408
TM_PROJ = 512
TM_CONV = 512
HALO = 32
CONV_ROWS = 32
NEG = -1e30
SEARCH_CAP = 400


def _params(*sem):
    return pltpu.CompilerParams(dimension_semantics=sem, vmem_limit_bytes=VMEM_LIMIT)


def _layer_norm(v, g, b):
    mu = jnp.mean(v, axis=-1, keepdims=True)
    c = v - mu
    var = jnp.mean(c * c, axis=-1, keepdims=True)
    return c * lax.rsqrt(var + LN_EPS) * g + b


def _dot(a, b):
    return jnp.dot(a, b, preferred_element_type=F32)


def _dot_nt(a, b):
    return lax.dot_general(a, b, (((1,), (1,)), ((), ())), preferred_element_type=F32)


def _ffn_kernel(x_ref, wg_ref, wu_ref, wo_ref, g_ref, b_ref, o_ref, xb_ref, acc_ref):
    j = pl.program_id(1)

    @pl.when(j == 0)
    def _():
        xb_ref[...] = x_ref[...].astype(BF16)
        acc_ref[...] = jnp.zeros_like(acc_ref)

    xb = xb_ref[...]
    a = _dot(xb, wg_ref[...])
    u = _dot(xb, wu_ref[...])
    h = (a * jax.nn.sigmoid(a) * u).astype(BF16)
    acc_ref[...] += _dot(h, wo_ref[...])

    @pl.when(j == pl.num_programs(1) - 1)
    def _():
        y = ALPHA * x_ref[...] + 0.5 * acc_ref[...]
        o_ref[...] = _layer_norm(y, g_ref[...], b_ref[...])


def _ffn(x, w_in, w_out, g, b):
    t, d = x.shape
    f = w_out.shape[0]
    nf = f // TF_FFN
    return pl.pallas_call(
        _ffn_kernel,
        grid=(t // TM_FFN, nf),
        in_specs=[
            pl.BlockSpec((TM_FFN, d), lambda i, j: (i, 0)),
            pl.BlockSpec((d, TF_FFN), lambda i, j: (0, j)),
            pl.BlockSpec((d, TF_FFN), lambda i, j: (0, j + nf)),
            pl.BlockSpec((TF_FFN, d), lambda i, j: (j, 0)),
            pl.BlockSpec((1, d), lambda i, j: (0, 0)),
            pl.BlockSpec((1, d), lambda i, j: (0, 0)),
        ],
        out_specs=pl.BlockSpec((TM_FFN, d), lambda i, j: (i, 0)),
        out_shape=jax.ShapeDtypeStruct((t, d), F32),
        scratch_shapes=[pltpu.VMEM((TM_FFN, d), BF16), pltpu.VMEM((TM_FFN, d), F32)],
        compiler_params=_params("parallel", "arbitrary"),
        name="ffn",
    )(x, w_in, w_in, w_out, g, b)


_W3 = 3 * HA * DA
_C_IQ = _W3
_C_IK = _C_IQ + HI * DI
_C_IW = _C_IK + LANES
_C_B = _C_IW + LANES
_C_END = _C_B + 3 * HB * 2 * DB


def _mix_proj_kernel(x_ref, w_ref, ikg_ref, ikb_ref,
                     qa_ref, ka_ref, va_ref, iq_ref, ik_ref, iw_ref, qb_ref, kb_ref, vb_ref):
    xb = x_ref[...].astype(BF16)
    hd = HA * DA
    qa_ref[...] = (_dot(xb, w_ref[:, 0:hd]) * (DA ** -0.5)).astype(BF16)
    ka_ref[...] = _dot(xb, w_ref[:, hd:2 * hd]).astype(BF16)
    va_ref[...] = _dot(xb, w_ref[:, 2 * hd:3 * hd]).astype(BF16)
    iq_ref[...] = (_dot(xb, w_ref[:, _C_IQ:_C_IK]) * (DI ** -0.5)).astype(BF16)
    ik2 = _dot(xb, w_ref[:, _C_IK:_C_IW])
    ik_ref[...] = _layer_norm(ik2, ikg_ref[...], ikb_ref[...]).astype(BF16)
    iw_ref[...] = _dot(xb, w_ref[:, _C_IW:_C_B]) * (HI ** -0.5)
    hb = HB * 2 * DB
    qb_ref[...] = (_dot(xb, w_ref[:, _C_B:_C_B + hb]) * (DB ** -0.5)).astype(BF16)
    kb_ref[...] = _dot(xb, w_ref[:, _C_B + hb:_C_B + 2 * hb]).astype(BF16)
    vb_ref[...] = _dot(xb, w_ref[:, _C_B + 2 * hb:_C_END]).astype(BF16)


def _mix_proj(x, w, ikg2, ikb2):
    t, d = x.shape
    row = lambda i: (i, 0)
    const = lambda i: (0, 0)
    wide = lambda n, dt: jax.ShapeDtypeStruct((t, n), dt)
    hd, hb = HA * DA, HB * 2 * DB
    return pl.pallas_call(
        _mix_proj_kernel,
        grid=(t // TM_PROJ,),
        in_specs=[
            pl.BlockSpec((TM_PROJ, d), row),
            pl.BlockSpec((d, _C_END), const),
            pl.BlockSpec((1, LANES), const),
            pl.BlockSpec((1, LANES), const),
        ],
        out_specs=[
            pl.BlockSpec((TM_PROJ, hd), row), pl.BlockSpec((TM_PROJ, hd), row),
            pl.BlockSpec((TM_PROJ, hd), row), pl.BlockSpec((TM_PROJ, HI * DI), row),
            pl.BlockSpec((TM_PROJ, LANES), row), pl.BlockSpec((TM_PROJ, LANES), row),
            pl.BlockSpec((TM_PROJ, hb), row), pl.BlockSpec((TM_PROJ, hb), row),
            pl.BlockSpec((TM_PROJ, hb), row),
        ],
        out_shape=[wide(hd, BF16), wide(hd, BF16), wide(hd, BF16), wide(HI * DI, BF16),
                   wide(LANES, BF16), wide(LANES, F32), wide(hb, BF16), wide(hb, BF16),
                   wide(hb, BF16)],
        compiler_params=_params("parallel"),
        name="mix_proj",
    )(x, w, ikg2, ikb2)


def _bias_kernel(tbl_ref, o_ref):
    nh = o_ref.shape[0]
    r = lax.broadcasted_iota(jnp.int32, (TQ, LANES), 0)
    c = lax.broadcasted_iota(jnp.int32, (TQ, LANES), 1)
    max_exact = NUM_BUCKETS // 2
    for blk in range(2):
        n = jnp.maximum(r - c + blk * TQ, 0)
        nf = jnp.maximum(n, 1).astype(F32)
        large = max_exact + (jnp.log(nf / max_exact) / math.log(MAX_DISTANCE / max_exact)
                             * (NUM_BUCKETS - max_exact)).astype(jnp.int32)
        large = jnp.minimum(large, NUM_BUCKETS - 1)
        bucket = jnp.where(n < max_exact, n, large)
        for h in range(nh):
            acc = jnp.zeros((TQ, LANES), F32)
            for k in range(NUM_BUCKETS):
                acc = jnp.where(bucket == k, tbl_ref[h, k], acc)
            o_ref[h, blk] = acc - tbl_ref[h, NUM_BUCKETS - 1]


def _bias_tiles(rel_bias_t):
    nh = rel_bias_t.shape[0]
    return pl.pallas_call(
        _bias_kernel,
        in_specs=[pl.BlockSpec(memory_space=pltpu.SMEM)],
        out_shape=jax.ShapeDtypeStruct((nh, 2, TQ, LANES), F32),
        name="bias_tiles",
    )(rel_bias_t)


def _half_masked(x, upper):
    lane = lax.broadcasted_iota(jnp.int32, x.shape, 1)
    keep = (lane >= LANES // 2) if upper else (lane < LANES // 2)
    return jnp.where(keep, x, jnp.zeros_like(x))


def _near_bias(bias_ref, head, i, j):
    parts = []
    for cc in range(KB):
        delta = i - (j * KB + cc)
        b0 = bias_ref[head, 0]
        b1 = bias_ref[head, 1]
        parts.append(jnp.where(delta == 0, b0, jnp.where(delta == 1, b1, jnp.zeros_like(b0))))
    return jnp.concatenate(parts, axis=1)


def _flash_tile(qm, k_t, v_t, mask_bias, rel_bias, m_ref, l_ref, acc_ref, slot):
    s = _dot_nt(qm, k_t)
    if mask_bias is not None:
        s = s + mask_bias
    if rel_bias is not None:
        s = s + rel_bias
    m_prev = m_ref[slot]
    m_next = jnp.maximum(m_prev, jnp.max(s, axis=1, keepdims=True))
    p = jnp.exp(s - jnp.tile(m_next, (1, KB)))
    corr = jnp.exp(m_prev - m_next)
    l_ref[slot] = corr * l_ref[slot] + jnp.sum(p, axis=1, keepdims=True)
    m_ref[slot] = m_next
    acc_ref[slot] = corr * acc_ref[slot] + _dot(p.astype(BF16), v_t)


def _flash_init(m_ref, l_ref, acc_ref):
    m_ref[...] = jnp.full(m_ref.shape, NEG, F32)
    l_ref[...] = jnp.zeros_like(l_ref)
    acc_ref[...] = jnp.zeros_like(acc_ref)


def _dsa_kernel(q_ref, k_ref, v_ref, iq_ref, ik_ref, iw_ref, bias_ref, o_ref,
                sc_ref, m_ref, l_ref, acc_ref, *, top_k):
    i = pl.program_id(1)
    t0 = i * TQ
    jd = t0 // TK
    ntile = jd + 1
    qpos = t0 + lax.broadcasted_iota(jnp.int32, (TQ, LANES), 0)

    iq = iq_ref[...]
    iw = iw_ref[...]
    iqm = [_half_masked(iq[:, (h // 2) * LANES:(h // 2 + 1) * LANES], h % 2 == 1) for h in range(HI)]
    wcol = [iw[:, h:h + 1] for h in range(HI)]

    def score_tile(j, carry):
        rmax, rmin = carry
        ik_t = ik_ref[pl.ds(pl.multiple_of(j * TK, TK), TK), :]
        s = jnp.zeros((TQ, TK), F32)
        for h in range(HI):
            s = s + wcol[h] * jnp.maximum(_dot_nt(iqm[h], ik_t), 0.0)
        kpos = j * TK + lax.broadcasted_iota(jnp.int32, (TQ, TK), 1)
        valid = kpos <= jnp.tile(qpos, (1, KB))
        sc_ref[j] = jnp.where(valid, s, -jnp.inf)
        rmax = jnp.maximum(rmax, jnp.max(jnp.where(valid, s, -jnp.inf), axis=1, keepdims=True))
        rmin = jnp.minimum(rmin, jnp.min(jnp.where(valid, s, jnp.inf), axis=1, keepdims=True))
        return rmax, rmin

    rmax, rmin = lax.fori_loop(
        0, ntile, score_tile,
        (jnp.full((TQ, LANES), -jnp.inf, F32), jnp.full((TQ, LANES), jnp.inf, F32)))

    def row_count(pred):
        def body(j, cnt):
            t = sc_ref[j]
            for cc in range(KB):
                cnt = cnt + jnp.where(pred(t[:, cc * LANES:(cc + 1) * LANES]), 1.0, 0.0)
            return cnt
        cnt = lax.fori_loop(0, ntile, body, jnp.zeros((TQ, LANES), F32))
        return jnp.broadcast_to(jnp.sum(cnt, axis=1, keepdims=True), (TQ, LANES))

    def count_ge(x):
        return row_count(lambda t: t >= x)

    kf = float(top_k)
    nvalid = (qpos + 1).astype(F32)
    few = nvalid <= kf
    c_ge0 = count_ge(0.0)
    c_gt0 = row_count(lambda t: t > 0.0)
    at0 = jnp.logical_and(c_gt0 < kf, c_ge0 >= kf)
    pos = c_gt0 >= kf
    lo0 = jnp.where(few, -3.0e38, jnp.where(pos, 0.0, jnp.where(at0, 0.0, rmin)))
    hi0 = jnp.where(pos, rmax, 0.0)
    clo0 = jnp.where(few, kf, jnp.where(jnp.logical_or(pos, at0), c_ge0, nvalid))
    done0 = jnp.where(jnp.logical_or(few, jnp.logical_or(at0, clo0 == kf)), 1.0, 0.0)

    def search_cond(st):
        it, _, _, _, done = st
        return jnp.logical_and(it < SEARCH_CAP, jnp.min(done) < 0.5)

    def search_body(st):
        it, lo, hi, clo, done = st
        mid = lo + (hi - lo) * 0.5
        stuck = jnp.logical_or(mid <= lo, mid >= hi)
        cm = count_ge(mid)
        ge = cm >= kf
        act = done < 0.5
        up = jnp.logical_and(act, ge)
        dn = jnp.logical_and(act, jnp.logical_not(ge))
        lo = jnp.where(up, mid, lo)
        clo = jnp.where(up, cm, clo)
        hi = jnp.where(dn, mid, hi)
        done = jnp.where(jnp.logical_or(stuck, clo == kf), 1.0, done)
        return it + 1, lo, hi, clo, done

    _, theta, _, c_theta, _ = lax.while_loop(search_cond, search_body, (jnp.int32(0), lo0, hi0, clo0, done0))

    tie = jnp.logical_and(jnp.logical_not(few), c_theta > kf)

    @pl.when(jnp.max(jnp.where(tie, 1.0, 0.0)) > 0.5)
    def _():
        n_eq = row_count(lambda t: t == theta)
        keep = kf - (c_theta - n_eq)
        ra = lax.broadcasted_iota(jnp.int32, (TK, TK), 0)
        ca = lax.broadcasted_iota(jnp.int32, (TK, TK), 1)
        tri = jnp.where(ra <= ca, 1.0, 0.0).astype(BF16)
        theta_t = jnp.tile(theta, (1, KB))
        keep_t = jnp.tile(keep, (1, KB))
        tie_t = jnp.tile(jnp.where(tie, 1.0, 0.0), (1, KB)) > 0.5

        def body(j, before):
            t = sc_ref[j]
            eq = t == theta_t
            eqf = jnp.where(eq, 1.0, 0.0)
            rank = _dot(eqf.astype(BF16), tri) + jnp.tile(before, (1, KB))
            drop = jnp.logical_and(jnp.logical_and(eq, tie_t), rank > keep_t)
            sc_ref[j] = jnp.where(drop, -jnp.inf, t)
            return before + jnp.sum(eqf, axis=1, keepdims=True)

        lax.fori_loop(0, ntile, body, jnp.zeros((TQ, LANES), F32))

    _flash_init(m_ref, l_ref, acc_ref)
    q = q_ref[...]
    qm = [_half_masked(q[:, (h // 2) * LANES:(h // 2 + 1) * LANES], h % 2 == 1) for h in range(HA)]

    def attend(j, near):
        t = sc_ref[j]
        mb = jnp.concatenate(
            [jnp.where(t[:, cc * LANES:(cc + 1) * LANES] >= theta, 0.0, NEG) for cc in range(KB)], axis=1)
        row = pl.ds(pl.multiple_of(j * TK, TK), TK)
        for h in range(HA):
            p = h // 2
            k_t = k_ref[row, p * LANES:(p + 1) * LANES]
            v_t = v_ref[row, p * LANES:(p + 1) * LANES]
            rb = _near_bias(bias_ref, h, i, j) if near else None
            _flash_tile(qm[h], k_t, v_t, mb, rb, m_ref, l_ref, acc_ref, h)

    def far_body(j, c):
        attend(j, False)
        return c

    lax.fori_loop(0, jnp.maximum(jd - 1, 0), far_body, 0)

    @pl.when(jd >= 1)
    def _():
        attend(jd - 1, True)

    attend(jd, True)

    for p in range(HA // 2):
        o_lo = acc_ref[2 * p] / l_ref[2 * p]
        o_hi = acc_ref[2 * p + 1] / l_ref[2 * p + 1]
        lane = lax.broadcasted_iota(jnp.int32, (TQ, LANES), 1)
        o_ref[:, p * LANES:(p + 1) * LANES] = jnp.where(lane < LANES // 2, o_lo, o_hi).astype(BF16)


def _dsa(q, k, v, iq, ik2, iw, bias, batch, seq):
    top_k = min(TOPK_MAX, seq // 4)
    nq = seq // TQ
    nt = seq // TK
    hd = HA * DA
    qrow = lambda b, i: (b * nq + i, 0)
    whole = lambda b, i: (b, 0)
    return pl.pallas_call(
        functools.partial(_dsa_kernel, top_k=top_k),
        grid=(batch, nq),
        in_specs=[
            pl.BlockSpec((TQ, hd), qrow),
            pl.BlockSpec((seq, hd), whole),
            pl.BlockSpec((seq, hd), whole),
            pl.BlockSpec((TQ, HI * DI), qrow),
            pl.BlockSpec((seq, LANES), whole),
            pl.BlockSpec((TQ, LANES), qrow),
            pl.BlockSpec((HA, 2, TQ, LANES), lambda b, i: (0, 0, 0, 0)),
        ],
        out_specs=pl.BlockSpec((TQ, hd), qrow),
        out_shape=jax.ShapeDtypeStruct((batch * seq, hd), BF16),
        scratch_shapes=[
            pltpu.VMEM((nt, TQ, TK), F32),
            pltpu.VMEM((HA, TQ, LANES), F32),
            pltpu.VMEM((HA, TQ, LANES), F32),
            pltpu.VMEM((HA, TQ, LANES), F32),
        ],
        compiler_params=_params("parallel", "arbitrary"),
        name="dsa",
    )(q, k, v, iq, ik2, iw, bias)


def _diff_kernel(q_ref, k_ref, v_ref, lam_ref, g_ref, bias_ref, o_ref, m_ref, l_ref, acc_ref, *, lam_init):
    i = pl.program_id(1)
    t0 = i * TQ
    jd = t0 // TK
    _flash_init(m_ref, l_ref, acc_ref)
    q = q_ref[...]
    nmap = 2 * HB
    qm = [_half_masked(q[:, (s // 2) * LANES:(s // 2 + 1) * LANES], s % 2 == 1) for s in range(nmap)]
    qpos = t0 + lax.broadcasted_iota(jnp.int32, (TQ, TK), 0)

    def attend(j, near, diag):
        row = pl.ds(pl.multiple_of(j * TK, TK), TK)
        mb = None
        if diag:
            kpos = j * TK + lax.broadcasted_iota(jnp.int32, (TQ, TK), 1)
            mb = jnp.where(kpos <= qpos, 0.0, NEG)
        for s in range(nmap):
            h = s // 2
            k_t = k_ref[row, h * LANES:(h + 1) * LANES]
            v_t = v_ref[row, h * LANES:(h + 1) * LANES]
            rb = _near_bias(bias_ref, h, i, j) if near else None
            _flash_tile(qm[s], k_t, v_t, mb, rb, m_ref, l_ref, acc_ref, s)

    def far_body(j, c):
        attend(j, False, False)
        return c

    lax.fori_loop(0, jnp.maximum(jd - 1, 0), far_body, 0)

    @pl.when(jd >= 1)
    def _():
        attend(jd - 1, True, False)

    attend(jd, True, True)

    lam = lam_ref[...]
    lam_full = (jnp.exp(jnp.sum(lam[0:1] * lam[1:2], axis=1, keepdims=True))
                - jnp.exp(jnp.sum(lam[2:3] * lam[3:4], axis=1, keepdims=True)) + lam_init)
    for h in range(HB):
        o0 = acc_ref[2 * h] / l_ref[2 * h]
        o1 = acc_ref[2 * h + 1] / l_ref[2 * h + 1]
        o = o0 - lam_full * o1
        y = o * lax.rsqrt(jnp.mean(o * o, axis=-1, keepdims=True) + LN_EPS) * g_ref[...]
        o_ref[:, h * LANES:(h + 1) * LANES] = (y * (1.0 - lam_init)).astype(BF16)


def _diff(q, k, v, lam, g, bias, batch, seq, lam_init):
    nq = seq // TQ
    hb = HB * 2 * DB
    qrow = lambda b, i: (b * nq + i, 0)
    whole = lambda b, i: (b, 0)
    nmap = 2 * HB
    return pl.pallas_call(
        functools.partial(_diff_kernel, lam_init=lam_init),
        grid=(batch, nq),
        in_specs=[
            pl.BlockSpec((TQ, hb), qrow),
            pl.BlockSpec((seq, hb), whole),
            pl.BlockSpec((seq, hb), whole),
            pl.BlockSpec((4, DB), lambda b, i: (0, 0)),
            pl.BlockSpec((1, 2 * DB), lambda b, i: (0, 0)),
            pl.BlockSpec((HB, 2, TQ, LANES), lambda b, i: (0, 0, 0, 0)),
        ],
        out_specs=pl.BlockSpec((TQ, hb), qrow),
        out_shape=jax.ShapeDtypeStruct((batch * seq, hb), BF16),
        scratch_shapes=[
            pltpu.VMEM((nmap, TQ, LANES), F32),
            pltpu.VMEM((nmap, TQ, LANES), F32),
            pltpu.VMEM((nmap, TQ, LANES), F32),
        ],
        compiler_params=_params("parallel", "arbitrary"),
        name="diff",
    )(q, k, v, lam, g, bias)


def _mix_out_kernel(oa_ref, ob_ref, wa_ref, wb_ref, x_ref, g_ref, b_ref, o_ref):
    m = _dot(oa_ref[...], wa_ref[...]) + _dot(ob_ref[...], wb_ref[...])
    o_ref[...] = _layer_norm(ALPHA * x_ref[...] + m, g_ref[...], b_ref[...])


def _mix_out(oa, ob, wa, wb, x, g, b):
    t, d = x.shape
    row = lambda i: (i, 0)
    const = lambda i: (0, 0)
    return pl.pallas_call(
        _mix_out_kernel,
        grid=(t // TM_PROJ,),
        in_specs=[
            pl.BlockSpec((TM_PROJ, oa.shape[1]), row),
            pl.BlockSpec((TM_PROJ, ob.shape[1]), row),
            pl.BlockSpec(wa.shape, const),
            pl.BlockSpec(wb.shape, const),
            pl.BlockSpec((TM_PROJ, d), row),
            pl.BlockSpec((1, d), const),
            pl.BlockSpec((1, d), const),
        ],
        out_specs=pl.BlockSpec((TM_PROJ, d), row),
        out_shape=jax.ShapeDtypeStruct((t, d), F32),
        compiler_params=_params("parallel"),
        name="mix_out",
    )(oa, ob, wa, wb, x, g, b)


def _conv_in_kernel(x_ref, w_ref, o_ref):
    xb = x_ref[...].astype(BF16)
    d = o_ref.shape[1]
    a = _dot(xb, w_ref[:, 0:d])
    gate = _dot(xb, w_ref[:, d:2 * d])
    o_ref[...] = a * jax.nn.sigmoid(gate)


def _conv_in(x, w):
    t, d = x.shape
    return pl.pallas_call(
        _conv_in_kernel,
        grid=(t // TM_CONV,),
        in_specs=[pl.BlockSpec((TM_CONV, d), lambda i: (i, 0)),
                  pl.BlockSpec((d, 2 * d), lambda i: (0, 0))],
        out_specs=pl.BlockSpec((TM_CONV, d), lambda i: (i, 0)),
        out_shape=jax.ShapeDtypeStruct((t, d), F32),
        compiler_params=_params("parallel"),
        name="conv_in",
    )(x, w)


def _conv_out_kernel(h_ref, halo_ref, dw_ref, dwb_ref, cg_ref, cb_ref, w_ref, x_ref, g_ref, b_ref,
                     o_ref, buf_ref, cv_ref):
    i = pl.program_id(1)
    buf_ref[0:HALO, :] = jnp.where(i > 0, halo_ref[...], 0.0)
    buf_ref[HALO:, :] = h_ref[...]
    off = HALO - (CONV_W - 1)

    def chunk(c, carry):
        r0 = pl.multiple_of(c * CONV_ROWS, CONV_ROWS)
        win = buf_ref[pl.ds(r0, CONV_ROWS + HALO), :]
        acc = jnp.broadcast_to(dwb_ref[...], (CONV_ROWS, dwb_ref.shape[1]))
        for sub in range(SUBLANES):
            shifted = win[sub:sub + CONV_ROWS + HALO - SUBLANES] if sub else win
            for k in range(CONV_W):
                if (off + k) % SUBLANES == sub:
                    a0 = off + k - sub
                    acc = acc + dw_ref[k:k + 1, :] * shifted[a0:a0 + CONV_ROWS]
        cv_ref[pl.ds(r0, CONV_ROWS), :] = acc
        return carry

    lax.fori_loop(0, TM_CONV // CONV_ROWS, chunk, 0)
    y = _layer_norm(cv_ref[...], cg_ref[...], cb_ref[...])
    u = (y * jax.nn.sigmoid(y)).astype(BF16)
    m = _dot(u, w_ref[...])
    o_ref[...] = _layer_norm(ALPHA * x_ref[...] + m, g_ref[...], b_ref[...])


def _conv_out(h, dw, dwb, cg, cb, w, x, g, b, batch, seq):
    t, d = x.shape
    nt = seq // TM_CONV
    per = TM_CONV // HALO
    row = lambda bb, i: (bb * nt + i, 0)
    halo = lambda bb, i: (jnp.maximum((bb * nt + i) * per - 1, 0), 0)
    const = lambda bb, i: (0, 0)
    return pl.pallas_call(
        _conv_out_kernel,
        grid=(batch, nt),
        in_specs=[
            pl.BlockSpec((TM_CONV, d), row),
            pl.BlockSpec((HALO, d), halo),
            pl.BlockSpec(dw.shape, const),
            pl.BlockSpec((1, d), const),
            pl.BlockSpec((1, d), const),
            pl.BlockSpec((1, d), const),
            pl.BlockSpec((d, d), const),
            pl.BlockSpec((TM_CONV, d), row),
            pl.BlockSpec((1, d), const),
            pl.BlockSpec((1, d), const),
        ],
        out_specs=pl.BlockSpec((TM_CONV, d), row),
        out_shape=jax.ShapeDtypeStruct((t, d), F32),
        scratch_shapes=[pltpu.VMEM((TM_CONV + HALO, d), F32), pltpu.VMEM((TM_CONV, d), F32)],
        compiler_params=_params("parallel", "arbitrary"),
        name="conv_out",
    )(h, h, dw, dwb, cg, cb, w, x, g, b)


def _attention_mixer(x, w_in, ikg, ikb, lam, subg, w_out, bias, g, b, batch, seq, layer_idx):
    d = x.shape[1]
    c = [0]
    for n in (HA * DA, HA * DA, HA * DA, HI * DI, DI, HI, HB * 2 * DB, HB * 2 * DB, HB * 2 * DB):
        c.append(c[-1] + n)
    w_ik = w_in[:, c[4]:c[5]]
    w_iw = jnp.pad(w_in[:, c[5]:c[6]], ((0, 0), (0, LANES - HI)))
    w = jnp.concatenate([w_in[:, :c[4]], w_ik, w_ik, w_iw, w_in[:, c[6]:]], axis=1).astype(BF16)
    ikg2 = jnp.concatenate([ikg, ikg])[None]
    ikb2 = jnp.concatenate([ikb, ikb])[None]
    qa, ka, va, iq, ik2, iw, qb, kb, vb = _mix_proj(x, w, ikg2, ikb2)
    o_a = _dsa(qa, ka, va, iq, ik2, iw, bias[:HA], batch, seq)
    lam_init = 0.8 - 0.6 * math.exp(-0.3 * layer_idx)
    o_b = _diff(qb, kb, vb, lam, subg[None], bias[HA:], batch, seq, lam_init)
    wo = w_out.astype(BF16)
    return _mix_out(o_a, o_b, wo[:HA * DA], wo[HA * DA:], x, g, b)


def _conv_module(x, w_in, dw, dwb, cg, cb, w_out, g, b, batch, seq):
    h = _conv_in(x, w_in.astype(BF16))
    dwp = jnp.pad(dw, ((0, HALO - CONV_W), (0, 0)))
    return _conv_out(h, dwp, dwb[None], cg[None], cb[None], w_out.astype(BF16), x, g, b, batch, seq)


def kernel(x, ffn_in, ffn_out, ln_g, ln_b, rel_bias, mix_w_in, idx_k_g, idx_k_b, diff_lambda,
           diff_subln_g, mix_w_out, conv_w_in, conv_dw, conv_dw_b, conv_ln_g, conv_ln_b, conv_w_out):
    batch, seq, d = x.shape
    x = x.reshape(batch * seq, d)
    bias = _bias_tiles(rel_bias.T)
    ffn_in_b = ffn_in.astype(BF16)
    ffn_out_b = ffn_out.astype(BF16)
    for l in range(DEPTH):
        j = l // 2
        x = _ffn(x, ffn_in_b[l, 0], ffn_out_b[l, 0], ln_g[l, 0][None], ln_b[l, 0][None])
        if l % 2 == 0:
            x = _attention_mixer(x, mix_w_in[j], idx_k_g[j], idx_k_b[j], diff_lambda[j], diff_subln_g[j],
                                 mix_w_out[j], bias, ln_g[l, 1][None], ln_b[l, 1][None], batch, seq, l)
        else:
            x = _conv_module(x, conv_w_in[j], conv_dw[j], conv_dw_b[j], conv_ln_g[j], conv_ln_b[j],
                             conv_w_out[j], ln_g[l, 1][None], ln_b[l, 1][None], batch, seq)
        x = _ffn(x, ffn_in_b[l, 1], ffn_out_b[l, 1], ln_g[l, 2][None], ln_b[l, 2][None])
    return x.reshape(batch, seq, d)
```

```python
import functools
import math

import jax
import jax.numpy as jnp
from jax import lax
from jax.experimental import pallas as pl
from jax.experimental.pallas import tpu as pltpu

F32 = jnp.float32
BF16 = jnp.bfloat16

D_MODEL = 1024
DEPTH = 4
HA, DA = 8, 64
HI, DI = 8, 64
TOPK_MAX = 256
HB, DB = 4, 64
CONV_W = 31
D_FF = 2816
NUM_BUCKETS = 32
MAX_DISTANCE = 128
ALPHA = (2 * DEPTH) ** 0.25
LN_EPS = 1e-5

LANES = 128
SUBLANES = 8
VMEM_LIMIT = 56 * 1024 * 1024

TQ = 256
TK = 512
QB = TQ // LANES
KB = TK // LANES
CH = 64
TM_FFN = 512
TF_FFN = 1408
TM_PROJ = TK
TM_CONV = 512
HALO = 32
CONV_ROWS = 32
NEG = -1e30
SEARCH_CAP = 1200


def _params(*sem):
    return pltpu.CompilerParams(dimension_semantics=sem, vmem_limit_bytes=VMEM_LIMIT)


def _layer_norm(v, g, b):
    mu = jnp.mean(v, axis=-1, keepdims=True)
    c = v - mu
    var = jnp.mean(c * c, axis=-1, keepdims=True)
    return c * lax.rsqrt(var + LN_EPS) * g + b


def _dot(a, b):
    return jnp.dot(a, b, preferred_element_type=F32)


def _dot_nt(a, b):
    return lax.dot_general(a, b, (((1,), (1,)), ((), ())), preferred_element_type=F32)


def _ffn_kernel(x_ref, wg_ref, wu_ref, wo_ref, g_ref, b_ref, o_ref, xb_ref, acc_ref):
    j = pl.program_id(1)

    @pl.when(j == 0)
    def _():
        xb_ref[...] = x_ref[...].astype(BF16)
        acc_ref[...] = jnp.zeros_like(acc_ref)

    xb = xb_ref[...]
    a = _dot(xb, wg_ref[...])
    u = _dot(xb, wu_ref[...])
    h = (a * jax.nn.sigmoid(a) * u).astype(BF16)
    acc_ref[...] += _dot(h, wo_ref[...])

    @pl.when(j == pl.num_programs(1) - 1)
    def _():
        y = ALPHA * x_ref[...] + 0.5 * acc_ref[...]
        o_ref[...] = _layer_norm(y, g_ref[...], b_ref[...])


def _ffn(x, w_in, w_out, g, b):
    t, d = x.shape
    f = w_out.shape[0]
    nf = f // TF_FFN
    return pl.pallas_call(
        _ffn_kernel,
        grid=(t // TM_FFN, nf),
        in_specs=[
            pl.BlockSpec((TM_FFN, d), lambda i, j: (i, 0)),
            pl.BlockSpec((d, TF_FFN), lambda i, j: (0, j)),
            pl.BlockSpec((d, TF_FFN), lambda i, j: (0, j + nf)),
            pl.BlockSpec((TF_FFN, d), lambda i, j: (j, 0)),
            pl.BlockSpec((1, d), lambda i, j: (0, 0)),
            pl.BlockSpec((1, d), lambda i, j: (0, 0)),
        ],
        out_specs=pl.BlockSpec((TM_FFN, d), lambda i, j: (i, 0)),
        out_shape=jax.ShapeDtypeStruct((t, d), F32),
        scratch_shapes=[pltpu.VMEM((TM_FFN, d), BF16), pltpu.VMEM((TM_FFN, d), F32)],
        compiler_params=_params("parallel", "arbitrary"),
        name="ffn",
    )(x, w_in, w_in, w_out, g, b)


_W3 = 3 * HA * DA
_C_IQ = _W3
_C_IK = _C_IQ + HI * DI
_C_IW = _C_IK + LANES
_C_B = _C_IW + LANES
_C_END = _C_B + 3 * HB * 2 * DB


def _mix_proj_kernel(x_ref, w_ref, ikg_ref, ikb_ref,
                     qa_ref, ka_ref, vat_ref, iq_ref, ik_ref, iwt_ref, qb_ref, kb_ref, vbt_ref):
    xb = x_ref[...].astype(BF16)
    hd = HA * DA
    qa_ref[...] = (_dot(xb, w_ref[:, 0:hd]) * (DA ** -0.5)).astype(BF16)
    ka_ref[...] = _dot(xb, w_ref[:, hd:2 * hd]).astype(BF16)
    vat_ref[0] = _dot(xb, w_ref[:, 2 * hd:3 * hd]).T.astype(BF16)
    iq_ref[...] = (_dot(xb, w_ref[:, _C_IQ:_C_IK]) * (DI ** -0.5)).astype(BF16)
    ik2 = _dot(xb, w_ref[:, _C_IK:_C_IW])
    ik_ref[...] = _layer_norm(ik2, ikg_ref[...], ikb_ref[...]).astype(BF16)
    iw = _dot(xb, w_ref[:, _C_IW:_C_B]) * (HI ** -0.5)
    iwt_ref[...] = iw.T[0:HI, :]
    hb = HB * 2 * DB
    qb_ref[...] = (_dot(xb, w_ref[:, _C_B:_C_B + hb]) * (DB ** -0.5)).astype(BF16)
    kb_ref[...] = _dot(xb, w_ref[:, _C_B + hb:_C_B + 2 * hb]).astype(BF16)
    vbt_ref[0] = _dot(xb, w_ref[:, _C_B + 2 * hb:_C_END]).T.astype(BF16)


def _mix_proj(x, w, ikg2, ikb2):
    t, d = x.shape
    nt = t // TM_PROJ
    row = lambda i: (i, 0)
    const = lambda i: (0, 0)
    tile = lambda i: (i, 0, 0)
    wide = lambda n, dt: jax.ShapeDtypeStruct((t, n), dt)
    hd, hb = HA * DA, HB * 2 * DB
    return pl.pallas_call(
        _mix_proj_kernel,
        grid=(nt,),
        in_specs=[
            pl.BlockSpec((TM_PROJ, d), row),
            pl.BlockSpec((d, _C_END), const),
            pl.BlockSpec((1, LANES), const),
            pl.BlockSpec((1, LANES), const),
        ],
        out_specs=[
            pl.BlockSpec((TM_PROJ, hd), row), pl.BlockSpec((TM_PROJ, hd), row),
            pl.BlockSpec((1, hd, TM_PROJ), tile), pl.BlockSpec((TM_PROJ, HI * DI), row),
            pl.BlockSpec((TM_PROJ, LANES), row), pl.BlockSpec((HI, TM_PROJ), lambda i: (0, i)),
            pl.BlockSpec((TM_PROJ, hb), row), pl.BlockSpec((TM_PROJ, hb), row),
            pl.BlockSpec((1, hb, TM_PROJ), tile),
        ],
        out_shape=[wide(hd, BF16), wide(hd, BF16), jax.ShapeDtypeStruct((nt, hd, TM_PROJ), BF16),
                   wide(HI * DI, BF16), wide(LANES, BF16), jax.ShapeDtypeStruct((HI, t), F32),
                   wide(hb, BF16), wide(hb, BF16), jax.ShapeDtypeStruct((nt, hb, TM_PROJ), BF16)],
        compiler_params=_params("parallel"),
        name="mix_proj",
    )(x, w, ikg2, ikb2)


def _bias_kernel(tbl_ref, o_ref):
    nh = o_ref.shape[0]
    kk = lax.broadcasted_iota(jnp.int32, (LANES, LANES), 0)
    qq = lax.broadcasted_iota(jnp.int32, (LANES, LANES), 1)
    max_exact = NUM_BUCKETS // 2
    for blk in range(2):
        n = jnp.maximum(qq - kk + blk * LANES, 0)
        nf = jnp.maximum(n, 1).astype(F32)
        large = max_exact + (jnp.log(nf / max_exact) / math.log(MAX_DISTANCE / max_exact)
                             * (NUM_BUCKETS - max_exact)).astype(jnp.int32)
        large = jnp.minimum(large, NUM_BUCKETS - 1)
        bucket = jnp.where(n < max_exact, n, large)
        for h in range(nh):
            acc = jnp.zeros((LANES, LANES), F32)
            for k in range(NUM_BUCKETS):
                acc = jnp.where(bucket == k, tbl_ref[h, k], acc)
            o_ref[h, blk] = acc - tbl_ref[h, NUM_BUCKETS - 1]


def _bias_tiles(rel_bias_t):
    nh = rel_bias_t.shape[0]
    return pl.pallas_call(
        _bias_kernel,
        in_specs=[pl.BlockSpec(memory_space=pltpu.SMEM)],
        out_shape=jax.ShapeDtypeStruct((nh, 2, LANES, LANES), F32),
        name="bias_tiles",
    )(rel_bias_t)


def _half_masked(x, upper):
    lane = lax.broadcasted_iota(jnp.int32, x.shape, 1)
    keep = (lane >= LANES // 2) if upper else (lane < LANES // 2)
    return jnp.where(keep, x, jnp.zeros_like(x))


def _near_bias_t(bias_ref, head, i, j, cc, r0):
    b0 = bias_ref[head, 0, r0:r0 + CH, :]
    b1 = bias_ref[head, 1, r0:r0 + CH, :]
    cols = []
    for qq in range(QB):
        delta = (i * QB + qq) - (j * KB + cc)
        cols.append(jnp.where(delta == 0, b0, jnp.where(delta == 1, b1, jnp.zeros_like(b0))))
    return jnp.concatenate(cols, axis=1)


def _fold8(x, op):
    return op(x.reshape(CH // SUBLANES, SUBLANES, x.shape[1]), axis=0)


def _flash_scores(par, maps, k_tile, mask_bias, near_bias, qm_ref, s_ref, m_ref, c_ref, only=None):
    nmap = len(maps)
    cur = par * nmap
    prv = nmap - cur
    nch = LANES // CH
    for s in (range(nmap) if only is None else (only,)):
        pair, _, head = maps[s]
        st = _dot_nt(k_tile(pair), qm_ref[s])
        mx = jnp.full((SUBLANES, TQ), NEG, F32)
        for cc in range(KB):
            for hh in range(nch):
                r0 = cc * LANES + hh * CH
                piece = st[r0:r0 + CH]
                if mask_bias is not None:
                    piece = piece + mask_bias(r0)
                if near_bias is not None:
                    piece = piece + near_bias(head, cc, hh * CH)
                s_ref[cur + s, r0:r0 + CH, :] = piece
                mx = jnp.maximum(mx, _fold8(piece, jnp.max))
        m_prev = m_ref[prv + s]
        m_next = jnp.maximum(m_prev, jnp.max(mx, axis=0, keepdims=True))
        c_ref[cur + s] = jnp.exp(m_prev - m_next)
        m_ref[cur + s] = m_next


def _flash_values(par, maps, vt_tile, s_ref, p_ref, m_ref, l_ref, c_ref, acc_ref, only=None):
    nmap = len(maps)
    cur = par * nmap
    for s in (range(nmap) if only is None else (only,)):
        m_new = m_ref[cur + s]
        sm = jnp.zeros((SUBLANES, TQ), F32)
        for r0 in range(0, TK, CH):
            p = jnp.exp(s_ref[cur + s, r0:r0 + CH, :] - m_new)
            sm = sm + _fold8(p, jnp.sum)
            p_ref[s, r0:r0 + CH, :] = p.astype(BF16)
        l_ref[s] = c_ref[cur + s] * l_ref[s] + jnp.sum(sm, axis=0, keepdims=True)
        acc_ref[s] = c_ref[cur + s] * acc_ref[s] + _dot(vt_tile(maps[s][1]), p_ref[s])


def _flash_sweep(jd, nmap, prep_far, scores_far, scores_any, values):
    scores_any(0, 0)

    def body(k, c):
        j = 2 * k
        for half in range(2):
            prep_far(j + half + 1, 1 - half)
            for s in range(nmap):
                scores_far(j + half + 1, 1 - half, s)
                values(j + half, half, s)
        return c

    npair = jnp.maximum(jd - 2, 0) // 2
    lax.fori_loop(0, npair, body, 0)
    t = 2 * npair
    values(t, 0, None)
    for r in range(1, 4):
        @pl.when(jd - t >= r)
        def _(r=r):
            scores_any(t + r, r % 2)
            values(t + r, r % 2, None)


def _flash_init(m_ref, l_ref, acc_ref):
    m_ref[...] = jnp.full(m_ref.shape, NEG, F32)
    l_ref[...] = jnp.zeros_like(l_ref)
    acc_ref[...] = jnp.zeros_like(acc_ref)


def _flash_scratch(nmap):
    return [
        pltpu.VMEM((nmap, TQ, LANES), BF16),
        pltpu.VMEM((2 * nmap, TK, TQ), F32),
        pltpu.VMEM((nmap, TK, TQ), BF16),
        pltpu.VMEM((2 * nmap, 1, TQ), F32),
        pltpu.VMEM((nmap, 1, TQ), F32),
        pltpu.VMEM((2 * nmap, 1, TQ), F32),
        pltpu.VMEM((nmap, LANES, TQ), F32),
        pltpu.VMEM((2, TK, TQ), F32),
    ]


def _dsa_kernel(q_ref, k_ref, vt_ref, iq_ref, ik_ref, iwt_ref, bias_ref, o_ref,
                sc_ref, qm_ref, s_ref, p_ref, m_ref, l_ref, c_ref, acc_ref, mb_ref, *, top_k):
    i = pl.program_id(1)
    t0 = i * TQ
    jd = t0 // TK
    ntile = jd + 1
    qpos = t0 + lax.broadcasted_iota(jnp.int32, (1, TQ), 1)

    iq = iq_ref[...]
    for h in range(HI):
        qm_ref[h] = _half_masked(iq[:, (h // 2) * LANES:(h // 2 + 1) * LANES], h % 2 == 1)
    iwt = iwt_ref[...]

    def score_tile(j, carry, diag):
        rmax, rmin = carry
        ik_t = ik_ref[pl.ds(pl.multiple_of(j * TK, TK), TK), :]
        for hp in range(HI // 2):
            da = _dot_nt(ik_t, qm_ref[2 * hp])
            db = _dot_nt(ik_t, qm_ref[2 * hp + 1])
            wa = iwt[2 * hp:2 * hp + 1, :]
            wb = iwt[2 * hp + 1:2 * hp + 2, :]
            for r0 in range(0, TK, CH):
                piece = wa * jnp.maximum(da[r0:r0 + CH], 0.0) + wb * jnp.maximum(db[r0:r0 + CH], 0.0)
                if hp > 0:
                    piece = piece + sc_ref[j, r0:r0 + CH, :]
                if hp == HI // 2 - 1:
                    if diag:
                        kpos = j * TK + r0 + lax.broadcasted_iota(jnp.int32, (CH, TQ), 0)
                        valid = kpos <= qpos
                        rmax = jnp.maximum(rmax, _fold8(jnp.where(valid, piece, -jnp.inf), jnp.max))
                        rmin = jnp.minimum(rmin, _fold8(jnp.where(valid, piece, jnp.inf), jnp.min))
                        piece = jnp.where(valid, piece, -jnp.inf)
                    else:
                        rmax = jnp.maximum(rmax, _fold8(piece, jnp.max))
                        rmin = jnp.minimum(rmin, _fold8(piece, jnp.min))
                sc_ref[j, r0:r0 + CH, :] = piece
        return rmax, rmin

    carry = lax.fori_loop(
        0, jd, lambda j, c: score_tile(j, c, False),
        (jnp.full((SUBLANES, TQ), -jnp.inf, F32), jnp.full((SUBLANES, TQ), jnp.inf, F32)))
    rmax8, rmin8 = score_tile(jd, carry, True)
    rmax = jnp.max(rmax8, axis=0, keepdims=True)
    rmin = jnp.min(rmin8, axis=0, keepdims=True)

    def row_count(pred):
        def body(j, cnt):
            for r0 in range(0, TK, CH):
                cnt = cnt + _fold8(jnp.where(pred(sc_ref[j, r0:r0 + CH, :]), 1.0, 0.0), jnp.sum)
            return cnt
        cnt = lax.fori_loop(0, ntile, body, jnp.zeros((SUBLANES, TQ), F32))
        return jnp.sum(cnt, axis=0, keepdims=True)

    def count_ge(x):
        return row_count(lambda t: t >= x)

    kf = float(top_k)
    nvalid = (qpos + 1).astype(F32)
    few = nvalid <= kf
    c_ge0 = count_ge(0.0)
    c_gt0 = row_count(lambda t: t > 0.0)
    at0 = jnp.logical_and(c_gt0 < kf, c_ge0 >= kf)
    pos = c_gt0 >= kf
    lo0 = jnp.where(few, -3.0e38, jnp.where(pos, 0.0, jnp.where(at0, 0.0, rmin)))
    hi0 = jnp.where(pos, rmax, 0.0)
    clo0 = jnp.where(few, kf, jnp.where(jnp.logical_or(pos, at0), c_ge0, nvalid))
    chi0 = jnp.where(pos, 1.0, c_ge0)
    done0 = jnp.where(jnp.logical_or(few, jnp.logical_or(at0, clo0 == kf)), 1.0, 0.0)

    def search_cond(st):
        it, _, _, _, _, done = st
        return jnp.logical_and(it < SEARCH_CAP, jnp.min(done) < 0.5)

    def search_body(st):
        it, lo, hi, clo, chi, done = st
        mid = lo + (hi - lo) * 0.5
        stuck = jnp.logical_or(mid <= lo, mid >= hi)
        frac = (clo - kf + 0.5) / jnp.maximum(clo - chi, 1.0)
        xi = lo + (hi - lo) * jnp.clip(frac, 0.0, 1.0)
        inside = jnp.logical_and(xi > lo, xi < hi)
        x = jnp.where(jnp.logical_and(inside, lax.rem(it, 3) != 2), xi, mid)
        cx = count_ge(x)
        ge = cx >= kf
        act = done < 0.5
        up = jnp.logical_and(act, ge)
        dn = jnp.logical_and(act, jnp.logical_not(ge))
        lo = jnp.where(up, x, lo)
        clo = jnp.where(up, cx, clo)
        hi = jnp.where(dn, x, hi)
        chi = jnp.where(dn, cx, chi)
        done = jnp.where(jnp.logical_or(stuck, clo == kf), 1.0, done)
        return it + 1, lo, hi, clo, chi, done

    _, theta, _, c_theta, _, _ = lax.while_loop(
        search_cond, search_body, (jnp.int32(0), lo0, hi0, clo0, chi0, done0))

    tie = jnp.where(jnp.logical_and(jnp.logical_not(few), c_theta > kf), 1.0, 0.0)

    @pl.when(jnp.max(tie) > 0.5)
    def _():
        n_eq = row_count(lambda t: t == theta)
        keep = kf - (c_theta - n_eq)
        ra = lax.broadcasted_iota(jnp.int32, (TK, TK), 0)
        ca = lax.broadcasted_iota(jnp.int32, (TK, TK), 1)
        tril = jnp.where(ca <= ra, 1.0, 0.0).astype(BF16)

        def body(j, before):
            t = sc_ref[j]
            eq = t == theta
            eqf = jnp.where(eq, 1.0, 0.0)
            rank = _dot(tril, eqf.astype(BF16)) + before
            drop = jnp.logical_and(jnp.logical_and(eq, tie > 0.5), rank > keep)
            sc_ref[j] = jnp.where(drop, -jnp.inf, t)
            return before + jnp.sum(eqf, axis=0, keepdims=True)

        lax.fori_loop(0, ntile, body, jnp.zeros((1, TQ), F32))

    _flash_init(m_ref, l_ref, acc_ref)
    q = q_ref[...]
    for h in range(HA):
        qm_ref[h] = _half_masked(q[:, (h // 2) * LANES:(h // 2 + 1) * LANES], h % 2 == 1)
    maps = [(h // 2, h // 2, h) for h in range(HA)]

    def prep(j, par):
        for r0 in range(0, TK, CH):
            mb_ref[par, r0:r0 + CH, :] = jnp.where(sc_ref[j, r0:r0 + CH, :] >= theta, 0.0, NEG)

    def scores(j, par, near, only=None):
        row = pl.ds(pl.multiple_of(j * TK, TK), TK)
        _flash_scores(
            par, maps,
            lambda p: k_ref[row, p * LANES:(p + 1) * LANES],
            lambda r0: mb_ref[par, r0:r0 + CH, :],
            (lambda h, cc, r: _near_bias_t(bias_ref, h, i, j, cc, r)) if near else None,
            qm_ref, s_ref, m_ref, c_ref, only)

    def scores_any(j, par):
        prep(j, par)
        scores(j, par, True)

    def values(j, par, only):
        _flash_values(par, maps, lambda p: vt_ref[j, p * LANES:(p + 1) * LANES, :],
                      s_ref, p_ref, m_ref, l_ref, c_ref, acc_ref, only)

    _flash_sweep(jd, HA, prep, lambda j, par, s: scores(j, par, False, s), scores_any, values)

    sub = lax.broadcasted_iota(jnp.int32, (LANES, TQ), 0)
    for p in range(HA // 2):
        o_lo = acc_ref[2 * p] / l_ref[2 * p]
        o_hi = acc_ref[2 * p + 1] / l_ref[2 * p + 1]
        o_t = jnp.where(sub < LANES // 2, o_lo, o_hi)
        o_ref[:, p * LANES:(p + 1) * LANES] = o_t.T.astype(BF16)


def _dsa(q, k, vt, iq, ik2, iwt, bias, batch, seq):
    top_k = min(TOPK_MAX, seq // 4)
    nq = seq // TQ
    nt = seq // TK
    hd = HA * DA
    qrow = lambda b, i: (b * nq + i, 0)
    whole = lambda b, i: (b, 0)
    once = pl.Buffered(1)
    return pl.pallas_call(
        functools.partial(_dsa_kernel, top_k=top_k),
        grid=(batch, nq),
        in_specs=[
            pl.BlockSpec((TQ, hd), qrow),
            pl.BlockSpec((seq, hd), whole, pipeline_mode=once),
            pl.BlockSpec((nt, hd, TK), lambda b, i: (b, 0, 0), pipeline_mode=once),
            pl.BlockSpec((TQ, HI * DI), qrow),
            pl.BlockSpec((seq, LANES), whole, pipeline_mode=once),
            pl.BlockSpec((HI, TQ), lambda b, i: (0, b * nq + i)),
            pl.BlockSpec((HA, 2, LANES, LANES), lambda b, i: (0, 0, 0, 0)),
        ],
        out_specs=pl.BlockSpec((TQ, hd), qrow),
        out_shape=jax.ShapeDtypeStruct((batch * seq, hd), BF16),
        scratch_shapes=[pltpu.VMEM((nt, TK, TQ), F32)] + _flash_scratch(HA),
        compiler_params=_params("parallel", "arbitrary"),
        name="dsa",
    )(q, k, vt, iq, ik2, iwt, bias)


def _diff_kernel(q_ref, k_ref, vt_ref, lam_ref, g_ref, bias_ref, o_ref,
                 qm_ref, s_ref, p_ref, m_ref, l_ref, c_ref, acc_ref, mb_ref, *, lam_init):
    i = pl.program_id(1)
    t0 = i * TQ
    jd = t0 // TK
    _flash_init(m_ref, l_ref, acc_ref)
    q = q_ref[...]
    nmap = 2 * HB
    for s in range(nmap):
        qm_ref[s] = _half_masked(q[:, (s // 2) * LANES:(s // 2 + 1) * LANES], s % 2 == 1)
    maps = [(s // 2, s // 2, s // 2) for s in range(nmap)]
    qpos = t0 + lax.broadcasted_iota(jnp.int32, (1, TQ), 1)

    def scores(j, par, general, only=None):
        row = pl.ds(pl.multiple_of(j * TK, TK), TK)
        if general:
            for r0 in range(0, TK, CH):
                kpos = j * TK + r0 + lax.broadcasted_iota(jnp.int32, (CH, TQ), 0)
                mb_ref[par, r0:r0 + CH, :] = jnp.where(kpos <= qpos, 0.0, NEG)
        _flash_scores(
            par, maps,
            lambda p: k_ref[row, p * LANES:(p + 1) * LANES],
            (lambda r0: mb_ref[par, r0:r0 + CH, :]) if general else None,
            (lambda h, cc, r: _near_bias_t(bias_ref, h, i, j, cc, r)) if general else None,
            qm_ref, s_ref, m_ref, c_ref, only)

    def values(j, par, only):
        _flash_values(par, maps, lambda p: vt_ref[j, p * LANES:(p + 1) * LANES, :],
                      s_ref, p_ref, m_ref, l_ref, c_ref, acc_ref, only)

    _flash_sweep(jd, nmap, lambda j, par: None, lambda j, par, s: scores(j, par, False, s),
                 lambda j, par: scores(j, par, True), values)

    lam = lam_ref[...]
    lam_full = (jnp.exp(jnp.sum(lam[0:1] * lam[1:2], axis=1, keepdims=True))
                - jnp.exp(jnp.sum(lam[2:3] * lam[3:4], axis=1, keepdims=True)) + lam_init)
    for h in range(HB):
        o0 = acc_ref[2 * h] / l_ref[2 * h]
        o1 = acc_ref[2 * h + 1] / l_ref[2 * h + 1]
        o = (o0 - lam_full * o1).T
        y = o * lax.rsqrt(jnp.mean(o * o, axis=-1, keepdims=True) + LN_EPS) * g_ref[...]
        o_ref[:, h * LANES:(h + 1) * LANES] = (y * (1.0 - lam_init)).astype(BF16)


def _diff(q, k, vt, lam, g, bias, batch, seq, lam_init):
    nq = seq // TQ
    nt = seq // TK
    hb = HB * 2 * DB
    qrow = lambda b, i: (b * nq + i, 0)
    whole = lambda b, i: (b, 0)
    once = pl.Buffered(1)
    return pl.pallas_call(
        functools.partial(_diff_kernel, lam_init=lam_init),
        grid=(batch, nq),
        in_specs=[
            pl.BlockSpec((TQ, hb), qrow),
            pl.BlockSpec((seq, hb), whole, pipeline_mode=once),
            pl.BlockSpec((nt, hb, TK), lambda b, i: (b, 0, 0), pipeline_mode=once),
            pl.BlockSpec((4, DB), lambda b, i: (0, 0)),
            pl.BlockSpec((1, 2 * DB), lambda b, i: (0, 0)),
            pl.BlockSpec((HB, 2, LANES, LANES), lambda b, i: (0, 0, 0, 0)),
        ],
        out_specs=pl.BlockSpec((TQ, hb), qrow),
        out_shape=jax.ShapeDtypeStruct((batch * seq, hb), BF16),
        scratch_shapes=_flash_scratch(2 * HB),
        compiler_params=_params("parallel", "arbitrary"),
        name="diff",
    )(q, k, vt, lam, g, bias)


def _mix_out_kernel(oa_ref, ob_ref, wa_ref, wb_ref, x_ref, g_ref, b_ref, o_ref):
    m = _dot(oa_ref[...], wa_ref[...]) + _dot(ob_ref[...], wb_ref[...])
    o_ref[...] = _layer_norm(ALPHA * x_ref[...] + m, g_ref[...], b_ref[...])


def _mix_out(oa, ob, wa, wb, x, g, b):
    t, d = x.shape
    row = lambda i: (i, 0)
    const = lambda i: (0, 0)
    return pl.pallas_call(
        _mix_out_kernel,
        grid=(t // TM_PROJ,),
        in_specs=[
            pl.BlockSpec((TM_PROJ, oa.shape[1]), row),
            pl.BlockSpec((TM_PROJ, ob.shape[1]), row),
            pl.BlockSpec(wa.shape, const),
            pl.BlockSpec(wb.shape, const),
            pl.BlockSpec((TM_PROJ, d), row),
            pl.BlockSpec((1, d), const),
            pl.BlockSpec((1, d), const),
        ],
        out_specs=pl.BlockSpec((TM_PROJ, d), row),
        out_shape=jax.ShapeDtypeStruct((t, d), F32),
        compiler_params=_params("parallel"),
        name="mix_out",
    )(oa, ob, wa, wb, x, g, b)


def _conv_in_kernel(x_ref, w_ref, o_ref):
    xb = x_ref[...].astype(BF16)
    d = o_ref.shape[1]
    a = _dot(xb, w_ref[:, 0:d])
    gate = _dot(xb, w_ref[:, d:2 * d])
    o_ref[...] = a * jax.nn.sigmoid(gate)


def _conv_in(x, w):
    t, d = x.shape
    return pl.pallas_call(
        _conv_in_kernel,
        grid=(t // TM_CONV,),
        in_specs=[pl.BlockSpec((TM_CONV, d), lambda i: (i, 0)),
                  pl.BlockSpec((d, 2 * d), lambda i: (0, 0))],
        out_specs=pl.BlockSpec((TM_CONV, d), lambda i: (i, 0)),
        out_shape=jax.ShapeDtypeStruct((t, d), F32),
        compiler_params=_params("parallel"),
        name="conv_in",
    )(x, w)


def _conv_out_kernel(h_ref, halo_ref, dw_ref, dwb_ref, cg_ref, cb_ref, w_ref, x_ref, g_ref, b_ref,
                     o_ref, buf_ref, cv_ref):
    i = pl.program_id(1)
    buf_ref[0:HALO, :] = jnp.where(i > 0, halo_ref[...], 0.0)
    buf_ref[HALO:, :] = h_ref[...]
    off = HALO - (CONV_W - 1)

    def chunk(c, carry):
        r0 = pl.multiple_of(c * CONV_ROWS, CONV_ROWS)
        win = buf_ref[pl.ds(r0, CONV_ROWS + HALO), :]
        acc = jnp.broadcast_to(dwb_ref[...], (CONV_ROWS, dwb_ref.shape[1]))
        for sub in range(SUBLANES):
            shifted = win[sub:sub + CONV_ROWS + HALO - SUBLANES] if sub else win
            for k in range(CONV_W):
                if (off + k) % SUBLANES == sub:
                    a0 = off + k - sub
                    acc = acc + dw_ref[k:k + 1, :] * shifted[a0:a0 + CONV_ROWS]
        cv_ref[pl.ds(r0, CONV_ROWS), :] = acc
        return carry

    lax.fori_loop(0, TM_CONV // CONV_ROWS, chunk, 0)
    y = _layer_norm(cv_ref[...], cg_ref[...], cb_ref[...])
    u = (y * jax.nn.sigmoid(y)).astype(BF16)
    m = _dot(u, w_ref[...])
    o_ref[...] = _layer_norm(ALPHA * x_ref[...] + m, g_ref[...], b_ref[...])


def _conv_out(h, dw, dwb, cg, cb, w, x, g, b, batch, seq):
    t, d = x.shape
    nt = seq // TM_CONV
    per = TM_CONV // HALO
    row = lambda bb, i: (bb * nt + i, 0)
    halo = lambda bb, i: (jnp.maximum((bb * nt + i) * per - 1, 0), 0)
    const = lambda bb, i: (0, 0)
    return pl.pallas_call(
        _conv_out_kernel,
        grid=(batch, nt),
        in_specs=[
            pl.BlockSpec((TM_CONV, d), row),
            pl.BlockSpec((HALO, d), halo),
            pl.BlockSpec(dw.shape, const),
            pl.BlockSpec((1, d), const),
            pl.BlockSpec((1, d), const),
            pl.BlockSpec((1, d), const),
            pl.BlockSpec((d, d), const),
            pl.BlockSpec((TM_CONV, d), row),
            pl.BlockSpec((1, d), const),
            pl.BlockSpec((1, d), const),
        ],
        out_specs=pl.BlockSpec((TM_CONV, d), row),
        out_shape=jax.ShapeDtypeStruct((t, d), F32),
        scratch_shapes=[pltpu.VMEM((TM_CONV + HALO, d), F32), pltpu.VMEM((TM_CONV, d), F32)],
        compiler_params=_params("parallel", "arbitrary"),
        name="conv_out",
    )(h, h, dw, dwb, cg, cb, w, x, g, b)


def _attention_mixer(x, w_in, ikg, ikb, lam, subg, w_out, bias, g, b, batch, seq, layer_idx):
    c = [0]
    for n in (HA * DA, HA * DA, HA * DA, HI * DI, DI, HI, HB * 2 * DB, HB * 2 * DB, HB * 2 * DB):
        c.append(c[-1] + n)
    w_ik = w_in[:, c[4]:c[5]]
    w_iw = jnp.pad(w_in[:, c[5]:c[6]], ((0, 0), (0, LANES - HI)))
    w = jnp.concatenate([w_in[:, :c[4]], w_ik, w_ik, w_iw, w_in[:, c[6]:]], axis=1).astype(BF16)
    ikg2 = jnp.concatenate([ikg, ikg])[None]
    ikb2 = jnp.concatenate([ikb, ikb])[None]
    qa, ka, vat, iq, ik2, iwt, qb, kb, vbt = _mix_proj(x, w, ikg2, ikb2)
    o_a = _dsa(qa, ka, vat, iq, ik2, iwt, bias[:HA], batch, seq)
    lam_init = 0.8 - 0.6 * math.exp(-0.3 * layer_idx)
    o_b = _diff(qb, kb, vbt, lam, subg[None], bias[HA:], batch, seq, lam_init)
    wo = w_out.astype(BF16)
    return _mix_out(o_a, o_b, wo[:HA * DA], wo[HA * DA:], x, g, b)


def _conv_module(x, w_in, dw, dwb, cg, cb, w_out, g, b, batch, seq):
    h = _conv_in(x, w_in.astype(BF16))
    dwp = jnp.pad(dw, ((0, HALO - CONV_W), (0, 0)))
    return _conv_out(h, dwp, dwb[None], cg[None], cb[None], w_out.astype(BF16), x, g, b, batch, seq)


def kernel(x, ffn_in, ffn_out, ln_g, ln_b, rel_bias, mix_w_in, idx_k_g, idx_k_b, diff_lambda,
           diff_subln_g, mix_w_out, conv_w_in, conv_dw, conv_dw_b, conv_ln_g, conv_ln_b, conv_w_out):
    batch, seq, d = x.shape
    x = x.reshape(batch * seq, d)
    bias = _bias_tiles(rel_bias.T)
    ffn_in_b = ffn_in.astype(BF16)
    ffn_out_b = ffn_out.astype(BF16)
    for l in range(DEPTH):
        j = l // 2
        x = _ffn(x, ffn_in_b[l, 0], ffn_out_b[l, 0], ln_g[l, 0][None], ln_b[l, 0][None])
        if l % 2 == 0:
            x = _attention_mixer(x, mix_w_in[j], idx_k_g[j], idx_k_b[j], diff_lambda[j], diff_subln_g[j],
                                 mix_w_out[j], bias, ln_g[l, 1][None], ln_b[l, 1][None], batch, seq, l)
        else:
            x = _conv_module(x, conv_w_in[j], conv_dw[j], conv_dw_b[j], conv_ln_g[j], conv_ln_b[j],
                             conv_w_out[j], ln_g[l, 1][None], ln_b[l, 1][None], batch, seq)
        x = _ffn(x, ffn_in_b[l, 1], ffn_out_b[l, 1], ln_g[l, 2][None], ln_b[l, 2][None])
    return x.reshape(batch, seq, d)
```

```python
import functools
import math

import jax
import jax.numpy as jnp
from jax import lax
from jax.experimental import pallas as pl
from jax.experimental.pallas import tpu as pltpu

F32 = jnp.float32
BF16 = jnp.bfloat16

D_MODEL = 1024
DEPTH = 4
HA, DA = 8, 64
HI, DI = 8, 64
TOPK_MAX = 256
HB, DB = 4, 64
CONV_W = 31
D_FF = 2816
NUM_BUCKETS = 32
MAX_DISTANCE = 128
ALPHA = (2 * DEPTH) ** 0.25
LN_EPS = 1e-5

LANES = 128
SUBLANES = 8
BF16_ROWS = 16
VMEM_LIMIT = 56 * 1024 * 1024

TQ = 256
TK = 512
QB = TQ // LANES
KB = TK // LANES
CH = 64
TM_FFN = 512
TF_FFN = 1408
TM_PROJ = TK
TM_CONV = 512
HALO = 32
CONV_ROWS = 32
NEG = -1e30
LOG2E = math.log2(math.e)
SEARCH_CAP = 1200


def _params(*sem):
    return pltpu.CompilerParams(dimension_semantics=sem, vmem_limit_bytes=VMEM_LIMIT)


def _layer_norm(v, g, b):
    mu = jnp.mean(v, axis=-1, keepdims=True)
    c = v - mu
    var = jnp.mean(c * c, axis=-1, keepdims=True)
    return c * lax.rsqrt(var + LN_EPS) * g + b


def _dot(a, b):
    return jnp.dot(a, b, preferred_element_type=F32)


def _dot_nt(a, b):
    return lax.dot_general(a, b, (((1,), (1,)), ((), ())), preferred_element_type=F32)


def _ffn_kernel(x_ref, wg_ref, wu_ref, wo_ref, g_ref, b_ref, o_ref, xb_ref, acc_ref):
    j = pl.program_id(1)

    @pl.when(j == 0)
    def _():
        xb_ref[...] = x_ref[...].astype(BF16)
        acc_ref[...] = jnp.zeros_like(acc_ref)

    xb = xb_ref[...]
    a = _dot(xb, wg_ref[...])
    u = _dot(xb, wu_ref[...])
    h = (a * jax.nn.sigmoid(a) * u).astype(BF16)
    acc_ref[...] += _dot(h, wo_ref[...])

    @pl.when(j == pl.num_programs(1) - 1)
    def _():
        y = ALPHA * x_ref[...] + 0.5 * acc_ref[...]
        o_ref[...] = _layer_norm(y, g_ref[...], b_ref[...])


def _ffn(x, w_in, w_out, g, b):
    t, d = x.shape
    f = w_out.shape[0]
    nf = f // TF_FFN
    return pl.pallas_call(
        _ffn_kernel,
        grid=(t // TM_FFN, nf),
        in_specs=[
            pl.BlockSpec((TM_FFN, d), lambda i, j: (i, 0)),
            pl.BlockSpec((d, TF_FFN), lambda i, j: (0, j)),
            pl.BlockSpec((d, TF_FFN), lambda i, j: (0, j + nf)),
            pl.BlockSpec((TF_FFN, d), lambda i, j: (j, 0)),
            pl.BlockSpec((1, d), lambda i, j: (0, 0)),
            pl.BlockSpec((1, d), lambda i, j: (0, 0)),
        ],
        out_specs=pl.BlockSpec((TM_FFN, d), lambda i, j: (i, 0)),
        out_shape=jax.ShapeDtypeStruct((t, d), F32),
        scratch_shapes=[pltpu.VMEM((TM_FFN, d), BF16), pltpu.VMEM((TM_FFN, d), F32)],
        compiler_params=_params("parallel", "arbitrary"),
        name="ffn",
    )(x, w_in, w_in, w_out, g, b)


_W3 = 3 * HA * DA
_C_IQ = _W3
_C_IK = _C_IQ + HI * DI
_C_IW = _C_IK + LANES
_C_B = _C_IW + LANES
_C_END = _C_B + 3 * HB * 2 * DB


def _mix_proj_kernel(x_ref, w_ref, ikg_ref, ikb_ref,
                     qa_ref, ka_ref, vat_ref, iq_ref, ik_ref, iwt_ref, qb_ref, kb_ref, vbt_ref):
    xb = x_ref[...].astype(BF16)
    hd = HA * DA
    qa_ref[...] = (_dot(xb, w_ref[:, 0:hd]) * (DA ** -0.5 * LOG2E)).astype(BF16)
    ka_ref[...] = _dot(xb, w_ref[:, hd:2 * hd]).astype(BF16)
    vat_ref[0] = _dot(xb, w_ref[:, 2 * hd:3 * hd]).T.astype(BF16)
    iq_ref[...] = (_dot(xb, w_ref[:, _C_IQ:_C_IK]) * (DI ** -0.5)).astype(BF16)
    ik2 = _dot(xb, w_ref[:, _C_IK:_C_IW])
    ik_ref[...] = _layer_norm(ik2, ikg_ref[...], ikb_ref[...]).astype(BF16)
    iw = _dot(xb, w_ref[:, _C_IW:_C_B]) * (HI ** -0.5)
    iwt_ref[...] = iw.T[0:HI, :]
    hb = HB * 2 * DB
    qb_ref[...] = (_dot(xb, w_ref[:, _C_B:_C_B + hb]) * (DB ** -0.5 * LOG2E)).astype(BF16)
    kb_ref[...] = _dot(xb, w_ref[:, _C_B + hb:_C_B + 2 * hb]).astype(BF16)
    vbt_ref[0] = _dot(xb, w_ref[:, _C_B + 2 * hb:_C_END]).T.astype(BF16)


def _mix_proj(x, w, ikg2, ikb2):
    t, d = x.shape
    nt = t // TM_PROJ
    row = lambda i: (i, 0)
    const = lambda i: (0, 0)
    tile = lambda i: (i, 0, 0)
    wide = lambda n, dt: jax.ShapeDtypeStruct((t, n), dt)
    hd, hb = HA * DA, HB * 2 * DB
    return pl.pallas_call(
        _mix_proj_kernel,
        grid=(nt,),
        in_specs=[
            pl.BlockSpec((TM_PROJ, d), row),
            pl.BlockSpec((d, _C_END), const),
            pl.BlockSpec((1, LANES), const),
            pl.BlockSpec((1, LANES), const),
        ],
        out_specs=[
            pl.BlockSpec((TM_PROJ, hd), row), pl.BlockSpec((TM_PROJ, hd), row),
            pl.BlockSpec((1, hd, TM_PROJ), tile), pl.BlockSpec((TM_PROJ, HI * DI), row),
            pl.BlockSpec((TM_PROJ, LANES), row), pl.BlockSpec((HI, TM_PROJ), lambda i: (0, i)),
            pl.BlockSpec((TM_PROJ, hb), row), pl.BlockSpec((TM_PROJ, hb), row),
            pl.BlockSpec((1, hb, TM_PROJ), tile),
        ],
        out_shape=[wide(hd, BF16), wide(hd, BF16), jax.ShapeDtypeStruct((nt, hd, TM_PROJ), BF16),
                   wide(HI * DI, BF16), wide(LANES, BF16), jax.ShapeDtypeStruct((HI, t), F32),
                   wide(hb, BF16), wide(hb, BF16), jax.ShapeDtypeStruct((nt, hb, TM_PROJ), BF16)],
        compiler_params=_params("parallel"),
        name="mix_proj",
    )(x, w, ikg2, ikb2)


def _bias_kernel(tbl_ref, o_ref):
    nh = o_ref.shape[0]
    kk = lax.broadcasted_iota(jnp.int32, (LANES, LANES), 0)
    qq = lax.broadcasted_iota(jnp.int32, (LANES, LANES), 1)
    max_exact = NUM_BUCKETS // 2
    for blk in range(2):
        n = jnp.maximum(qq - kk + blk * LANES, 0)
        nf = jnp.maximum(n, 1).astype(F32)
        large = max_exact + (jnp.log(nf / max_exact) / math.log(MAX_DISTANCE / max_exact)
                             * (NUM_BUCKETS - max_exact)).astype(jnp.int32)
        large = jnp.minimum(large, NUM_BUCKETS - 1)
        bucket = jnp.where(n < max_exact, n, large)
        for h in range(nh):
            acc = jnp.zeros((LANES, LANES), F32)
            for k in range(NUM_BUCKETS):
                acc = jnp.where(bucket == k, tbl_ref[h, k], acc)
            o_ref[h, blk] = (acc - tbl_ref[h, NUM_BUCKETS - 1]) * LOG2E


def _bias_tiles(rel_bias_t):
    nh = rel_bias_t.shape[0]
    return pl.pallas_call(
        _bias_kernel,
        in_specs=[pl.BlockSpec(memory_space=pltpu.SMEM)],
        out_shape=jax.ShapeDtypeStruct((nh, 2, LANES, LANES), F32),
        name="bias_tiles",
    )(rel_bias_t)


def _half_masked(x, upper):
    lane = lax.broadcasted_iota(jnp.int32, x.shape, 1)
    keep = (lane >= LANES // 2) if upper else (lane < LANES // 2)
    return jnp.where(keep, x, jnp.zeros_like(x))


def _near_bias_t(bias_ref, head, i, j, cc, r0):
    b0 = bias_ref[head, 0, r0:r0 + CH, :]
    b1 = bias_ref[head, 1, r0:r0 + CH, :]
    cols = []
    for qq in range(QB):
        delta = (i * QB + qq) - (j * KB + cc)
        cols.append(jnp.where(delta == 0, b0, jnp.where(delta == 1, b1, jnp.zeros_like(b0))))
    return jnp.concatenate(cols, axis=1)


def _fold8(x, op):
    return op(x.reshape(CH // SUBLANES, SUBLANES, x.shape[1]), axis=0)


def _flash_scores(par, maps, k_tile, mask_bias, near_bias, qm_ref, s_ref, m_ref, c_ref, only=None):
    nmap = len(maps)
    cur = par * nmap
    prv = nmap - cur
    nch = LANES // CH
    for s in (range(nmap) if only is None else (only,)):
        pair, _, head = maps[s]
        st = _dot_nt(k_tile(pair), qm_ref[s])
        mx = jnp.full((SUBLANES, TQ), NEG, F32)
        for cc in range(KB):
            for hh in range(nch):
                r0 = cc * LANES + hh * CH
                piece = st[r0:r0 + CH]
                if mask_bias is not None:
                    piece = piece + mask_bias(r0)
                if near_bias is not None:
                    piece = piece + near_bias(head, cc, hh * CH)
                s_ref[cur + s, r0:r0 + CH, :] = piece
                mx = jnp.maximum(mx, _fold8(piece, jnp.max))
        m_prev = m_ref[prv + s]
        m_next = jnp.maximum(m_prev, jnp.max(mx, axis=0, keepdims=True))
        c_ref[cur + s] = jnp.exp2(m_prev - m_next)
        m_ref[cur + s] = m_next


def _flash_values(par, maps, vt_tile, s_ref, p_ref, m_ref, l_ref, c_ref, acc_ref, only=None):
    nmap = len(maps)
    cur = par * nmap
    for s in (range(nmap) if only is None else (only,)):
        m_new = m_ref[cur + s]
        for r0 in range(0, TK, CH):
            p_ref[s, r0:r0 + CH, :] = jnp.exp2(s_ref[cur + s, r0:r0 + CH, :] - m_new).astype(BF16)
        lhs = jnp.concatenate([vt_tile(maps[s][1]), jnp.ones((BF16_ROWS, TK), BF16)], axis=0)
        pv = _dot(lhs, p_ref[s])
        l_ref[s] = c_ref[cur + s] * l_ref[s] + pv[LANES:LANES + 1]
        acc_ref[s] = c_ref[cur + s] * acc_ref[s] + pv[:LANES]


def _flash_sweep(jd, nmap, prep_far, scores_far, prep_any, scores_any, values):
    def step(j_next, par_next, prep, scores):
        prep(j_next, par_next)
        for s in range(nmap):
            scores(j_next, par_next, s)
            values(j_next - 1, 1 - par_next, s)

    prep_any(0, 0)
    for s in range(nmap):
        scores_any(0, 0, s)

    def body(k, c):
        step(2 * k + 1, 1, prep_far, scores_far)
        step(2 * k + 2, 0, prep_far, scores_far)
        return c

    npair = jnp.maximum(jd - 2, 0) // 2
    lax.fori_loop(0, npair, body, 0)
    t = 2 * npair
    left = jd - t
    for r in range(1, 4):
        @pl.when(left >= r)
        def _(r=r):
            step(t + r, r % 2, prep_any, scores_any)

    for par in range(2):
        @pl.when(lax.rem(left, 2) == par)
        def _(par=par):
            values(jd, par, None)


def _flash_init(m_ref, l_ref, acc_ref):
    m_ref[...] = jnp.full(m_ref.shape, NEG, F32)
    l_ref[...] = jnp.zeros_like(l_ref)
    acc_ref[...] = jnp.zeros_like(acc_ref)


def _flash_scratch(nmap):
    return [
        pltpu.VMEM((nmap, TQ, LANES), BF16),
        pltpu.VMEM((2 * nmap, TK, TQ), F32),
        pltpu.VMEM((nmap, TK, TQ), BF16),
        pltpu.VMEM((2 * nmap, 1, TQ), F32),
        pltpu.VMEM((nmap, 1, TQ), F32),
        pltpu.VMEM((2 * nmap, 1, TQ), F32),
        pltpu.VMEM((nmap, LANES, TQ), F32),
        pltpu.VMEM((2, TK, TQ), F32),
    ]


def _dsa_kernel(q_ref, k_ref, vt_ref, iq_ref, ik_ref, iwt_ref, bias_ref, o_ref,
                sc_ref, qm_ref, s_ref, p_ref, m_ref, l_ref, c_ref, acc_ref, mb_ref, *, top_k):
    i = pl.program_id(1)
    t0 = i * TQ
    jd = t0 // TK
    ntile = jd + 1
    qpos = t0 + lax.broadcasted_iota(jnp.int32, (1, TQ), 1)

    iq = iq_ref[...]
    for h in range(HI):
        qm_ref[h] = _half_masked(iq[:, (h // 2) * LANES:(h // 2 + 1) * LANES], h % 2 == 1)
    iwt = iwt_ref[...]

    def score_tile(j, carry, diag):
        rmax, rmin = carry
        ik_t = ik_ref[pl.ds(pl.multiple_of(j * TK, TK), TK), :]
        for hp in range(HI // 2):
            da = _dot_nt(ik_t, qm_ref[2 * hp])
            db = _dot_nt(ik_t, qm_ref[2 * hp + 1])
            wa = iwt[2 * hp:2 * hp + 1, :]
            wb = iwt[2 * hp + 1:2 * hp + 2, :]
            for r0 in range(0, TK, CH):
                piece = wa * jnp.maximum(da[r0:r0 + CH], 0.0) + wb * jnp.maximum(db[r0:r0 + CH], 0.0)
                if hp > 0:
                    piece = piece + sc_ref[j, r0:r0 + CH, :]
                if hp == HI // 2 - 1:
                    if diag:
                        kpos = j * TK + r0 + lax.broadcasted_iota(jnp.int32, (CH, TQ), 0)
                        valid = kpos <= qpos
                        rmax = jnp.maximum(rmax, _fold8(jnp.where(valid, piece, -jnp.inf), jnp.max))
                        rmin = jnp.minimum(rmin, _fold8(jnp.where(valid, piece, jnp.inf), jnp.min))
                        piece = jnp.where(valid, piece, -jnp.inf)
                    else:
                        rmax = jnp.maximum(rmax, _fold8(piece, jnp.max))
                        rmin = jnp.minimum(rmin, _fold8(piece, jnp.min))
                sc_ref[j, r0:r0 + CH, :] = piece
        return rmax, rmin

    carry = lax.fori_loop(
        0, jd, lambda j, c: score_tile(j, c, False),
        (jnp.full((SUBLANES, TQ), -jnp.inf, F32), jnp.full((SUBLANES, TQ), jnp.inf, F32)))
    rmax8, rmin8 = score_tile(jd, carry, True)
    rmax = jnp.max(rmax8, axis=0, keepdims=True)
    rmin = jnp.min(rmin8, axis=0, keepdims=True)

    def row_count(pred):
        def body(j, cnt):
            for r0 in range(0, TK, CH):
                cnt = cnt + _fold8(jnp.where(pred(sc_ref[j, r0:r0 + CH, :]), 1.0, 0.0), jnp.sum)
            return cnt
        cnt = lax.fori_loop(0, ntile, body, jnp.zeros((SUBLANES, TQ), F32))
        return jnp.sum(cnt, axis=0, keepdims=True)

    def count_ge(x):
        return row_count(lambda t: t >= x)

    kf = float(top_k)
    nvalid = (qpos + 1).astype(F32)
    few = nvalid <= kf
    c_ge0 = count_ge(0.0)
    c_gt0 = row_count(lambda t: t > 0.0)
    at0 = jnp.logical_and(c_gt0 < kf, c_ge0 >= kf)
    pos = c_gt0 >= kf
    lo0 = jnp.where(few, -3.0e38, jnp.where(pos, 0.0, jnp.where(at0, 0.0, rmin)))
    hi0 = jnp.where(pos, rmax * 1.000001 + 1e-37, 0.0)
    clo0 = jnp.where(few, kf, jnp.where(jnp.logical_or(pos, at0), c_ge0, nvalid))
    chi0 = jnp.where(pos, 0.0, c_ge0)

    def settled(clo, chi):
        return jnp.logical_or(clo == kf, kf - chi == 1.0)

    done0 = jnp.where(jnp.logical_or(jnp.logical_or(few, at0), settled(clo0, chi0)), 1.0, 0.0)

    def search_cond(st):
        it, _, _, _, _, done = st
        return jnp.logical_and(it < SEARCH_CAP, jnp.min(done) < 0.5)

    def search_body(st):
        it, lo, hi, clo, chi, done = st
        mid = lo + (hi - lo) * 0.5
        stuck = jnp.logical_or(mid <= lo, mid >= hi)
        cx = count_ge(mid)
        ge = cx >= kf
        act = done < 0.5
        up = jnp.logical_and(act, ge)
        dn = jnp.logical_and(act, jnp.logical_not(ge))
        lo = jnp.where(up, mid, lo)
        clo = jnp.where(up, cx, clo)
        hi = jnp.where(dn, mid, hi)
        chi = jnp.where(dn, cx, chi)
        done = jnp.where(jnp.logical_or(stuck, settled(clo, chi)), 1.0, done)
        return it + 1, lo, hi, clo, chi, done

    _, lo, hi, clo, chi, _ = lax.while_loop(
        search_cond, search_body, (jnp.int32(0), lo0, hi0, clo0, chi0, done0))

    def below_hi_max(j, mx):
        for r0 in range(0, TK, CH):
            t = sc_ref[j, r0:r0 + CH, :]
            mx = jnp.maximum(mx, _fold8(jnp.where(t < hi, t, -jnp.inf), jnp.max))
        return mx

    top_below = jnp.max(lax.fori_loop(0, ntile, below_hi_max, jnp.full((SUBLANES, TQ), -jnp.inf, F32)),
                        axis=0, keepdims=True)
    last_one = jnp.logical_and(jnp.logical_and(jnp.logical_not(few), clo != kf), kf - chi == 1.0)
    theta = jnp.where(last_one, top_below, lo)
    c_theta = count_ge(theta)

    tie = jnp.where(jnp.logical_and(jnp.logical_not(few), c_theta > kf), 1.0, 0.0)

    @pl.when(jnp.max(tie) > 0.5)
    def _():
        n_eq = row_count(lambda t: t == theta)
        keep = kf - (c_theta - n_eq)
        ra = lax.broadcasted_iota(jnp.int32, (TK, TK), 0)
        ca = lax.broadcasted_iota(jnp.int32, (TK, TK), 1)
        tril = jnp.where(ca <= ra, 1.0, 0.0).astype(BF16)

        def body(j, before):
            t = sc_ref[j]
            eq = t == theta
            eqf = jnp.where(eq, 1.0, 0.0)
            rank = _dot(tril, eqf.astype(BF16)) + before
            drop = jnp.logical_and(jnp.logical_and(eq, tie > 0.5), rank > keep)
            sc_ref[j] = jnp.where(drop, -jnp.inf, t)
            return before + jnp.sum(eqf, axis=0, keepdims=True)

        lax.fori_loop(0, ntile, body, jnp.zeros((1, TQ), F32))

    _flash_init(m_ref, l_ref, acc_ref)
    q = q_ref[...]
    for h in range(HA):
        qm_ref[h] = _half_masked(q[:, (h // 2) * LANES:(h // 2 + 1) * LANES], h % 2 == 1)
    maps = [(h // 2, h // 2, h) for h in range(HA)]

    def prep(j, par):
        for r0 in range(0, TK, CH):
            mb_ref[par, r0:r0 + CH, :] = jnp.where(sc_ref[j, r0:r0 + CH, :] >= theta, 0.0, NEG)

    def scores(j, par, near, only=None):
        row = pl.ds(pl.multiple_of(j * TK, TK), TK)
        _flash_scores(
            par, maps,
            lambda p: k_ref[row, p * LANES:(p + 1) * LANES],
            lambda r0: mb_ref[par, r0:r0 + CH, :],
            (lambda h, cc, r: _near_bias_t(bias_ref, h, i, j, cc, r)) if near else None,
            qm_ref, s_ref, m_ref, c_ref, only)

    def values(j, par, only):
        _flash_values(par, maps, lambda p: vt_ref[j, p * LANES:(p + 1) * LANES, :],
                      s_ref, p_ref, m_ref, l_ref, c_ref, acc_ref, only)

    _flash_sweep(jd, HA, prep, lambda j, par, s: scores(j, par, False, s),
                 prep, lambda j, par, s: scores(j, par, True, s), values)

    sub = lax.broadcasted_iota(jnp.int32, (LANES, TQ), 0)
    for p in range(HA // 2):
        o_lo = acc_ref[2 * p] / l_ref[2 * p]
        o_hi = acc_ref[2 * p + 1] / l_ref[2 * p + 1]
        o_t = jnp.where(sub < LANES // 2, o_lo, o_hi)
        o_ref[:, p * LANES:(p + 1) * LANES] = o_t.T.astype(BF16)


def _dsa(q, k, vt, iq, ik2, iwt, bias, batch, seq):
    top_k = min(TOPK_MAX, seq // 4)
    nq = seq // TQ
    nt = seq // TK
    hd = HA * DA
    qrow = lambda b, i: (b * nq + i, 0)
    whole = lambda b, i: (b, 0)
    once = pl.Buffered(1)
    return pl.pallas_call(
        functools.partial(_dsa_kernel, top_k=top_k),
        grid=(batch, nq),
        in_specs=[
            pl.BlockSpec((TQ, hd), qrow),
            pl.BlockSpec((seq, hd), whole, pipeline_mode=once),
            pl.BlockSpec((nt, hd, TK), lambda b, i: (b, 0, 0), pipeline_mode=once),
            pl.BlockSpec((TQ, HI * DI), qrow),
            pl.BlockSpec((seq, LANES), whole, pipeline_mode=once),
            pl.BlockSpec((HI, TQ), lambda b, i: (0, b * nq + i)),
            pl.BlockSpec((HA, 2, LANES, LANES), lambda b, i: (0, 0, 0, 0)),
        ],
        out_specs=pl.BlockSpec((TQ, hd), qrow),
        out_shape=jax.ShapeDtypeStruct((batch * seq, hd), BF16),
        scratch_shapes=[pltpu.VMEM((nt, TK, TQ), F32)] + _flash_scratch(HA),
        compiler_params=_params("parallel", "arbitrary"),
        name="dsa",
    )(q, k, vt, iq, ik2, iwt, bias)


def _diff_kernel(q_ref, k_ref, vt_ref, lam_ref, g_ref, bias_ref, o_ref,
                 qm_ref, s_ref, p_ref, m_ref, l_ref, c_ref, acc_ref, mb_ref, *, lam_init):
    i = pl.program_id(1)
    t0 = i * TQ
    jd = t0 // TK
    _flash_init(m_ref, l_ref, acc_ref)
    q = q_ref[...]
    nmap = 2 * HB
    for s in range(nmap):
        qm_ref[s] = _half_masked(q[:, (s // 2) * LANES:(s // 2 + 1) * LANES], s % 2 == 1)
    maps = [(s // 2, s // 2, s // 2) for s in range(nmap)]
    qpos = t0 + lax.broadcasted_iota(jnp.int32, (1, TQ), 1)

    def prep(j, par):
        for r0 in range(0, TK, CH):
            kpos = j * TK + r0 + lax.broadcasted_iota(jnp.int32, (CH, TQ), 0)
            mb_ref[par, r0:r0 + CH, :] = jnp.where(kpos <= qpos, 0.0, NEG)

    def scores(j, par, general, only=None):
        row = pl.ds(pl.multiple_of(j * TK, TK), TK)
        _flash_scores(
            par, maps,
            lambda p: k_ref[row, p * LANES:(p + 1) * LANES],
            (lambda r0: mb_ref[par, r0:r0 + CH, :]) if general else None,
            (lambda h, cc, r: _near_bias_t(bias_ref, h, i, j, cc, r)) if general else None,
            qm_ref, s_ref, m_ref, c_ref, only)

    def values(j, par, only):
        _flash_values(par, maps, lambda p: vt_ref[j, p * LANES:(p + 1) * LANES, :],
                      s_ref, p_ref, m_ref, l_ref, c_ref, acc_ref, only)

    _flash_sweep(jd, nmap, lambda j, par: None, lambda j, par, s: scores(j, par, False, s),
                 prep, lambda j, par, s: scores(j, par, True, s), values)

    lam = lam_ref[...]
    lam_full = (jnp.exp(jnp.sum(lam[0:1] * lam[1:2], axis=1, keepdims=True))
                - jnp.exp(jnp.sum(lam[2:3] * lam[3:4], axis=1, keepdims=True)) + lam_init)
    for h in range(HB):
        o0 = acc_ref[2 * h] / l_ref[2 * h]
        o1 = acc_ref[2 * h + 1] / l_ref[2 * h + 1]
        o = (o0 - lam_full * o1).T
        y = o * lax.rsqrt(jnp.mean(o * o, axis=-1, keepdims=True) + LN_EPS) * g_ref[...]
        o_ref[:, h * LANES:(h + 1) * LANES] = (y * (1.0 - lam_init)).astype(BF16)


def _diff(q, k, vt, lam, g, bias, batch, seq, lam_init):
    nq = seq // TQ
    nt = seq // TK
    hb = HB * 2 * DB
    qrow = lambda b, i: (b * nq + i, 0)
    whole = lambda b, i: (b, 0)
    once = pl.Buffered(1)
    return pl.pallas_call(
        functools.partial(_diff_kernel, lam_init=lam_init),
        grid=(batch, nq),
        in_specs=[
            pl.BlockSpec((TQ, hb), qrow),
            pl.BlockSpec((seq, hb), whole, pipeline_mode=once),
            pl.BlockSpec((nt, hb, TK), lambda b, i: (b, 0, 0), pipeline_mode=once),
            pl.BlockSpec((4, DB), lambda b, i: (0, 0)),
            pl.BlockSpec((1, 2 * DB), lambda b, i: (0, 0)),
            pl.BlockSpec((HB, 2, LANES, LANES), lambda b, i: (0, 0, 0, 0)),
        ],
        out_specs=pl.BlockSpec((TQ, hb), qrow),
        out_shape=jax.ShapeDtypeStruct((batch * seq, hb), BF16),
        scratch_shapes=_flash_scratch(2 * HB),
        compiler_params=_params("parallel", "arbitrary"),
        name="diff",
    )(q, k, vt, lam, g, bias)


def _mix_out_kernel(oa_ref, ob_ref, wa_ref, wb_ref, x_ref, g_ref, b_ref, o_ref):
    m = _dot(oa_ref[...], wa_ref[...]) + _dot(ob_ref[...], wb_ref[...])
    o_ref[...] = _layer_norm(ALPHA * x_ref[...] + m, g_ref[...], b_ref[...])


def _mix_out(oa, ob, wa, wb, x, g, b):
    t, d = x.shape
    row = lambda i: (i, 0)
    const = lambda i: (0, 0)
    return pl.pallas_call(
        _mix_out_kernel,
        grid=(t // TM_PROJ,),
        in_specs=[
            pl.BlockSpec((TM_PROJ, oa.shape[1]), row),
            pl.BlockSpec((TM_PROJ, ob.shape[1]), row),
            pl.BlockSpec(wa.shape, const),
            pl.BlockSpec(wb.shape, const),
            pl.BlockSpec((TM_PROJ, d), row),
            pl.BlockSpec((1, d), const),
            pl.BlockSpec((1, d), const),
        ],
        out_specs=pl.BlockSpec((TM_PROJ, d), row),
        out_shape=jax.ShapeDtypeStruct((t, d), F32),
        compiler_params=_params("parallel"),
        name="mix_out",
    )(oa, ob, wa, wb, x, g, b)


def _conv_in_kernel(x_ref, w_ref, o_ref):
    xb = x_ref[...].astype(BF16)
    d = o_ref.shape[1]
    a = _dot(xb, w_ref[:, 0:d])
    gate = _dot(xb, w_ref[:, d:2 * d])
    o_ref[...] = a * jax.nn.sigmoid(gate)


def _conv_in(x, w):
    t, d = x.shape
    return pl.pallas_call(
        _conv_in_kernel,
        grid=(t // TM_CONV,),
        in_specs=[pl.BlockSpec((TM_CONV, d), lambda i: (i, 0)),
                  pl.BlockSpec((d, 2 * d), lambda i: (0, 0))],
        out_specs=pl.BlockSpec((TM_CONV, d), lambda i: (i, 0)),
        out_shape=jax.ShapeDtypeStruct((t, d), F32),
        compiler_params=_params("parallel"),
        name="conv_in",
    )(x, w)


def _conv_out_kernel(h_ref, halo_ref, dw_ref, dwb_ref, cg_ref, cb_ref, w_ref, x_ref, g_ref, b_ref,
                     o_ref, buf_ref, cv_ref):
    i = pl.program_id(1)
    buf_ref[0:HALO, :] = jnp.where(i > 0, halo_ref[...], 0.0)
    buf_ref[HALO:, :] = h_ref[...]
    off = HALO - (CONV_W - 1)

    def chunk(c, carry):
        r0 = pl.multiple_of(c * CONV_ROWS, CONV_ROWS)
        win = buf_ref[pl.ds(r0, CONV_ROWS + HALO), :]
        acc = jnp.broadcast_to(dwb_ref[...], (CONV_ROWS, dwb_ref.shape[1]))
        for sub in range(SUBLANES):
            shifted = win[sub:sub + CONV_ROWS + HALO - SUBLANES] if sub else win
            for k in range(CONV_W):
                if (off + k) % SUBLANES == sub:
                    a0 = off + k - sub
                    acc = acc + dw_ref[k:k + 1, :] * shifted[a0:a0 + CONV_ROWS]
        cv_ref[pl.ds(r0, CONV_ROWS), :] = acc
        return carry

    lax.fori_loop(0, TM_CONV // CONV_ROWS, chunk, 0)
    y = _layer_norm(cv_ref[...], cg_ref[...], cb_ref[...])
    u = (y * jax.nn.sigmoid(y)).astype(BF16)
    m = _dot(u, w_ref[...])
    o_ref[...] = _layer_norm(ALPHA * x_ref[...] + m, g_ref[...], b_ref[...])


def _conv_out(h, dw, dwb, cg, cb, w, x, g, b, batch, seq):
    t, d = x.shape
    nt = seq // TM_CONV
    per = TM_CONV // HALO
    row = lambda bb, i: (bb * nt + i, 0)
    halo = lambda bb, i: (jnp.maximum((bb * nt + i) * per - 1, 0), 0)
    const = lambda bb, i: (0, 0)
    return pl.pallas_call(
        _conv_out_kernel,
        grid=(batch, nt),
        in_specs=[
            pl.BlockSpec((TM_CONV, d), row),
            pl.BlockSpec((HALO, d), halo),
            pl.BlockSpec(dw.shape, const),
            pl.BlockSpec((1, d), const),
            pl.BlockSpec((1, d), const),
            pl.BlockSpec((1, d), const),
            pl.BlockSpec((d, d), const),
            pl.BlockSpec((TM_CONV, d), row),
            pl.BlockSpec((1, d), const),
            pl.BlockSpec((1, d), const),
        ],
        out_specs=pl.BlockSpec((TM_CONV, d), row),
        out_shape=jax.ShapeDtypeStruct((t, d), F32),
        scratch_shapes=[pltpu.VMEM((TM_CONV + HALO, d), F32), pltpu.VMEM((TM_CONV, d), F32)],
        compiler_params=_params("parallel", "arbitrary"),
        name="conv_out",
    )(h, h, dw, dwb, cg, cb, w, x, g, b)


def _attention_mixer(x, w_in, ikg, ikb, lam, subg, w_out, bias, g, b, batch, seq, layer_idx):
    c = [0]
    for n in (HA * DA, HA * DA, HA * DA, HI * DI, DI, HI, HB * 2 * DB, HB * 2 * DB, HB * 2 * DB):
        c.append(c[-1] + n)
    w_ik = w_in[:, c[4]:c[5]]
    w_iw = jnp.pad(w_in[:, c[5]:c[6]], ((0, 0), (0, LANES - HI)))
    w = jnp.concatenate([w_in[:, :c[4]], w_ik, w_ik, w_iw, w_in[:, c[6]:]], axis=1).astype(BF16)
    ikg2 = jnp.concatenate([ikg, ikg])[None]
    ikb2 = jnp.concatenate([ikb, ikb])[None]
    qa, ka, vat, iq, ik2, iwt, qb, kb, vbt = _mix_proj(x, w, ikg2, ikb2)
    o_a = _dsa(qa, ka, vat, iq, ik2, iwt, bias[:HA], batch, seq)
    lam_init = 0.8 - 0.6 * math.exp(-0.3 * layer_idx)
    o_b = _diff(qb, kb, vbt, lam, subg[None], bias[HA:], batch, seq, lam_init)
    wo = w_out.astype(BF16)
    return _mix_out(o_a, o_b, wo[:HA * DA], wo[HA * DA:], x, g, b)


def _conv_module(x, w_in, dw, dwb, cg, cb, w_out, g, b, batch, seq):
    h = _conv_in(x, w_in.astype(BF16))
    dwp = jnp.pad(dw, ((0, HALO - CONV_W), (0, 0)))
    return _conv_out(h, dwp, dwb[None], cg[None], cb[None], w_out.astype(BF16), x, g, b, batch, seq)


def kernel(x, ffn_in, ffn_out, ln_g, ln_b, rel_bias, mix_w_in, idx_k_g, idx_k_b, diff_lambda,
           diff_subln_g, mix_w_out, conv_w_in, conv_dw, conv_dw_b, conv_ln_g, conv_ln_b, conv_w_out):
    batch, seq, d = x.shape
    x = x.reshape(batch * seq, d)
    bias = _bias_tiles(rel_bias.T)
    ffn_in_b = ffn_in.astype(BF16)
    ffn_out_b = ffn_out.astype(BF16)
    for l in range(DEPTH):
        j = l // 2
        x = _ffn(x, ffn_in_b[l, 0], ffn_out_b[l, 0], ln_g[l, 0][None], ln_b[l, 0][None])
        if l % 2 == 0:
            x = _attention_mixer(x, mix_w_in[j], idx_k_g[j], idx_k_b[j], diff_lambda[j], diff_subln_g[j],
                                 mix_w_out[j], bias, ln_g[l, 1][None], ln_b[l, 1][None], batch, seq, l)
        else:
            x = _conv_module(x, conv_w_in[j], conv_dw[j], conv_dw_b[j], conv_ln_g[j], conv_ln_b[j],
                             conv_w_out[j], ln_g[l, 1][None], ln_b[l, 1][None], batch, seq)
        x = _ffn(x, ffn_in_b[l, 1], ffn_out_b[l, 1], ln_g[l, 2][None], ln_b[l, 2][None])
    return x.reshape(batch, seq, d)
```

```python
import functools
import math

import jax
import jax.numpy as jnp
from jax import lax
from jax.experimental import pallas as pl
from jax.experimental.pallas import tpu as pltpu

F32 = jnp.float32
BF16 = jnp.bfloat16

D_MODEL = 1024
DEPTH = 4
HA, DA = 8, 64
HI, DI = 8, 64
TOPK_MAX = 256
HB, DB = 4, 64
CONV_W = 31
D_FF = 2816
NUM_BUCKETS = 32
MAX_DISTANCE = 128
ALPHA = (2 * DEPTH) ** 0.25
LN_EPS = 1e-5

LANES = 128
SUBLANES = 8
BF16_ROWS = 16
VMEM_LIMIT = 56 * 1024 * 1024

TQ = 256
TK = 512
QB = TQ // LANES
KB = TK // LANES
CH = 64
TM_FFN = 512
TF_FFN = 256
TM_PROJ = TK
TM_CONV = 512
HALO = 32
CONV_ROWS = 128
NEG = -1e30
LOG2E = math.log2(math.e)
SEARCH_CAP = 1200


def _params(*sem):
    return pltpu.CompilerParams(dimension_semantics=sem, vmem_limit_bytes=VMEM_LIMIT)


def _layer_norm(v, g, b):
    mu = jnp.mean(v, axis=-1, keepdims=True)
    c = v - mu
    var = jnp.mean(c * c, axis=-1, keepdims=True)
    return c * lax.rsqrt(var + LN_EPS) * g + b


def _dot(a, b):
    return jnp.dot(a, b, preferred_element_type=F32)


def _dot_nt(a, b):
    return lax.dot_general(a, b, (((1,), (1,)), ((), ())), preferred_element_type=F32)


def _ffn_kernel(x_ref, wi_ref, wo_ref, g_ref, b_ref, o_ref, xb_ref, acc_ref):
    f = wo_ref.shape[0]
    xb_ref[...] = x_ref[...].astype(BF16)
    for c in range(f // TF_FFN):
        xb = xb_ref[...]
        a = _dot(xb, wi_ref[:, c * TF_FFN:(c + 1) * TF_FFN])
        u = _dot(xb, wi_ref[:, f + c * TF_FFN:f + (c + 1) * TF_FFN])
        h = (a * jax.nn.sigmoid(a) * u).astype(BF16)
        part = _dot(h, wo_ref[c * TF_FFN:(c + 1) * TF_FFN, :])
        if c == 0:
            acc_ref[...] = part
        else:
            acc_ref[...] += part
    y = ALPHA * x_ref[...] + 0.5 * acc_ref[...]
    o_ref[...] = _layer_norm(y, g_ref[...], b_ref[...])


def _ffn(x, w_in, w_out, g, b):
    t, d = x.shape
    f = w_out.shape[0]
    once = pl.Buffered(1)
    return pl.pallas_call(
        _ffn_kernel,
        grid=(t // TM_FFN,),
        in_specs=[
            pl.BlockSpec((TM_FFN, d), lambda i: (i, 0)),
            pl.BlockSpec((d, 2 * f), lambda i: (0, 0), pipeline_mode=once),
            pl.BlockSpec((f, d), lambda i: (0, 0), pipeline_mode=once),
            pl.BlockSpec((1, d), lambda i: (0, 0)),
            pl.BlockSpec((1, d), lambda i: (0, 0)),
        ],
        out_specs=pl.BlockSpec((TM_FFN, d), lambda i: (i, 0)),
        out_shape=jax.ShapeDtypeStruct((t, d), F32),
        scratch_shapes=[pltpu.VMEM((TM_FFN, d), BF16), pltpu.VMEM((TM_FFN, d), F32)],
        compiler_params=_params("parallel"),
        name="ffn",
    )(x, w_in, w_out, g, b)


_W3 = 3 * HA * DA
_C_IQ = _W3
_C_IK = _C_IQ + HI * DI
_C_IW = _C_IK + LANES
_C_B = _C_IW + LANES
_C_END = _C_B + 3 * HB * 2 * DB


def _mix_proj_kernel(x_ref, w_ref, ikg_ref, ikb_ref,
                     qa_ref, ka_ref, vat_ref, iq_ref, ik_ref, iwt_ref, qb_ref, kb_ref, vbt_ref):
    xb = x_ref[...].astype(BF16)
    hd = HA * DA
    qa_ref[...] = (_dot(xb, w_ref[:, 0:hd]) * (DA ** -0.5 * LOG2E)).astype(BF16)
    ka_ref[...] = _dot(xb, w_ref[:, hd:2 * hd]).astype(BF16)
    vat_ref[0] = _dot(xb, w_ref[:, 2 * hd:3 * hd]).T.astype(BF16)
    iq_ref[...] = (_dot(xb, w_ref[:, _C_IQ:_C_IK]) * (DI ** -0.5)).astype(BF16)
    ik2 = _dot(xb, w_ref[:, _C_IK:_C_IW])
    ik_ref[...] = _layer_norm(ik2, ikg_ref[...], ikb_ref[...]).astype(BF16)
    iw = _dot(xb, w_ref[:, _C_IW:_C_B]) * (HI ** -0.5)
    iwt_ref[...] = iw.T[0:HI, :]
    hb = HB * 2 * DB
    qb_ref[...] = (_dot(xb, w_ref[:, _C_B:_C_B + hb]) * (DB ** -0.5 * LOG2E)).astype(BF16)
    kb_ref[...] = _dot(xb, w_ref[:, _C_B + hb:_C_B + 2 * hb]).astype(BF16)
    vbt_ref[0] = _dot(xb, w_ref[:, _C_B + 2 * hb:_C_END]).T.astype(BF16)


def _mix_proj(x, w, ikg2, ikb2):
    t, d = x.shape
    nt = t // TM_PROJ
    row = lambda i: (i, 0)
    const = lambda i: (0, 0)
    tile = lambda i: (i, 0, 0)
    wide = lambda n, dt: jax.ShapeDtypeStruct((t, n), dt)
    hd, hb = HA * DA, HB * 2 * DB
    return pl.pallas_call(
        _mix_proj_kernel,
        grid=(nt,),
        in_specs=[
            pl.BlockSpec((TM_PROJ, d), row),
            pl.BlockSpec((d, _C_END), const),
            pl.BlockSpec((1, LANES), const),
            pl.BlockSpec((1, LANES), const),
        ],
        out_specs=[
            pl.BlockSpec((TM_PROJ, hd), row), pl.BlockSpec((TM_PROJ, hd), row),
            pl.BlockSpec((1, hd, TM_PROJ), tile), pl.BlockSpec((TM_PROJ, HI * DI), row),
            pl.BlockSpec((TM_PROJ, LANES), row), pl.BlockSpec((HI, TM_PROJ), lambda i: (0, i)),
            pl.BlockSpec((TM_PROJ, hb), row), pl.BlockSpec((TM_PROJ, hb), row),
            pl.BlockSpec((1, hb, TM_PROJ), tile),
        ],
        out_shape=[wide(hd, BF16), wide(hd, BF16), jax.ShapeDtypeStruct((nt, hd, TM_PROJ), BF16),
                   wide(HI * DI, BF16), wide(LANES, BF16), jax.ShapeDtypeStruct((HI, t), F32),
                   wide(hb, BF16), wide(hb, BF16), jax.ShapeDtypeStruct((nt, hb, TM_PROJ), BF16)],
        compiler_params=_params("parallel"),
        name="mix_proj",
    )(x, w, ikg2, ikb2)


def _bias_kernel(tbl_ref, o_ref):
    nh = o_ref.shape[0]
    kk = lax.broadcasted_iota(jnp.int32, (LANES, LANES), 0)
    qq = lax.broadcasted_iota(jnp.int32, (LANES, LANES), 1)
    max_exact = NUM_BUCKETS // 2
    for blk in range(2):
        n = jnp.maximum(qq - kk + blk * LANES, 0)
        nf = jnp.maximum(n, 1).astype(F32)
        large = max_exact + (jnp.log(nf / max_exact) / math.log(MAX_DISTANCE / max_exact)
                             * (NUM_BUCKETS - max_exact)).astype(jnp.int32)
        large = jnp.minimum(large, NUM_BUCKETS - 1)
        bucket = jnp.where(n < max_exact, n, large)
        for h in range(nh):
            acc = jnp.zeros((LANES, LANES), F32)
            for k in range(NUM_BUCKETS):
                acc = jnp.where(bucket == k, tbl_ref[h, k], acc)
            o_ref[h, blk] = (acc - tbl_ref[h, NUM_BUCKETS - 1]) * LOG2E


def _bias_tiles(rel_bias_t):
    nh = rel_bias_t.shape[0]
    return pl.pallas_call(
        _bias_kernel,
        in_specs=[pl.BlockSpec(memory_space=pltpu.SMEM)],
        out_shape=jax.ShapeDtypeStruct((nh, 2, LANES, LANES), F32),
        name="bias_tiles",
    )(rel_bias_t)


def _half_masked(x, upper):
    lane = lax.broadcasted_iota(jnp.int32, x.shape, 1)
    keep = (lane >= LANES // 2) if upper else (lane < LANES // 2)
    return jnp.where(keep, x, jnp.zeros_like(x))


def _near_bias_t(bias_ref, head, i, j, cc, r0):
    b0 = bias_ref[head, 0, r0:r0 + CH, :]
    b1 = bias_ref[head, 1, r0:r0 + CH, :]
    cols = []
    for qq in range(QB):
        delta = (i * QB + qq) - (j * KB + cc)
        cols.append(jnp.where(delta == 0, b0, jnp.where(delta == 1, b1, jnp.zeros_like(b0))))
    return jnp.concatenate(cols, axis=1)


def _fold8(x, op):
    return op(x.reshape(CH // SUBLANES, SUBLANES, x.shape[1]), axis=0)


def _flash_scores(par, maps, k_tile, mask_bias, near_bias, qm_ref, s_ref, m_ref, c_ref, only=None):
    nmap = len(maps)
    cur = par * nmap
    prv = nmap - cur
    nch = LANES // CH
    for s in (range(nmap) if only is None else (only,)):
        pair, _, head = maps[s]
        st = _dot_nt(k_tile(pair), qm_ref[s])
        mx = jnp.full((SUBLANES, TQ), NEG, F32)
        for cc in range(KB):
            for hh in range(nch):
                r0 = cc * LANES + hh * CH
                piece = st[r0:r0 + CH]
                if mask_bias is not None:
                    piece = piece + mask_bias(r0)
                if near_bias is not None:
                    piece = piece + near_bias(head, cc, hh * CH)
                s_ref[cur + s, r0:r0 + CH, :] = piece
                mx = jnp.maximum(mx, _fold8(piece, jnp.max))
        m_prev = m_ref[prv + s]
        m_next = jnp.maximum(m_prev, jnp.max(mx, axis=0, keepdims=True))
        c_ref[cur + s] = jnp.exp2(m_prev - m_next)
        m_ref[cur + s] = m_next


def _flash_values(par, maps, vt_tile, s_ref, p_ref, m_ref, l_ref, c_ref, acc_ref, only=None):
    nmap = len(maps)
    cur = par * nmap
    for s in (range(nmap) if only is None else (only,)):
        m_new = m_ref[cur + s]
        for r0 in range(0, TK, CH):
            p_ref[s, r0:r0 + CH, :] = jnp.exp2(s_ref[cur + s, r0:r0 + CH, :] - m_new).astype(BF16)
        lhs = jnp.concatenate([vt_tile(maps[s][1]), jnp.ones((BF16_ROWS, TK), BF16)], axis=0)
        pv = _dot(lhs, p_ref[s])
        l_ref[s] = c_ref[cur + s] * l_ref[s] + pv[LANES:LANES + 1]
        acc_ref[s] = c_ref[cur + s] * acc_ref[s] + pv[:LANES]


def _flash_sweep(jd, nmap, prep_far, scores_far, prep_any, scores_any, values):
    def step(j_next, par_next, prep, scores):
        prep(j_next, par_next)
        for s in range(nmap):
            scores(j_next, par_next, s)
            values(j_next - 1, 1 - par_next, s)

    prep_any(0, 0)
    for s in range(nmap):
        scores_any(0, 0, s)

    def body(k, c):
        step(2 * k + 1, 1, prep_far, scores_far)
        step(2 * k + 2, 0, prep_far, scores_far)
        return c

    npair = jnp.maximum(jd - 2, 0) // 2
    lax.fori_loop(0, npair, body, 0)
    t = 2 * npair
    left = jd - t
    for r in range(1, 4):
        @pl.when(left >= r)
        def _(r=r):
            step(t + r, r % 2, prep_any, scores_any)

    for par in range(2):
        @pl.when(lax.rem(left, 2) == par)
        def _(par=par):
            values(jd, par, None)


def _flash_init(m_ref, l_ref, acc_ref):
    m_ref[...] = jnp.full(m_ref.shape, NEG, F32)
    l_ref[...] = jnp.zeros_like(l_ref)
    acc_ref[...] = jnp.zeros_like(acc_ref)


def _flash_scratch(nmap):
    return [
        pltpu.VMEM((nmap, TQ, LANES), BF16),
        pltpu.VMEM((2 * nmap, TK, TQ), F32),
        pltpu.VMEM((nmap, TK, TQ), BF16),
        pltpu.VMEM((2 * nmap, 1, TQ), F32),
        pltpu.VMEM((nmap, 1, TQ), F32),
        pltpu.VMEM((2 * nmap, 1, TQ), F32),
        pltpu.VMEM((nmap, LANES, TQ), F32),
        pltpu.VMEM((2, TK, TQ), F32),
    ]


def _dsa_kernel(q_ref, k_ref, vt_ref, iq_ref, ik_ref, iwt_ref, bias_ref, o_ref,
                sc_ref, qm_ref, s_ref, p_ref, m_ref, l_ref, c_ref, acc_ref, mb_ref, *, top_k):
    i = pl.program_id(1)
    t0 = i * TQ
    jd = t0 // TK
    ntile = jd + 1
    qpos = t0 + lax.broadcasted_iota(jnp.int32, (1, TQ), 1)

    iq = iq_ref[...]
    for h in range(HI):
        qm_ref[h] = _half_masked(iq[:, (h // 2) * LANES:(h // 2 + 1) * LANES], h % 2 == 1)
    iwt = iwt_ref[...]

    def score_tile(j, carry, diag):
        rmax, rmin = carry
        ik_t = ik_ref[pl.ds(pl.multiple_of(j * TK, TK), TK), :]
        for hp in range(HI // 2):
            da = _dot_nt(ik_t, qm_ref[2 * hp])
            db = _dot_nt(ik_t, qm_ref[2 * hp + 1])
            wa = iwt[2 * hp:2 * hp + 1, :]
            wb = iwt[2 * hp + 1:2 * hp + 2, :]
            for r0 in range(0, TK, CH):
                piece = wa * jnp.maximum(da[r0:r0 + CH], 0.0) + wb * jnp.maximum(db[r0:r0 + CH], 0.0)
                if hp > 0:
                    piece = piece + sc_ref[j, r0:r0 + CH, :]
                if hp == HI // 2 - 1:
                    if diag:
                        kpos = j * TK + r0 + lax.broadcasted_iota(jnp.int32, (CH, TQ), 0)
                        valid = kpos <= qpos
                        rmax = jnp.maximum(rmax, _fold8(jnp.where(valid, piece, -jnp.inf), jnp.max))
                        rmin = jnp.minimum(rmin, _fold8(jnp.where(valid, piece, jnp.inf), jnp.min))
                        piece = jnp.where(valid, piece, -jnp.inf)
                    else:
                        rmax = jnp.maximum(rmax, _fold8(piece, jnp.max))
                        rmin = jnp.minimum(rmin, _fold8(piece, jnp.min))
                sc_ref[j, r0:r0 + CH, :] = piece
        return rmax, rmin

    carry = lax.fori_loop(
        0, jd, lambda j, c: score_tile(j, c, False),
        (jnp.full((SUBLANES, TQ), -jnp.inf, F32), jnp.full((SUBLANES, TQ), jnp.inf, F32)))
    rmax8, rmin8 = score_tile(jd, carry, True)
    rmax = jnp.max(rmax8, axis=0, keepdims=True)
    rmin = jnp.min(rmin8, axis=0, keepdims=True)

    def row_count(pred):
        def body(j, cnt):
            for r0 in range(0, TK, CH):
                cnt = cnt + _fold8(jnp.where(pred(sc_ref[j, r0:r0 + CH, :]), 1.0, 0.0), jnp.sum)
            return cnt
        cnt = lax.fori_loop(0, ntile, body, jnp.zeros((SUBLANES, TQ), F32))
        return jnp.sum(cnt, axis=0, keepdims=True)

    def count_ge(x):
        return row_count(lambda t: t >= x)

    kf = float(top_k)
    nvalid = (qpos + 1).astype(F32)
    few = nvalid <= kf
    c_ge0 = count_ge(0.0)
    c_gt0 = row_count(lambda t: t > 0.0)
    at0 = jnp.logical_and(c_gt0 < kf, c_ge0 >= kf)
    pos = c_gt0 >= kf
    lo0 = jnp.where(few, -3.0e38, jnp.where(pos, 0.0, jnp.where(at0, 0.0, rmin)))
    hi0 = jnp.where(pos, rmax * 1.000001 + 1e-37, 0.0)
    clo0 = jnp.where(few, kf, jnp.where(jnp.logical_or(pos, at0), c_ge0, nvalid))
    chi0 = jnp.where(pos, 0.0, c_ge0)

    def settled(clo, chi):
        return jnp.logical_or(clo == kf, kf - chi == 1.0)

    done0 = jnp.where(jnp.logical_or(jnp.logical_or(few, at0), settled(clo0, chi0)), 1.0, 0.0)

    def search_cond(st):
        it, _, _, _, _, done = st
        return jnp.logical_and(it < SEARCH_CAP, jnp.min(done) < 0.5)

    def search_body(st):
        it, lo, hi, clo, chi, done = st
        mid = lo + (hi - lo) * 0.5
        stuck = jnp.logical_or(mid <= lo, mid >= hi)
        cx = count_ge(mid)
        ge = cx >= kf
        act = done < 0.5
        up = jnp.logical_and(act, ge)
        dn = jnp.logical_and(act, jnp.logical_not(ge))
        lo = jnp.where(up, mid, lo)
        clo = jnp.where(up, cx, clo)
        hi = jnp.where(dn, mid, hi)
        chi = jnp.where(dn, cx, chi)
        done = jnp.where(jnp.logical_or(stuck, settled(clo, chi)), 1.0, done)
        return it + 1, lo, hi, clo, chi, done

    _, lo, hi, clo, chi, _ = lax.while_loop(
        search_cond, search_body, (jnp.int32(0), lo0, hi0, clo0, chi0, done0))

    def below_hi_max(j, mx):
        for r0 in range(0, TK, CH):
            t = sc_ref[j, r0:r0 + CH, :]
            mx = jnp.maximum(mx, _fold8(jnp.where(t < hi, t, -jnp.inf), jnp.max))
        return mx

    top_below = jnp.max(lax.fori_loop(0, ntile, below_hi_max, jnp.full((SUBLANES, TQ), -jnp.inf, F32)),
                        axis=0, keepdims=True)
    last_one = jnp.logical_and(jnp.logical_and(jnp.logical_not(few), clo != kf), kf - chi == 1.0)
    theta = jnp.where(last_one, top_below, lo)
    c_theta = count_ge(theta)

    tie = jnp.where(jnp.logical_and(jnp.logical_not(few), c_theta > kf), 1.0, 0.0)

    @pl.when(jnp.max(tie) > 0.5)
    def _():
        n_eq = row_count(lambda t: t == theta)
        keep = kf - (c_theta - n_eq)
        ra = lax.broadcasted_iota(jnp.int32, (TK, TK), 0)
        ca = lax.broadcasted_iota(jnp.int32, (TK, TK), 1)
        tril = jnp.where(ca <= ra, 1.0, 0.0).astype(BF16)

        def body(j, before):
            t = sc_ref[j]
            eq = t == theta
            eqf = jnp.where(eq, 1.0, 0.0)
            rank = _dot(tril, eqf.astype(BF16)) + before
            drop = jnp.logical_and(jnp.logical_and(eq, tie > 0.5), rank > keep)
            sc_ref[j] = jnp.where(drop, -jnp.inf, t)
            return before + jnp.sum(eqf, axis=0, keepdims=True)

        lax.fori_loop(0, ntile, body, jnp.zeros((1, TQ), F32))

    _flash_init(m_ref, l_ref, acc_ref)
    q = q_ref[...]
    for h in range(HA):
        qm_ref[h] = _half_masked(q[:, (h // 2) * LANES:(h // 2 + 1) * LANES], h % 2 == 1)
    maps = [(h // 2, h // 2, h) for h in range(HA)]

    def prep(j, par):
        for r0 in range(0, TK, CH):
            mb_ref[par, r0:r0 + CH, :] = jnp.where(sc_ref[j, r0:r0 + CH, :] >= theta, 0.0, NEG)

    def scores(j, par, near, only=None):
        row = pl.ds(pl.multiple_of(j * TK, TK), TK)
        _flash_scores(
            par, maps,
            lambda p: k_ref[row, p * LANES:(p + 1) * LANES],
            lambda r0: mb_ref[par, r0:r0 + CH, :],
            (lambda h, cc, r: _near_bias_t(bias_ref, h, i, j, cc, r)) if near else None,
            qm_ref, s_ref, m_ref, c_ref, only)

    def values(j, par, only):
        _flash_values(par, maps, lambda p: vt_ref[j, p * LANES:(p + 1) * LANES, :],
                      s_ref, p_ref, m_ref, l_ref, c_ref, acc_ref, only)

    _flash_sweep(jd, HA, prep, lambda j, par, s: scores(j, par, False, s),
                 prep, lambda j, par, s: scores(j, par, True, s), values)

    sub = lax.broadcasted_iota(jnp.int32, (LANES, TQ), 0)
    for p in range(HA // 2):
        o_lo = acc_ref[2 * p] / l_ref[2 * p]
        o_hi = acc_ref[2 * p + 1] / l_ref[2 * p + 1]
        o_t = jnp.where(sub < LANES // 2, o_lo, o_hi)
        o_ref[:, p * LANES:(p + 1) * LANES] = o_t.T.astype(BF16)


def _dsa(q, k, vt, iq, ik2, iwt, bias, batch, seq):
    top_k = min(TOPK_MAX, seq // 4)
    nq = seq // TQ
    nt = seq // TK
    hd = HA * DA
    qrow = lambda b, i: (b * nq + i, 0)
    whole = lambda b, i: (b, 0)
    once = pl.Buffered(1)
    return pl.pallas_call(
        functools.partial(_dsa_kernel, top_k=top_k),
        grid=(batch, nq),
        in_specs=[
            pl.BlockSpec((TQ, hd), qrow),
            pl.BlockSpec((seq, hd), whole, pipeline_mode=once),
            pl.BlockSpec((nt, hd, TK), lambda b, i: (b, 0, 0), pipeline_mode=once),
            pl.BlockSpec((TQ, HI * DI), qrow),
            pl.BlockSpec((seq, LANES), whole, pipeline_mode=once),
            pl.BlockSpec((HI, TQ), lambda b, i: (0, b * nq + i)),
            pl.BlockSpec((HA, 2, LANES, LANES), lambda b, i: (0, 0, 0, 0)),
        ],
        out_specs=pl.BlockSpec((TQ, hd), qrow),
        out_shape=jax.ShapeDtypeStruct((batch * seq, hd), BF16),
        scratch_shapes=[pltpu.VMEM((nt, TK, TQ), F32)] + _flash_scratch(HA),
        compiler_params=_params("parallel", "arbitrary"),
        name="dsa",
    )(q, k, vt, iq, ik2, iwt, bias)


def _diff_kernel(q_ref, k_ref, vt_ref, lam_ref, g_ref, bias_ref, o_ref,
                 qm_ref, s_ref, p_ref, m_ref, l_ref, c_ref, acc_ref, mb_ref, *, lam_init):
    i = pl.program_id(1)
    t0 = i * TQ
    jd = t0 // TK
    _flash_init(m_ref, l_ref, acc_ref)
    q = q_ref[...]
    nmap = 2 * HB
    for s in range(nmap):
        qm_ref[s] = _half_masked(q[:, (s // 2) * LANES:(s // 2 + 1) * LANES], s % 2 == 1)
    maps = [(s // 2, s // 2, s // 2) for s in range(nmap)]
    qpos = t0 + lax.broadcasted_iota(jnp.int32, (1, TQ), 1)

    def prep(j, par):
        for r0 in range(0, TK, CH):
            kpos = j * TK + r0 + lax.broadcasted_iota(jnp.int32, (CH, TQ), 0)
            mb_ref[par, r0:r0 + CH, :] = jnp.where(kpos <= qpos, 0.0, NEG)

    def scores(j, par, general, only=None):
        row = pl.ds(pl.multiple_of(j * TK, TK), TK)
        _flash_scores(
            par, maps,
            lambda p: k_ref[row, p * LANES:(p + 1) * LANES],
            (lambda r0: mb_ref[par, r0:r0 + CH, :]) if general else None,
            (lambda h, cc, r: _near_bias_t(bias_ref, h, i, j, cc, r)) if general else None,
            qm_ref, s_ref, m_ref, c_ref, only)

    def values(j, par, only):
        _flash_values(par, maps, lambda p: vt_ref[j, p * LANES:(p + 1) * LANES, :],
                      s_ref, p_ref, m_ref, l_ref, c_ref, acc_ref, only)

    _flash_sweep(jd, nmap, lambda j, par: None, lambda j, par, s: scores(j, par, False, s),
                 prep, lambda j, par, s: scores(j, par, True, s), values)

    lam = lam_ref[...]
    lam_full = (jnp.exp(jnp.sum(lam[0:1] * lam[1:2], axis=1, keepdims=True))
                - jnp.exp(jnp.sum(lam[2:3] * lam[3:4], axis=1, keepdims=True)) + lam_init)
    for h in range(HB):
        o0 = acc_ref[2 * h] / l_ref[2 * h]
        o1 = acc_ref[2 * h + 1] / l_ref[2 * h + 1]
        o = (o0 - lam_full * o1).T
        y = o * lax.rsqrt(jnp.mean(o * o, axis=-1, keepdims=True) + LN_EPS) * g_ref[...]
        o_ref[:, h * LANES:(h + 1) * LANES] = (y * (1.0 - lam_init)).astype(BF16)


def _diff(q, k, vt, lam, g, bias, batch, seq, lam_init):
    nq = seq // TQ
    nt = seq // TK
    hb = HB * 2 * DB
    qrow = lambda b, i: (b * nq + i, 0)
    whole = lambda b, i: (b, 0)
    once = pl.Buffered(1)
    return pl.pallas_call(
        functools.partial(_diff_kernel, lam_init=lam_init),
        grid=(batch, nq),
        in_specs=[
            pl.BlockSpec((TQ, hb), qrow),
            pl.BlockSpec((seq, hb), whole, pipeline_mode=once),
            pl.BlockSpec((nt, hb, TK), lambda b, i: (b, 0, 0), pipeline_mode=once),
            pl.BlockSpec((4, DB), lambda b, i: (0, 0)),
            pl.BlockSpec((1, 2 * DB), lambda b, i: (0, 0)),
            pl.BlockSpec((HB, 2, LANES, LANES), lambda b, i: (0, 0, 0, 0)),
        ],
        out_specs=pl.BlockSpec((TQ, hb), qrow),
        out_shape=jax.ShapeDtypeStruct((batch * seq, hb), BF16),
        scratch_shapes=_flash_scratch(2 * HB),
        compiler_params=_params("parallel", "arbitrary"),
        name="diff",
    )(q, k, vt, lam, g, bias)


def _mix_out_kernel(oa_ref, ob_ref, wa_ref, wb_ref, x_ref, g_ref, b_ref, o_ref):
    m = _dot(oa_ref[...], wa_ref[...]) + _dot(ob_ref[...], wb_ref[...])
    o_ref[...] = _layer_norm(ALPHA * x_ref[...] + m, g_ref[...], b_ref[...])


def _mix_out(oa, ob, wa, wb, x, g, b):
    t, d = x.shape
    row = lambda i: (i, 0)
    const = lambda i: (0, 0)
    return pl.pallas_call(
        _mix_out_kernel,
        grid=(t // TM_PROJ,),
        in_specs=[
            pl.BlockSpec((TM_PROJ, oa.shape[1]), row),
            pl.BlockSpec((TM_PROJ, ob.shape[1]), row),
            pl.BlockSpec(wa.shape, const),
            pl.BlockSpec(wb.shape, const),
            pl.BlockSpec((TM_PROJ, d), row),
            pl.BlockSpec((1, d), const),
            pl.BlockSpec((1, d), const),
        ],
        out_specs=pl.BlockSpec((TM_PROJ, d), row),
        out_shape=jax.ShapeDtypeStruct((t, d), F32),
        compiler_params=_params("parallel"),
        name="mix_out",
    )(oa, ob, wa, wb, x, g, b)


def _conv_in_kernel(x_ref, w_ref, o_ref):
    xb = x_ref[...].astype(BF16)
    d = o_ref.shape[1]
    a = _dot(xb, w_ref[:, 0:d])
    gate = _dot(xb, w_ref[:, d:2 * d])
    o_ref[...] = a * jax.nn.sigmoid(gate)


def _conv_in(x, w):
    t, d = x.shape
    return pl.pallas_call(
        _conv_in_kernel,
        grid=(t // TM_CONV,),
        in_specs=[pl.BlockSpec((TM_CONV, d), lambda i: (i, 0)),
                  pl.BlockSpec((d, 2 * d), lambda i: (0, 0))],
        out_specs=pl.BlockSpec((TM_CONV, d), lambda i: (i, 0)),
        out_shape=jax.ShapeDtypeStruct((t, d), F32),
        compiler_params=_params("parallel"),
        name="conv_in",
    )(x, w)


def _conv_out_kernel(h_ref, halo_ref, dw_ref, dwb_ref, cg_ref, cb_ref, w_ref, x_ref, g_ref, b_ref,
                     o_ref, buf_ref, cv_ref):
    i = pl.program_id(1)
    buf_ref[0:HALO, :] = jnp.where(i > 0, halo_ref[...], 0.0)
    buf_ref[HALO:, :] = h_ref[...]
    off = HALO - (CONV_W - 1)

    def chunk(c, carry):
        r0 = pl.multiple_of(c * CONV_ROWS, CONV_ROWS)
        nwin = CONV_ROWS + HALO
        for lt in range(dwb_ref.shape[1] // LANES):
            cols = slice(lt * LANES, (lt + 1) * LANES)
            win = buf_ref[pl.ds(r0, nwin), cols]
            acc = jnp.broadcast_to(dwb_ref[:, cols], (CONV_ROWS, LANES))
            for sub in range(SUBLANES):
                shifted = pltpu.roll(win, nwin - sub, axis=0) if sub else win
                for k in range(CONV_W):
                    if (off + k) % SUBLANES == sub:
                        a0 = off + k - sub
                        acc = acc + dw_ref[k:k + 1, cols] * shifted[a0:a0 + CONV_ROWS]
            cv_ref[pl.ds(r0, CONV_ROWS), cols] = acc
        return carry

    lax.fori_loop(0, TM_CONV // CONV_ROWS, chunk, 0)
    y = _layer_norm(cv_ref[...], cg_ref[...], cb_ref[...])
    u = (y * jax.nn.sigmoid(y)).astype(BF16)
    m = _dot(u, w_ref[...])
    o_ref[...] = _layer_norm(ALPHA * x_ref[...] + m, g_ref[...], b_ref[...])


def _conv_out(h, dw, dwb, cg, cb, w, x, g, b, batch, seq):
    t, d = x.shape
    nt = seq // TM_CONV
    per = TM_CONV // HALO
    row = lambda bb, i: (bb * nt + i, 0)
    halo = lambda bb, i: (jnp.maximum((bb * nt + i) * per - 1, 0), 0)
    const = lambda bb, i: (0, 0)
    return pl.pallas_call(
        _conv_out_kernel,
        grid=(batch, nt),
        in_specs=[
            pl.BlockSpec((TM_CONV, d), row),
            pl.BlockSpec((HALO, d), halo),
            pl.BlockSpec(dw.shape, const),
            pl.BlockSpec((1, d), const),
            pl.BlockSpec((1, d), const),
            pl.BlockSpec((1, d), const),
            pl.BlockSpec((d, d), const),
            pl.BlockSpec((TM_CONV, d), row),
            pl.BlockSpec((1, d), const),
            pl.BlockSpec((1, d), const),
        ],
        out_specs=pl.BlockSpec((TM_CONV, d), row),
        out_shape=jax.ShapeDtypeStruct((t, d), F32),
        scratch_shapes=[pltpu.VMEM((TM_CONV + HALO, d), F32), pltpu.VMEM((TM_CONV, d), F32)],
        compiler_params=_params("parallel", "arbitrary"),
        name="conv_out",
    )(h, h, dw, dwb, cg, cb, w, x, g, b)


def _attention_mixer(x, w_in, ikg, ikb, lam, subg, w_out, bias, g, b, batch, seq, layer_idx):
    c = [0]
    for n in (HA * DA, HA * DA, HA * DA, HI * DI, DI, HI, HB * 2 * DB, HB * 2 * DB, HB * 2 * DB):
        c.append(c[-1] + n)
    w_ik = w_in[:, c[4]:c[5]]
    w_iw = jnp.pad(w_in[:, c[5]:c[6]], ((0, 0), (0, LANES - HI)))
    w = jnp.concatenate([w_in[:, :c[4]], w_ik, w_ik, w_iw, w_in[:, c[6]:]], axis=1).astype(BF16)
    ikg2 = jnp.concatenate([ikg, ikg])[None]
    ikb2 = jnp.concatenate([ikb, ikb])[None]
    qa, ka, vat, iq, ik2, iwt, qb, kb, vbt = _mix_proj(x, w, ikg2, ikb2)
    o_a = _dsa(qa, ka, vat, iq, ik2, iwt, bias[:HA], batch, seq)
    lam_init = 0.8 - 0.6 * math.exp(-0.3 * layer_idx)
    o_b = _diff(qb, kb, vbt, lam, subg[None], bias[HA:], batch, seq, lam_init)
    wo = w_out.astype(BF16)
    return _mix_out(o_a, o_b, wo[:HA * DA], wo[HA * DA:], x, g, b)


def _conv_module(x, w_in, dw, dwb, cg, cb, w_out, g, b, batch, seq):
    h = _conv_in(x, w_in.astype(BF16))
    dwp = jnp.pad(dw, ((0, HALO - CONV_W), (0, 0)))
    return _conv_out(h, dwp, dwb[None], cg[None], cb[None], w_out.astype(BF16), x, g, b, batch, seq)


def kernel(x, ffn_in, ffn_out, ln_g, ln_b, rel_bias, mix_w_in, idx_k_g, idx_k_b, diff_lambda,
           diff_subln_g, mix_w_out, conv_w_in, conv_dw, conv_dw_b, conv_ln_g, conv_ln_b, conv_w_out):
    batch, seq, d = x.shape
    x = x.reshape(batch * seq, d)
    bias = _bias_tiles(rel_bias.T)
    ffn_in_b = ffn_in.astype(BF16)
    ffn_out_b = ffn_out.astype(BF16)
    for l in range(DEPTH):
        j = l // 2
        x = _ffn(x, ffn_in_b[l, 0], ffn_out_b[l, 0], ln_g[l, 0][None], ln_b[l, 0][None])
        if l % 2 == 0:
            x = _attention_mixer(x, mix_w_in[j], idx_k_g[j], idx_k_b[j], diff_lambda[j], diff_subln_g[j],
                                 mix_w_out[j], bias, ln_g[l, 1][None], ln_b[l, 1][None], batch, seq, l)
        else:
            x = _conv_module(x, conv_w_in[j], conv_dw[j], conv_dw_b[j], conv_ln_g[j], conv_ln_b[j],
                             conv_w_out[j], ln_g[l, 1][None], ln_b[l, 1][None], batch, seq)
        x = _ffn(x, ffn_in_b[l, 1], ffn_out_b[l, 1], ln_g[l, 2][None], ln_b[l, 2][None])
    return x.reshape(batch, seq, d)
```

```python
import functools
import math

import jax
import jax.numpy as jnp
from jax import lax
from jax.experimental import pallas as pl
from jax.experimental.pallas import tpu as pltpu

F32 = jnp.float32
BF16 = jnp.bfloat16

D_MODEL = 1024
DEPTH = 4
HA, DA = 8, 64
HI, DI = 8, 64
TOPK_MAX = 256
HB, DB = 4, 64
CONV_W = 31
D_FF = 2816
NUM_BUCKETS = 32
MAX_DISTANCE = 128
ALPHA = (2 * DEPTH) ** 0.25
LN_EPS = 1e-5

LANES = 128
SUBLANES = 8
BF16_ROWS = 16
VMEM_LIMIT = 56 * 1024 * 1024

TQ = 256
TK = 512
QB = TQ // LANES
KB = TK // LANES
CH = 64
TM_FFN = 512
TF_FFN = 256
TM_PROJ = TK
TM_CONV = 512
HALO = 32
CONV_ROWS = 128
NEG = -1e30
LOG2E = math.log2(math.e)
SEARCH_CAP = 1200


def _params(*sem):
    return pltpu.CompilerParams(dimension_semantics=sem, vmem_limit_bytes=VMEM_LIMIT)


def _layer_norm(v, g, b):
    mu = jnp.mean(v, axis=-1, keepdims=True)
    c = v - mu
    var = jnp.mean(c * c, axis=-1, keepdims=True)
    return c * lax.rsqrt(var + LN_EPS) * g + b


def _dot(a, b):
    return jnp.dot(a, b, preferred_element_type=F32)


def _dot_nt(a, b):
    return lax.dot_general(a, b, (((1,), (1,)), ((), ())), preferred_element_type=F32)


def _ffn_kernel(x_ref, wi_ref, wo_ref, g_ref, b_ref, o_ref, xb_ref, acc_ref):
    f = wo_ref.shape[0]
    xb_ref[...] = x_ref[...].astype(BF16)
    for c in range(f // TF_FFN):
        xb = xb_ref[...]
        a = _dot(xb, wi_ref[:, c * TF_FFN:(c + 1) * TF_FFN])
        u = _dot(xb, wi_ref[:, f + c * TF_FFN:f + (c + 1) * TF_FFN])
        h = (a * jax.nn.sigmoid(a) * u).astype(BF16)
        part = _dot(h, wo_ref[c * TF_FFN:(c + 1) * TF_FFN, :])
        if c == 0:
            acc_ref[...] = part
        else:
            acc_ref[...] += part
    y = ALPHA * x_ref[...] + 0.5 * acc_ref[...]
    o_ref[...] = _layer_norm(y, g_ref[...], b_ref[...])


def _ffn(x, w_in, w_out, g, b):
    t, d = x.shape
    f = w_out.shape[0]
    once = pl.Buffered(1)
    return pl.pallas_call(
        _ffn_kernel,
        grid=(t // TM_FFN,),
        in_specs=[
            pl.BlockSpec((TM_FFN, d), lambda i: (i, 0)),
            pl.BlockSpec((d, 2 * f), lambda i: (0, 0), pipeline_mode=once),
            pl.BlockSpec((f, d), lambda i: (0, 0), pipeline_mode=once),
            pl.BlockSpec((1, d), lambda i: (0, 0)),
            pl.BlockSpec((1, d), lambda i: (0, 0)),
        ],
        out_specs=pl.BlockSpec((TM_FFN, d), lambda i: (i, 0)),
        out_shape=jax.ShapeDtypeStruct((t, d), F32),
        scratch_shapes=[pltpu.VMEM((TM_FFN, d), BF16), pltpu.VMEM((TM_FFN, d), F32)],
        compiler_params=_params("parallel"),
        name="ffn",
    )(x, w_in, w_out, g, b)


_W3 = 3 * HA * DA
_C_IQ = _W3
_C_IK = _C_IQ + HI * DI
_C_IW = _C_IK + LANES
_C_B = _C_IW + LANES
_C_END = _C_B + 3 * HB * 2 * DB


def _mix_proj_kernel(x_ref, w_ref, ikg_ref, ikb_ref,
                     qa_ref, ka_ref, vat_ref, iq_ref, ik_ref, iwt_ref, qb_ref, kb_ref, vbt_ref):
    xb = x_ref[...].astype(BF16)
    hd = HA * DA
    qa_ref[...] = (_dot(xb, w_ref[:, 0:hd]) * (DA ** -0.5 * LOG2E)).astype(BF16)
    ka_ref[...] = _dot(xb, w_ref[:, hd:2 * hd]).astype(BF16)
    vat_ref[0] = _dot(xb, w_ref[:, 2 * hd:3 * hd]).T.astype(BF16)
    iq_ref[...] = (_dot(xb, w_ref[:, _C_IQ:_C_IK]) * (DI ** -0.5)).astype(BF16)
    ik2 = _dot(xb, w_ref[:, _C_IK:_C_IW])
    ik_ref[...] = _layer_norm(ik2, ikg_ref[...], ikb_ref[...]).astype(BF16)
    iw = _dot(xb, w_ref[:, _C_IW:_C_B]) * (HI ** -0.5)
    iwt_ref[...] = iw.T[0:HI, :]
    hb = HB * 2 * DB
    qb_ref[...] = (_dot(xb, w_ref[:, _C_B:_C_B + hb]) * (DB ** -0.5 * LOG2E)).astype(BF16)
    kb_ref[...] = _dot(xb, w_ref[:, _C_B + hb:_C_B + 2 * hb]).astype(BF16)
    vbt_ref[0] = _dot(xb, w_ref[:, _C_B + 2 * hb:_C_END]).T.astype(BF16)


def _mix_proj(x, w, ikg2, ikb2):
    t, d = x.shape
    nt = t // TM_PROJ
    row = lambda i: (i, 0)
    const = lambda i: (0, 0)
    tile = lambda i: (i, 0, 0)
    wide = lambda n, dt: jax.ShapeDtypeStruct((t, n), dt)
    hd, hb = HA * DA, HB * 2 * DB
    return pl.pallas_call(
        _mix_proj_kernel,
        grid=(nt,),
        in_specs=[
            pl.BlockSpec((TM_PROJ, d), row),
            pl.BlockSpec((d, _C_END), const),
            pl.BlockSpec((1, LANES), const),
            pl.BlockSpec((1, LANES), const),
        ],
        out_specs=[
            pl.BlockSpec((TM_PROJ, hd), row), pl.BlockSpec((TM_PROJ, hd), row),
            pl.BlockSpec((1, hd, TM_PROJ), tile), pl.BlockSpec((TM_PROJ, HI * DI), row),
            pl.BlockSpec((TM_PROJ, LANES), row), pl.BlockSpec((HI, TM_PROJ), lambda i: (0, i)),
            pl.BlockSpec((TM_PROJ, hb), row), pl.BlockSpec((TM_PROJ, hb), row),
            pl.BlockSpec((1, hb, TM_PROJ), tile),
        ],
        out_shape=[wide(hd, BF16), wide(hd, BF16), jax.ShapeDtypeStruct((nt, hd, TM_PROJ), BF16),
                   wide(HI * DI, BF16), wide(LANES, BF16), jax.ShapeDtypeStruct((HI, t), F32),
                   wide(hb, BF16), wide(hb, BF16), jax.ShapeDtypeStruct((nt, hb, TM_PROJ), BF16)],
        compiler_params=_params("parallel"),
        name="mix_proj",
    )(x, w, ikg2, ikb2)


def _bias_kernel(tbl_ref, o_ref):
    nh = o_ref.shape[0]
    kk = lax.broadcasted_iota(jnp.int32, (LANES, LANES), 0)
    qq = lax.broadcasted_iota(jnp.int32, (LANES, LANES), 1)
    max_exact = NUM_BUCKETS // 2
    for blk in range(2):
        n = jnp.maximum(qq - kk + blk * LANES, 0)
        nf = jnp.maximum(n, 1).astype(F32)
        large = max_exact + (jnp.log(nf / max_exact) / math.log(MAX_DISTANCE / max_exact)
                             * (NUM_BUCKETS - max_exact)).astype(jnp.int32)
        large = jnp.minimum(large, NUM_BUCKETS - 1)
        bucket = jnp.where(n < max_exact, n, large)
        for h in range(nh):
            acc = jnp.zeros((LANES, LANES), F32)
            for k in range(NUM_BUCKETS):
                acc = jnp.where(bucket == k, tbl_ref[h, k], acc)
            o_ref[h, blk] = (acc - tbl_ref[h, NUM_BUCKETS - 1]) * LOG2E


def _bias_tiles(rel_bias_t):
    nh = rel_bias_t.shape[0]
    return pl.pallas_call(
        _bias_kernel,
        in_specs=[pl.BlockSpec(memory_space=pltpu.SMEM)],
        out_shape=jax.ShapeDtypeStruct((nh, 2, LANES, LANES), F32),
        name="bias_tiles",
    )(rel_bias_t)


def _half_masked(x, upper):
    lane = lax.broadcasted_iota(jnp.int32, x.shape, 1)
    keep = (lane >= LANES // 2) if upper else (lane < LANES // 2)
    return jnp.where(keep, x, jnp.zeros_like(x))


def _near_bias_t(bias_ref, head, i, j, cc, r0):
    b0 = bias_ref[head, 0, r0:r0 + CH, :]
    b1 = bias_ref[head, 1, r0:r0 + CH, :]
    cols = []
    for qq in range(QB):
        delta = (i * QB + qq) - (j * KB + cc)
        cols.append(jnp.where(delta == 0, b0, jnp.where(delta == 1, b1, jnp.zeros_like(b0))))
    return jnp.concatenate(cols, axis=1)


def _fold8(x, op):
    return op(x.reshape(CH // SUBLANES, SUBLANES, x.shape[1]), axis=0)


def _flash_scores(par, maps, k_tile, mask_bias, near_bias, qm_ref, s_ref, m_ref, c_ref, only=None, st=None):
    nmap = len(maps)
    cur = par * nmap
    prv = nmap - cur
    nch = LANES // CH
    for s in (range(nmap) if only is None else (only,)):
        pair, _, head = maps[s]
        if st is None or only is None:
            st = _dot_nt(k_tile(pair), qm_ref[s])
        mx = jnp.full((SUBLANES, TQ), NEG, F32)
        for cc in range(KB):
            for hh in range(nch):
                r0 = cc * LANES + hh * CH
                piece = st[r0:r0 + CH]
                if mask_bias is not None:
                    piece = piece + mask_bias(r0)
                if near_bias is not None:
                    piece = piece + near_bias(head, cc, hh * CH)
                s_ref[cur + s, r0:r0 + CH, :] = piece
                mx = jnp.maximum(mx, _fold8(piece, jnp.max))
        m_prev = m_ref[prv + s]
        m_next = jnp.maximum(m_prev, jnp.max(mx, axis=0, keepdims=True))
        c_ref[cur + s] = jnp.exp2(m_prev - m_next)
        m_ref[cur + s] = m_next


def _flash_values(par, maps, vt_tile, s_ref, p_ref, m_ref, l_ref, c_ref, acc_ref, only=None, part=None):
    nmap = len(maps)
    cur = par * nmap
    for s in (range(nmap) if only is None else (only,)):
        if part != "pv":
            m_new = m_ref[cur + s]
            for r0 in range(0, TK, CH):
                p_ref[s, r0:r0 + CH, :] = jnp.exp2(s_ref[cur + s, r0:r0 + CH, :] - m_new).astype(BF16)
        if part == "probs":
            continue
        lhs = jnp.concatenate([vt_tile(maps[s][1]), jnp.ones((BF16_ROWS, TK), BF16)], axis=0)
        pv = _dot(lhs, p_ref[s])
        dv = acc_ref.shape[1]
        l_ref[s] = c_ref[cur + s] * l_ref[s] + pv[dv:dv + 1]
        acc_ref[s] = c_ref[cur + s] * acc_ref[s] + pv[:dv]


def _flash_sweep(jd, nmap, qk, prep_far, scores_far, prep_any, scores_any, values):
    def step(j_next, par_next, prep, scores):
        prep(j_next, par_next)
        st = qk(j_next, 0)
        for s in range(nmap):
            values(j_next - 1, 1 - par_next, s, "probs")
            st_after = qk(j_next, s + 1) if s + 1 < nmap else None
            scores(j_next, par_next, s, st)
            values(j_next - 1, 1 - par_next, s, "pv")
            st = st_after

    prep_any(0, 0)
    for s in range(nmap):
        scores_any(0, 0, s, None)

    def body(k, c):
        step(2 * k + 1, 1, prep_far, scores_far)
        step(2 * k + 2, 0, prep_far, scores_far)
        return c

    npair = jnp.maximum(jd - 2, 0) // 2
    lax.fori_loop(0, npair, body, 0)
    t = 2 * npair
    left = jd - t
    for r in range(1, 4):
        @pl.when(left >= r)
        def _(r=r):
            step(t + r, r % 2, prep_any, scores_any)

    for par in range(2):
        @pl.when(lax.rem(left, 2) == par)
        def _(par=par):
            values(jd, par, None, None)


def _flash_init(m_ref, l_ref, acc_ref):
    m_ref[...] = jnp.full(m_ref.shape, NEG, F32)
    l_ref[...] = jnp.zeros_like(l_ref)
    acc_ref[...] = jnp.zeros_like(acc_ref)


def _flash_scratch(nmap, dv):
    return [
        pltpu.VMEM((nmap, TQ, LANES), BF16),
        pltpu.VMEM((2 * nmap, TK, TQ), F32),
        pltpu.VMEM((nmap, TK, TQ), BF16),
        pltpu.VMEM((2 * nmap, 1, TQ), F32),
        pltpu.VMEM((nmap, 1, TQ), F32),
        pltpu.VMEM((2 * nmap, 1, TQ), F32),
        pltpu.VMEM((nmap, dv, TQ), F32),
        pltpu.VMEM((2, TK, TQ), F32),
    ]


def _dsa_kernel(q_ref, k_ref, vt_ref, iq_ref, ik_ref, iwt_ref, bias_ref, o_ref,
                sc_ref, qm_ref, s_ref, p_ref, m_ref, l_ref, c_ref, acc_ref, mb_ref, *, top_k):
    i = pl.program_id(1)
    t0 = i * TQ
    jd = t0 // TK
    ntile = jd + 1
    qpos = t0 + lax.broadcasted_iota(jnp.int32, (1, TQ), 1)

    iq = iq_ref[...]
    for h in range(HI):
        qm_ref[h] = _half_masked(iq[:, (h // 2) * LANES:(h // 2 + 1) * LANES], h % 2 == 1)
    iwt = iwt_ref[...]

    def score_tile(j, carry, diag):
        rmax, rmin = carry
        ik_t = ik_ref[pl.ds(pl.multiple_of(j * TK, TK), TK), :]
        for hp in range(HI // 2):
            da = _dot_nt(ik_t, qm_ref[2 * hp])
            db = _dot_nt(ik_t, qm_ref[2 * hp + 1])
            wa = iwt[2 * hp:2 * hp + 1, :]
            wb = iwt[2 * hp + 1:2 * hp + 2, :]
            for r0 in range(0, TK, CH):
                piece = wa * jnp.maximum(da[r0:r0 + CH], 0.0) + wb * jnp.maximum(db[r0:r0 + CH], 0.0)
                if hp > 0:
                    piece = piece + sc_ref[j, r0:r0 + CH, :]
                if hp == HI // 2 - 1:
                    if diag:
                        kpos = j * TK + r0 + lax.broadcasted_iota(jnp.int32, (CH, TQ), 0)
                        valid = kpos <= qpos
                        rmax = jnp.maximum(rmax, _fold8(jnp.where(valid, piece, -jnp.inf), jnp.max))
                        rmin = jnp.minimum(rmin, _fold8(jnp.where(valid, piece, jnp.inf), jnp.min))
                        piece = jnp.where(valid, piece, -jnp.inf)
                    else:
                        rmax = jnp.maximum(rmax, _fold8(piece, jnp.max))
                        rmin = jnp.minimum(rmin, _fold8(piece, jnp.min))
                sc_ref[j, r0:r0 + CH, :] = piece
        return rmax, rmin

    carry = lax.fori_loop(
        0, jd, lambda j, c: score_tile(j, c, False),
        (jnp.full((SUBLANES, TQ), -jnp.inf, F32), jnp.full((SUBLANES, TQ), jnp.inf, F32)))
    rmax8, rmin8 = score_tile(jd, carry, True)
    rmax = jnp.max(rmax8, axis=0, keepdims=True)
    rmin = jnp.min(rmin8, axis=0, keepdims=True)

    def row_count(pred):
        def body(j, cnt):
            for r0 in range(0, TK, CH):
                cnt = cnt + _fold8(jnp.where(pred(sc_ref[j, r0:r0 + CH, :]), 1.0, 0.0), jnp.sum)
            return cnt
        cnt = lax.fori_loop(0, ntile, body, jnp.zeros((SUBLANES, TQ), F32))
        return jnp.sum(cnt, axis=0, keepdims=True)

    def count_ge(x):
        return row_count(lambda t: t >= x)

    kf = float(top_k)
    nvalid = (qpos + 1).astype(F32)
    few = nvalid <= kf
    c_ge0 = count_ge(0.0)
    c_gt0 = row_count(lambda t: t > 0.0)
    at0 = jnp.logical_and(c_gt0 < kf, c_ge0 >= kf)
    pos = c_gt0 >= kf
    lo0 = jnp.where(few, -3.0e38, jnp.where(pos, 0.0, jnp.where(at0, 0.0, rmin)))
    hi0 = jnp.where(pos, rmax * 1.000001 + 1e-37, 0.0)
    clo0 = jnp.where(few, kf, jnp.where(jnp.logical_or(pos, at0), c_ge0, nvalid))
    chi0 = jnp.where(pos, 0.0, c_ge0)

    def settled(clo, chi):
        return jnp.logical_or(clo == kf, kf - chi == 1.0)

    done0 = jnp.where(jnp.logical_or(jnp.logical_or(few, at0), settled(clo0, chi0)), 1.0, 0.0)

    def search_cond(st):
        it, _, _, _, _, done = st
        return jnp.logical_and(it < SEARCH_CAP, jnp.min(done) < 0.5)

    def search_body(st):
        it, lo, hi, clo, chi, done = st
        mid = lo + (hi - lo) * 0.5
        stuck = jnp.logical_or(mid <= lo, mid >= hi)
        cx = count_ge(mid)
        ge = cx >= kf
        act = done < 0.5
        up = jnp.logical_and(act, ge)
        dn = jnp.logical_and(act, jnp.logical_not(ge))
        lo = jnp.where(up, mid, lo)
        clo = jnp.where(up, cx, clo)
        hi = jnp.where(dn, mid, hi)
        chi = jnp.where(dn, cx, chi)
        done = jnp.where(jnp.logical_or(stuck, settled(clo, chi)), 1.0, done)
        return it + 1, lo, hi, clo, chi, done

    _, lo, hi, clo, chi, _ = lax.while_loop(
        search_cond, search_body, (jnp.int32(0), lo0, hi0, clo0, chi0, done0))

    def below_hi_max(j, mx):
        for r0 in range(0, TK, CH):
            t = sc_ref[j, r0:r0 + CH, :]
            mx = jnp.maximum(mx, _fold8(jnp.where(t < hi, t, -jnp.inf), jnp.max))
        return mx

    top_below = jnp.max(lax.fori_loop(0, ntile, below_hi_max, jnp.full((SUBLANES, TQ), -jnp.inf, F32)),
                        axis=0, keepdims=True)
    last_one = jnp.logical_and(jnp.logical_and(jnp.logical_not(few), clo != kf), kf - chi == 1.0)
    theta = jnp.where(last_one, top_below, lo)
    c_theta = count_ge(theta)

    tie = jnp.where(jnp.logical_and(jnp.logical_not(few), c_theta > kf), 1.0, 0.0)

    @pl.when(jnp.max(tie) > 0.5)
    def _():
        n_eq = row_count(lambda t: t == theta)
        keep = kf - (c_theta - n_eq)
        ra = lax.broadcasted_iota(jnp.int32, (TK, TK), 0)
        ca = lax.broadcasted_iota(jnp.int32, (TK, TK), 1)
        tril = jnp.where(ca <= ra, 1.0, 0.0).astype(BF16)

        def body(j, before):
            t = sc_ref[j]
            eq = t == theta
            eqf = jnp.where(eq, 1.0, 0.0)
            rank = _dot(tril, eqf.astype(BF16)) + before
            drop = jnp.logical_and(jnp.logical_and(eq, tie > 0.5), rank > keep)
            sc_ref[j] = jnp.where(drop, -jnp.inf, t)
            return before + jnp.sum(eqf, axis=0, keepdims=True)

        lax.fori_loop(0, ntile, body, jnp.zeros((1, TQ), F32))

    _flash_init(m_ref, l_ref, acc_ref)
    q = q_ref[...]
    for h in range(HA):
        qm_ref[h] = _half_masked(q[:, (h // 2) * LANES:(h // 2 + 1) * LANES], h % 2 == 1)
    maps = [(h // 2, h, h) for h in range(HA)]

    def prep(j, par):
        for r0 in range(0, TK, CH):
            mb_ref[par, r0:r0 + CH, :] = jnp.where(sc_ref[j, r0:r0 + CH, :] >= theta, 0.0, NEG)

    def k_tile(j):
        row = pl.ds(pl.multiple_of(j * TK, TK), TK)
        return lambda p: k_ref[row, p * LANES:(p + 1) * LANES]

    def qk(j, s):
        return _dot_nt(k_tile(j)(maps[s][0]), qm_ref[s])

    def scores(j, par, near, only, st):
        _flash_scores(
            par, maps, k_tile(j),
            lambda r0: mb_ref[par, r0:r0 + CH, :],
            (lambda h, cc, r: _near_bias_t(bias_ref, h, i, j, cc, r)) if near else None,
            qm_ref, s_ref, m_ref, c_ref, only, st)

    def values(j, par, only, part):
        _flash_values(par, maps, lambda h: vt_ref[j, h * DA:(h + 1) * DA, :],
                      s_ref, p_ref, m_ref, l_ref, c_ref, acc_ref, only, part)

    _flash_sweep(jd, HA, qk, prep, lambda j, par, s, st: scores(j, par, False, s, st),
                 prep, lambda j, par, s, st: scores(j, par, True, s, st), values)

    for p in range(HA // 2):
        o_lo = acc_ref[2 * p] / l_ref[2 * p]
        o_hi = acc_ref[2 * p + 1] / l_ref[2 * p + 1]
        o_t = jnp.concatenate([o_lo, o_hi], axis=0)
        o_ref[:, p * LANES:(p + 1) * LANES] = o_t.T.astype(BF16)


def _dsa(q, k, vt, iq, ik2, iwt, bias, batch, seq):
    top_k = min(TOPK_MAX, seq // 4)
    nq = seq // TQ
    nt = seq // TK
    hd = HA * DA
    qrow = lambda b, i: (b * nq + i, 0)
    whole = lambda b, i: (b, 0)
    once = pl.Buffered(1)
    return pl.pallas_call(
        functools.partial(_dsa_kernel, top_k=top_k),
        grid=(batch, nq),
        in_specs=[
            pl.BlockSpec((TQ, hd), qrow),
            pl.BlockSpec((seq, hd), whole, pipeline_mode=once),
            pl.BlockSpec((nt, hd, TK), lambda b, i: (b, 0, 0), pipeline_mode=once),
            pl.BlockSpec((TQ, HI * DI), qrow),
            pl.BlockSpec((seq, LANES), whole, pipeline_mode=once),
            pl.BlockSpec((HI, TQ), lambda b, i: (0, b * nq + i)),
            pl.BlockSpec((HA, 2, LANES, LANES), lambda b, i: (0, 0, 0, 0)),
        ],
        out_specs=pl.BlockSpec((TQ, hd), qrow),
        out_shape=jax.ShapeDtypeStruct((batch * seq, hd), BF16),
        scratch_shapes=[pltpu.VMEM((nt, TK, TQ), F32)] + _flash_scratch(HA, DA),
        compiler_params=_params("parallel", "arbitrary"),
        name="dsa",
    )(q, k, vt, iq, ik2, iwt, bias)


def _diff_kernel(q_ref, k_ref, vt_ref, lam_ref, g_ref, bias_ref, o_ref,
                 qm_ref, s_ref, p_ref, m_ref, l_ref, c_ref, acc_ref, mb_ref, *, lam_init):
    i = pl.program_id(1)
    t0 = i * TQ
    jd = t0 // TK
    _flash_init(m_ref, l_ref, acc_ref)
    q = q_ref[...]
    nmap = 2 * HB
    for s in range(nmap):
        qm_ref[s] = _half_masked(q[:, (s // 2) * LANES:(s // 2 + 1) * LANES], s % 2 == 1)
    maps = [(s // 2, s // 2, s // 2) for s in range(nmap)]
    qpos = t0 + lax.broadcasted_iota(jnp.int32, (1, TQ), 1)

    def prep(j, par):
        for r0 in range(0, TK, CH):
            kpos = j * TK + r0 + lax.broadcasted_iota(jnp.int32, (CH, TQ), 0)
            mb_ref[par, r0:r0 + CH, :] = jnp.where(kpos <= qpos, 0.0, NEG)

    def k_tile(j):
        row = pl.ds(pl.multiple_of(j * TK, TK), TK)
        return lambda p: k_ref[row, p * LANES:(p + 1) * LANES]

    def qk(j, s):
        return _dot_nt(k_tile(j)(maps[s][0]), qm_ref[s])

    def scores(j, par, general, only, st):
        _flash_scores(
            par, maps, k_tile(j),
            (lambda r0: mb_ref[par, r0:r0 + CH, :]) if general else None,
            (lambda h, cc, r: _near_bias_t(bias_ref, h, i, j, cc, r)) if general else None,
            qm_ref, s_ref, m_ref, c_ref, only, st)

    def values(j, par, only, part):
        _flash_values(par, maps, lambda p: vt_ref[j, p * LANES:(p + 1) * LANES, :],
                      s_ref, p_ref, m_ref, l_ref, c_ref, acc_ref, only, part)

    _flash_sweep(jd, nmap, qk, lambda j, par: None, lambda j, par, s, st: scores(j, par, False, s, st),
                 prep, lambda j, par, s, st: scores(j, par, True, s, st), values)

    lam = lam_ref[...]
    lam_full = (jnp.exp(jnp.sum(lam[0:1] * lam[1:2], axis=1, keepdims=True))
                - jnp.exp(jnp.sum(lam[2:3] * lam[3:4], axis=1, keepdims=True)) + lam_init)
    for h in range(HB):
        o0 = acc_ref[2 * h] / l_ref[2 * h]
        o1 = acc_ref[2 * h + 1] / l_ref[2 * h + 1]
        o = (o0 - lam_full * o1).T
        y = o * lax.rsqrt(jnp.mean(o * o, axis=-1, keepdims=True) + LN_EPS) * g_ref[...]
        o_ref[:, h * LANES:(h + 1) * LANES] = (y * (1.0 - lam_init)).astype(BF16)


def _diff(q, k, vt, lam, g, bias, batch, seq, lam_init):
    nq = seq // TQ
    nt = seq // TK
    hb = HB * 2 * DB
    qrow = lambda b, i: (b * nq + i, 0)
    whole = lambda b, i: (b, 0)
    once = pl.Buffered(1)
    return pl.pallas_call(
        functools.partial(_diff_kernel, lam_init=lam_init),
        grid=(batch, nq),
        in_specs=[
            pl.BlockSpec((TQ, hb), qrow),
            pl.BlockSpec((seq, hb), whole, pipeline_mode=once),
            pl.BlockSpec((nt, hb, TK), lambda b, i: (b, 0, 0), pipeline_mode=once),
            pl.BlockSpec((4, DB), lambda b, i: (0, 0)),
            pl.BlockSpec((1, 2 * DB), lambda b, i: (0, 0)),
            pl.BlockSpec((HB, 2, LANES, LANES), lambda b, i: (0, 0, 0, 0)),
        ],
        out_specs=pl.BlockSpec((TQ, hb), qrow),
        out_shape=jax.ShapeDtypeStruct((batch * seq, hb), BF16),
        scratch_shapes=_flash_scratch(2 * HB, 2 * DB),
        compiler_params=_params("parallel", "arbitrary"),
        name="diff",
    )(q, k, vt, lam, g, bias)


def _mix_out_kernel(oa_ref, ob_ref, wa_ref, wb_ref, x_ref, g_ref, b_ref, o_ref):
    m = _dot(oa_ref[...], wa_ref[...]) + _dot(ob_ref[...], wb_ref[...])
    o_ref[...] = _layer_norm(ALPHA * x_ref[...] + m, g_ref[...], b_ref[...])


def _mix_out(oa, ob, wa, wb, x, g, b):
    t, d = x.shape
    row = lambda i: (i, 0)
    const = lambda i: (0, 0)
    return pl.pallas_call(
        _mix_out_kernel,
        grid=(t // TM_PROJ,),
        in_specs=[
            pl.BlockSpec((TM_PROJ, oa.shape[1]), row),
            pl.BlockSpec((TM_PROJ, ob.shape[1]), row),
            pl.BlockSpec(wa.shape, const),
            pl.BlockSpec(wb.shape, const),
            pl.BlockSpec((TM_PROJ, d), row),
            pl.BlockSpec((1, d), const),
            pl.BlockSpec((1, d), const),
        ],
        out_specs=pl.BlockSpec((TM_PROJ, d), row),
        out_shape=jax.ShapeDtypeStruct((t, d), F32),
        compiler_params=_params("parallel"),
        name="mix_out",
    )(oa, ob, wa, wb, x, g, b)


def _conv_in_kernel(x_ref, w_ref, o_ref):
    xb = x_ref[...].astype(BF16)
    d = o_ref.shape[1]
    a = _dot(xb, w_ref[:, 0:d])
    gate = _dot(xb, w_ref[:, d:2 * d])
    o_ref[...] = a * jax.nn.sigmoid(gate)


def _conv_in(x, w):
    t, d = x.shape
    return pl.pallas_call(
        _conv_in_kernel,
        grid=(t // TM_CONV,),
        in_specs=[pl.BlockSpec((TM_CONV, d), lambda i: (i, 0)),
                  pl.BlockSpec((d, 2 * d), lambda i: (0, 0))],
        out_specs=pl.BlockSpec((TM_CONV, d), lambda i: (i, 0)),
        out_shape=jax.ShapeDtypeStruct((t, d), F32),
        compiler_params=_params("parallel"),
        name="conv_in",
    )(x, w)


def _conv_out_kernel(h_ref, halo_ref, dw_ref, dwb_ref, cg_ref, cb_ref, w_ref, x_ref, g_ref, b_ref,
                     o_ref, buf_ref, cv_ref):
    i = pl.program_id(1)
    buf_ref[0:HALO, :] = jnp.where(i > 0, halo_ref[...], 0.0)
    buf_ref[HALO:, :] = h_ref[...]
    off = HALO - (CONV_W - 1)

    def chunk(c, carry):
        r0 = pl.multiple_of(c * CONV_ROWS, CONV_ROWS)
        nwin = CONV_ROWS + HALO
        for lt in range(dwb_ref.shape[1] // LANES):
            cols = slice(lt * LANES, (lt + 1) * LANES)
            win = buf_ref[pl.ds(r0, nwin), cols]
            acc = jnp.broadcast_to(dwb_ref[:, cols], (CONV_ROWS, LANES))
            for sub in range(SUBLANES):
                shifted = pltpu.roll(win, nwin - sub, axis=0) if sub else win
                for k in range(CONV_W):
                    if (off + k) % SUBLANES == sub:
                        a0 = off + k - sub
                        acc = acc + dw_ref[k:k + 1, cols] * shifted[a0:a0 + CONV_ROWS]
            cv_ref[pl.ds(r0, CONV_ROWS), cols] = acc
        return carry

    lax.fori_loop(0, TM_CONV // CONV_ROWS, chunk, 0)
    y = _layer_norm(cv_ref[...], cg_ref[...], cb_ref[...])
    u = (y * jax.nn.sigmoid(y)).astype(BF16)
    m = _dot(u, w_ref[...])
    o_ref[...] = _layer_norm(ALPHA * x_ref[...] + m, g_ref[...], b_ref[...])


def _conv_out(h, dw, dwb, cg, cb, w, x, g, b, batch, seq):
    t, d = x.shape
    nt = seq // TM_CONV
    per = TM_CONV // HALO
    row = lambda bb, i: (bb * nt + i, 0)
    halo = lambda bb, i: (jnp.maximum((bb * nt + i) * per - 1, 0), 0)
    const = lambda bb, i: (0, 0)
    return pl.pallas_call(
        _conv_out_kernel,
        grid=(batch, nt),
        in_specs=[
            pl.BlockSpec((TM_CONV, d), row),
            pl.BlockSpec((HALO, d), halo),
            pl.BlockSpec(dw.shape, const),
            pl.BlockSpec((1, d), const),
            pl.BlockSpec((1, d), const),
            pl.BlockSpec((1, d), const),
            pl.BlockSpec((d, d), const),
            pl.BlockSpec((TM_CONV, d), row),
            pl.BlockSpec((1, d), const),
            pl.BlockSpec((1, d), const),
        ],
        out_specs=pl.BlockSpec((TM_CONV, d), row),
        out_shape=jax.ShapeDtypeStruct((t, d), F32),
        scratch_shapes=[pltpu.VMEM((TM_CONV + HALO, d), F32), pltpu.VMEM((TM_CONV, d), F32)],
        compiler_params=_params("parallel", "arbitrary"),
        name="conv_out",
    )(h, h, dw, dwb, cg, cb, w, x, g, b)


def _attention_mixer(x, w_in, ikg, ikb, lam, subg, w_out, bias, g, b, batch, seq, layer_idx):
    c = [0]
    for n in (HA * DA, HA * DA, HA * DA, HI * DI, DI, HI, HB * 2 * DB, HB * 2 * DB, HB * 2 * DB):
        c.append(c[-1] + n)
    w_ik = w_in[:, c[4]:c[5]]
    w_iw = jnp.pad(w_in[:, c[5]:c[6]], ((0, 0), (0, LANES - HI)))
    w = jnp.concatenate([w_in[:, :c[4]], w_ik, w_ik, w_iw, w_in[:, c[6]:]], axis=1).astype(BF16)
    ikg2 = jnp.concatenate([ikg, ikg])[None]
    ikb2 = jnp.concatenate([ikb, ikb])[None]
    qa, ka, vat, iq, ik2, iwt, qb, kb, vbt = _mix_proj(x, w, ikg2, ikb2)
    o_a = _dsa(qa, ka, vat, iq, ik2, iwt, bias[:HA], batch, seq)
    lam_init = 0.8 - 0.6 * math.exp(-0.3 * layer_idx)
    o_b = _diff(qb, kb, vbt, lam, subg[None], bias[HA:], batch, seq, lam_init)
    wo = w_out.astype(BF16)
    return _mix_out(o_a, o_b, wo[:HA * DA], wo[HA * DA:], x, g, b)


def _conv_module(x, w_in, dw, dwb, cg, cb, w_out, g, b, batch, seq):
    h = _conv_in(x, w_in.astype(BF16))
    dwp = jnp.pad(dw, ((0, HALO - CONV_W), (0, 0)))
    return _conv_out(h, dwp, dwb[None], cg[None], cb[None], w_out.astype(BF16), x, g, b, batch, seq)


def kernel(x, ffn_in, ffn_out, ln_g, ln_b, rel_bias, mix_w_in, idx_k_g, idx_k_b, diff_lambda,
           diff_subln_g, mix_w_out, conv_w_in, conv_dw, conv_dw_b, conv_ln_g, conv_ln_b, conv_w_out):
    batch, seq, d = x.shape
    x = x.reshape(batch * seq, d)
    bias = _bias_tiles(rel_bias.T)
    ffn_in_b = ffn_in.astype(BF16)
    ffn_out_b = ffn_out.astype(BF16)
    for l in range(DEPTH):
        j = l // 2
        x = _ffn(x, ffn_in_b[l, 0], ffn_out_b[l, 0], ln_g[l, 0][None], ln_b[l, 0][None])
        if l % 2 == 0:
            x = _attention_mixer(x, mix_w_in[j], idx_k_g[j], idx_k_b[j], diff_lambda[j], diff_subln_g[j],
                                 mix_w_out[j], bias, ln_g[l, 1][None], ln_b[l, 1][None], batch, seq, l)
        else:
            x = _conv_module(x, conv_w_in[j], conv_dw[j], conv_dw_b[j], conv_ln_g[j], conv_ln_b[j],
                             conv_w_out[j], ln_g[l, 1][None], ln_b[l, 1][None], batch, seq)
        x = _ffn(x, ffn_in_b[l, 1], ffn_out_b[l, 1], ln_g[l, 2][None], ln_b[l, 2][None])
    return x.reshape(batch, seq, d)
```

```python
import functools
import math

import jax
import jax.numpy as jnp
from jax import lax
from jax.experimental import pallas as pl
from jax.experimental.pallas import tpu as pltpu

F32 = jnp.float32
BF16 = jnp.bfloat16

D_MODEL = 1024
DEPTH = 4
HA, DA = 8, 64
HI, DI = 8, 64
TOPK_MAX = 256
HB, DB = 4, 64
CONV_W = 31
D_FF = 2816
NUM_BUCKETS = 32
MAX_DISTANCE = 128
ALPHA = (2 * DEPTH) ** 0.25
LN_EPS = 1e-5

LANES = 128
SUBLANES = 8
BF16_ROWS = 16
VMEM_LIMIT = 56 * 1024 * 1024

TQ = 256
TK = 512
QB = TQ // LANES
KB = TK // LANES
CH = 64
TM_FFN = 512
TF_FFN = 256
TM_PROJ = TK
TM_CONV = 512
HALO = 32
CONV_ROWS = 128
NEG = -1e30
LOG2E = math.log2(math.e)
SEARCH_CAP = 1200


def _params(*sem):
    return pltpu.CompilerParams(dimension_semantics=sem, vmem_limit_bytes=VMEM_LIMIT)


def _layer_norm(v, g, b):
    mu = jnp.mean(v, axis=-1, keepdims=True)
    c = v - mu
    var = jnp.mean(c * c, axis=-1, keepdims=True)
    return c * lax.rsqrt(var + LN_EPS) * g + b


def _dot(a, b):
    return jnp.dot(a, b, preferred_element_type=F32)


def _dot_nt(a, b):
    return lax.dot_general(a, b, (((1,), (1,)), ((), ())), preferred_element_type=F32)


def _ffn_kernel(x_ref, wi_ref, wo_ref, g_ref, b_ref, o_ref, xb_ref, acc_ref):
    f = wo_ref.shape[0]
    xb_ref[...] = x_ref[...].astype(BF16)
    for c in range(f // TF_FFN):
        xb = xb_ref[...]
        a = _dot(xb, wi_ref[:, c * TF_FFN:(c + 1) * TF_FFN])
        u = _dot(xb, wi_ref[:, f + c * TF_FFN:f + (c + 1) * TF_FFN])
        h = (a * jax.nn.sigmoid(a) * u).astype(BF16)
        part = _dot(h, wo_ref[c * TF_FFN:(c + 1) * TF_FFN, :])
        if c == 0:
            acc_ref[...] = part
        else:
            acc_ref[...] += part
    y = ALPHA * x_ref[...] + 0.5 * acc_ref[...]
    o_ref[...] = _layer_norm(y, g_ref[...], b_ref[...])


def _ffn(x, w_in, w_out, g, b, layer, half):
    t, d = x.shape
    f = w_out.shape[2]
    once = pl.Buffered(1)
    ln = 3 * layer + 2 * half
    return pl.pallas_call(
        _ffn_kernel,
        grid=(t // TM_FFN,),
        in_specs=[
            pl.BlockSpec((TM_FFN, d), lambda i: (i, 0)),
            pl.BlockSpec((None, None, d, 2 * f), lambda i: (layer, half, 0, 0), pipeline_mode=once),
            pl.BlockSpec((None, None, f, d), lambda i: (layer, half, 0, 0), pipeline_mode=once),
            pl.BlockSpec((None, 1, d), lambda i: (ln, 0, 0)),
            pl.BlockSpec((None, 1, d), lambda i: (ln, 0, 0)),
        ],
        out_specs=pl.BlockSpec((TM_FFN, d), lambda i: (i, 0)),
        out_shape=jax.ShapeDtypeStruct((t, d), F32),
        scratch_shapes=[pltpu.VMEM((TM_FFN, d), BF16), pltpu.VMEM((TM_FFN, d), F32)],
        compiler_params=_params("parallel"),
        name="ffn",
    )(x, w_in, w_out, g, b)


_W3 = 3 * HA * DA
_C_IQ = _W3
_C_IK = _C_IQ + HI * DI
_C_IW = _C_IK + LANES
_C_B = _C_IW + LANES
_C_END = _C_B + 3 * HB * 2 * DB


def _mix_proj_kernel(x_ref, w_ref, ikg_ref, ikb_ref,
                     qa_ref, ka_ref, vat_ref, iq_ref, ik_ref, iwt_ref, qb_ref, kb_ref, vbt_ref):
    xb = x_ref[...].astype(BF16)
    hd = HA * DA
    qa_ref[...] = (_dot(xb, w_ref[:, 0:hd]) * (DA ** -0.5 * LOG2E)).astype(BF16)
    ka_ref[...] = _dot(xb, w_ref[:, hd:2 * hd]).astype(BF16)
    vat_ref[0] = _dot(xb, w_ref[:, 2 * hd:3 * hd]).T.astype(BF16)
    iq_ref[...] = (_dot(xb, w_ref[:, _C_IQ:_C_IK]) * (DI ** -0.5)).astype(BF16)
    ik2 = _dot(xb, w_ref[:, _C_IK:_C_IW])
    ik_ref[...] = _layer_norm(ik2, ikg_ref[...], ikb_ref[...]).astype(BF16)
    iw = _dot(xb, w_ref[:, _C_IW:_C_B]) * (HI ** -0.5)
    iwt_ref[...] = iw.T[0:HI, :]
    hb = HB * 2 * DB
    qb_ref[...] = (_dot(xb, w_ref[:, _C_B:_C_B + hb]) * (DB ** -0.5 * LOG2E)).astype(BF16)
    kb_ref[...] = _dot(xb, w_ref[:, _C_B + hb:_C_B + 2 * hb]).astype(BF16)
    vbt_ref[0] = _dot(xb, w_ref[:, _C_B + 2 * hb:_C_END]).T.astype(BF16)


def _mix_proj(x, w, ikg2, ikb2):
    t, d = x.shape
    nt = t // TM_PROJ
    row = lambda i: (i, 0)
    const = lambda i: (0, 0)
    tile = lambda i: (i, 0, 0)
    wide = lambda n, dt: jax.ShapeDtypeStruct((t, n), dt)
    hd, hb = HA * DA, HB * 2 * DB
    return pl.pallas_call(
        _mix_proj_kernel,
        grid=(nt,),
        in_specs=[
            pl.BlockSpec((TM_PROJ, d), row),
            pl.BlockSpec((d, _C_END), const),
            pl.BlockSpec((1, LANES), const),
            pl.BlockSpec((1, LANES), const),
        ],
        out_specs=[
            pl.BlockSpec((TM_PROJ, hd), row), pl.BlockSpec((TM_PROJ, hd), row),
            pl.BlockSpec((1, hd, TM_PROJ), tile), pl.BlockSpec((TM_PROJ, HI * DI), row),
            pl.BlockSpec((TM_PROJ, LANES), row), pl.BlockSpec((HI, TM_PROJ), lambda i: (0, i)),
            pl.BlockSpec((TM_PROJ, hb), row), pl.BlockSpec((TM_PROJ, hb), row),
            pl.BlockSpec((1, hb, TM_PROJ), tile),
        ],
        out_shape=[wide(hd, BF16), wide(hd, BF16), jax.ShapeDtypeStruct((nt, hd, TM_PROJ), BF16),
                   wide(HI * DI, BF16), wide(LANES, BF16), jax.ShapeDtypeStruct((HI, t), F32),
                   wide(hb, BF16), wide(hb, BF16), jax.ShapeDtypeStruct((nt, hb, TM_PROJ), BF16)],
        compiler_params=_params("parallel"),
        name="mix_proj",
    )(x, w, ikg2, ikb2)


def _bias_kernel(tbl_ref, o_ref):
    nh = o_ref.shape[0]
    kk = lax.broadcasted_iota(jnp.int32, (LANES, LANES), 0)
    qq = lax.broadcasted_iota(jnp.int32, (LANES, LANES), 1)
    max_exact = NUM_BUCKETS // 2
    for blk in range(2):
        n = jnp.maximum(qq - kk + blk * LANES, 0)
        nf = jnp.maximum(n, 1).astype(F32)
        large = max_exact + (jnp.log(nf / max_exact) / math.log(MAX_DISTANCE / max_exact)
                             * (NUM_BUCKETS - max_exact)).astype(jnp.int32)
        large = jnp.minimum(large, NUM_BUCKETS - 1)
        bucket = jnp.where(n < max_exact, n, large)
        for h in range(nh):
            acc = jnp.zeros((LANES, LANES), F32)
            for k in range(NUM_BUCKETS):
                acc = jnp.where(bucket == k, tbl_ref[h, k], acc)
            o_ref[h, blk] = (acc - tbl_ref[h, NUM_BUCKETS - 1]) * LOG2E


def _bias_tiles(rel_bias_t):
    nh = rel_bias_t.shape[0]
    return pl.pallas_call(
        _bias_kernel,
        in_specs=[pl.BlockSpec(memory_space=pltpu.SMEM)],
        out_shape=jax.ShapeDtypeStruct((nh, 2, LANES, LANES), F32),
        name="bias_tiles",
    )(rel_bias_t)


def _half_masked(x, upper):
    lane = lax.broadcasted_iota(jnp.int32, x.shape, 1)
    keep = (lane >= LANES // 2) if upper else (lane < LANES // 2)
    return jnp.where(keep, x, jnp.zeros_like(x))


def _near_bias_t(bias_ref, head, i, j, cc, r0):
    b0 = bias_ref[head, 0, r0:r0 + CH, :]
    b1 = bias_ref[head, 1, r0:r0 + CH, :]
    cols = []
    for qq in range(QB):
        delta = (i * QB + qq) - (j * KB + cc)
        cols.append(jnp.where(delta == 0, b0, jnp.where(delta == 1, b1, jnp.zeros_like(b0))))
    return jnp.concatenate(cols, axis=1)


def _fold8(x, op):
    return op(x.reshape(CH // SUBLANES, SUBLANES, x.shape[1]), axis=0)


def _flash_scores(par, maps, k_tile, mask_bias, near_bias, qm_ref, s_ref, m_ref, c_ref, only=None, st=None):
    nmap = len(maps)
    cur = par * nmap
    prv = nmap - cur
    nch = LANES // CH
    for s in (range(nmap) if only is None else (only,)):
        pair, _, head = maps[s]
        if st is None or only is None:
            st = _dot_nt(k_tile(pair), qm_ref[s])
        mx = jnp.full((SUBLANES, TQ), NEG, F32)
        for cc in range(KB):
            for hh in range(nch):
                r0 = cc * LANES + hh * CH
                piece = st[r0:r0 + CH]
                if mask_bias is not None:
                    piece = piece + mask_bias(r0)
                if near_bias is not None:
                    piece = piece + near_bias(head, cc, hh * CH)
                s_ref[cur + s, r0:r0 + CH, :] = piece
                mx = jnp.maximum(mx, _fold8(piece, jnp.max))
        m_prev = m_ref[prv + s]
        m_next = jnp.maximum(m_prev, jnp.max(mx, axis=0, keepdims=True))
        c_ref[cur + s] = jnp.exp2(m_prev - m_next)
        m_ref[cur + s] = m_next


def _flash_values(par, maps, vt_tile, s_ref, p_ref, m_ref, l_ref, c_ref, acc_ref, only=None, part=None):
    nmap = len(maps)
    cur = par * nmap
    for s in (range(nmap) if only is None else (only,)):
        if part != "pv":
            m_new = m_ref[cur + s]
            for r0 in range(0, TK, CH):
                p_ref[s, r0:r0 + CH, :] = jnp.exp2(s_ref[cur + s, r0:r0 + CH, :] - m_new).astype(BF16)
        if part == "probs":
            continue
        lhs = jnp.concatenate([vt_tile(maps[s][1]), jnp.ones((BF16_ROWS, TK), BF16)], axis=0)
        pv = _dot(lhs, p_ref[s])
        dv = acc_ref.shape[1]
        l_ref[s] = c_ref[cur + s] * l_ref[s] + pv[dv:dv + 1]
        acc_ref[s] = c_ref[cur + s] * acc_ref[s] + pv[:dv]


def _flash_sweep(jd, nmap, qk, prep_far, scores_far, prep_any, scores_any, values):
    def step(j_next, par_next, prep, scores):
        prep(j_next, par_next)
        st = qk(j_next, 0)
        for s in range(nmap):
            values(j_next - 1, 1 - par_next, s, "probs")
            st_after = qk(j_next, s + 1) if s + 1 < nmap else None
            scores(j_next, par_next, s, st)
            values(j_next - 1, 1 - par_next, s, "pv")
            st = st_after

    prep_any(0, 0)
    for s in range(nmap):
        scores_any(0, 0, s, None)

    def body(k, c):
        step(2 * k + 1, 1, prep_far, scores_far)
        step(2 * k + 2, 0, prep_far, scores_far)
        return c

    npair = jnp.maximum(jd - 2, 0) // 2
    lax.fori_loop(0, npair, body, 0)
    t = 2 * npair
    left = jd - t
    for r in range(1, 4):
        @pl.when(left >= r)
        def _(r=r):
            step(t + r, r % 2, prep_any, scores_any)

    for par in range(2):
        @pl.when(lax.rem(left, 2) == par)
        def _(par=par):
            values(jd, par, None, None)


def _flash_init(m_ref, l_ref, acc_ref):
    m_ref[...] = jnp.full(m_ref.shape, NEG, F32)
    l_ref[...] = jnp.zeros_like(l_ref)
    acc_ref[...] = jnp.zeros_like(acc_ref)


def _flash_scratch(nmap, dv):
    return [
        pltpu.VMEM((nmap, TQ, LANES), BF16),
        pltpu.VMEM((2 * nmap, TK, TQ), F32),
        pltpu.VMEM((nmap, TK, TQ), BF16),
        pltpu.VMEM((2 * nmap, 1, TQ), F32),
        pltpu.VMEM((nmap, 1, TQ), F32),
        pltpu.VMEM((2 * nmap, 1, TQ), F32),
        pltpu.VMEM((nmap, dv, TQ), F32),
        pltpu.VMEM((2, TK, TQ), F32),
    ]


def _dsa_kernel(q_ref, k_ref, vt_ref, iq_ref, ik_ref, iwt_ref, bias_ref, o_ref,
                sc_ref, qm_ref, s_ref, p_ref, m_ref, l_ref, c_ref, acc_ref, mb_ref, *, top_k):
    i = pl.program_id(1)
    t0 = i * TQ
    jd = t0 // TK
    ntile = jd + 1
    qpos = t0 + lax.broadcasted_iota(jnp.int32, (1, TQ), 1)

    iq = iq_ref[...]
    for h in range(HI):
        qm_ref[h] = _half_masked(iq[:, (h // 2) * LANES:(h // 2 + 1) * LANES], h % 2 == 1)
    iwt = iwt_ref[...]

    def score_tile(j, carry, diag):
        rmax, rmin = carry
        ik_t = ik_ref[pl.ds(pl.multiple_of(j * TK, TK), TK), :]
        for hp in range(HI // 2):
            da = _dot_nt(ik_t, qm_ref[2 * hp])
            db = _dot_nt(ik_t, qm_ref[2 * hp + 1])
            wa = iwt[2 * hp:2 * hp + 1, :]
            wb = iwt[2 * hp + 1:2 * hp + 2, :]
            for r0 in range(0, TK, CH):
                piece = wa * jnp.maximum(da[r0:r0 + CH], 0.0) + wb * jnp.maximum(db[r0:r0 + CH], 0.0)
                if hp > 0:
                    piece = piece + sc_ref[j, r0:r0 + CH, :]
                if hp == HI // 2 - 1:
                    if diag:
                        kpos = j * TK + r0 + lax.broadcasted_iota(jnp.int32, (CH, TQ), 0)
                        valid = kpos <= qpos
                        rmax = jnp.maximum(rmax, _fold8(jnp.where(valid, piece, -jnp.inf), jnp.max))
                        rmin = jnp.minimum(rmin, _fold8(jnp.where(valid, piece, jnp.inf), jnp.min))
                        piece = jnp.where(valid, piece, -jnp.inf)
                    else:
                        rmax = jnp.maximum(rmax, _fold8(piece, jnp.max))
                        rmin = jnp.minimum(rmin, _fold8(piece, jnp.min))
                sc_ref[j, r0:r0 + CH, :] = piece
        return rmax, rmin

    carry = lax.fori_loop(
        0, jd, lambda j, c: score_tile(j, c, False),
        (jnp.full((SUBLANES, TQ), -jnp.inf, F32), jnp.full((SUBLANES, TQ), jnp.inf, F32)))
    rmax8, rmin8 = score_tile(jd, carry, True)
    rmax = jnp.max(rmax8, axis=0, keepdims=True)
    rmin = jnp.min(rmin8, axis=0, keepdims=True)

    def row_count(pred):
        def body(j, cnt):
            for r0 in range(0, TK, CH):
                cnt = cnt + _fold8(jnp.where(pred(sc_ref[j, r0:r0 + CH, :]), 1.0, 0.0), jnp.sum)
            return cnt
        cnt = lax.fori_loop(0, ntile, body, jnp.zeros((SUBLANES, TQ), F32))
        return jnp.sum(cnt, axis=0, keepdims=True)

    def count_ge(x):
        return row_count(lambda t: t >= x)

    kf = float(top_k)
    nvalid = (qpos + 1).astype(F32)
    few = nvalid <= kf
    c_ge0 = count_ge(0.0)
    c_gt0 = row_count(lambda t: t > 0.0)
    at0 = jnp.logical_and(c_gt0 < kf, c_ge0 >= kf)
    pos = c_gt0 >= kf
    lo0 = jnp.where(few, -3.0e38, jnp.where(pos, 0.0, jnp.where(at0, 0.0, rmin)))
    hi0 = jnp.where(pos, rmax * 1.000001 + 1e-37, 0.0)
    clo0 = jnp.where(few, kf, jnp.where(jnp.logical_or(pos, at0), c_ge0, nvalid))
    chi0 = jnp.where(pos, 0.0, c_ge0)

    def settled(clo, chi):
        return jnp.logical_or(clo == kf, kf - chi == 1.0)

    done0 = jnp.where(jnp.logical_or(jnp.logical_or(few, at0), settled(clo0, chi0)), 1.0, 0.0)

    def search_cond(st):
        it, _, _, _, _, done = st
        return jnp.logical_and(it < SEARCH_CAP, jnp.min(done) < 0.5)

    def search_body(st):
        it, lo, hi, clo, chi, done = st
        mid = lo + (hi - lo) * 0.5
        stuck = jnp.logical_or(mid <= lo, mid >= hi)
        cx = count_ge(mid)
        ge = cx >= kf
        act = done < 0.5
        up = jnp.logical_and(act, ge)
        dn = jnp.logical_and(act, jnp.logical_not(ge))
        lo = jnp.where(up, mid, lo)
        clo = jnp.where(up, cx, clo)
        hi = jnp.where(dn, mid, hi)
        chi = jnp.where(dn, cx, chi)
        done = jnp.where(jnp.logical_or(stuck, settled(clo, chi)), 1.0, done)
        return it + 1, lo, hi, clo, chi, done

    _, lo, hi, clo, chi, _ = lax.while_loop(
        search_cond, search_body, (jnp.int32(0), lo0, hi0, clo0, chi0, done0))

    def below_hi_max(j, mx):
        for r0 in range(0, TK, CH):
            t = sc_ref[j, r0:r0 + CH, :]
            mx = jnp.maximum(mx, _fold8(jnp.where(t < hi, t, -jnp.inf), jnp.max))
        return mx

    top_below = jnp.max(lax.fori_loop(0, ntile, below_hi_max, jnp.full((SUBLANES, TQ), -jnp.inf, F32)),
                        axis=0, keepdims=True)
    last_one = jnp.logical_and(jnp.logical_and(jnp.logical_not(few), clo != kf), kf - chi == 1.0)
    theta = jnp.where(last_one, top_below, lo)
    c_theta = count_ge(theta)

    surplus = jnp.where(jnp.logical_and(jnp.logical_not(few), c_theta > kf), c_theta - kf, 0.0)

    @pl.when(jnp.max(surplus) > 0.5)
    def _():
        ra = lax.broadcasted_iota(jnp.int32, (TK, TK), 0)
        ca = lax.broadcasted_iota(jnp.int32, (TK, TK), 1)
        triu = jnp.where(ca >= ra, 1.0, 0.0).astype(BF16)

        def body(jj, later):
            j = ntile - 1 - jj
            cnt = jnp.zeros((SUBLANES, TQ), F32)
            for r0 in range(0, TK, CH):
                eqf = jnp.where(sc_ref[j, r0:r0 + CH, :] == theta, 1.0, 0.0)
                p_ref[0, r0:r0 + CH, :] = eqf.astype(BF16)
                cnt = cnt + _fold8(eqf, jnp.sum)
            from_end = _dot(triu, p_ref[0])
            for r0 in range(0, TK, CH):
                t = sc_ref[j, r0:r0 + CH, :]
                cut = jnp.where(from_end[r0:r0 + CH] + later <= surplus, -jnp.inf, t)
                sc_ref[j, r0:r0 + CH, :] = jnp.where(t == theta, cut, t)
            return later + jnp.sum(cnt, axis=0, keepdims=True)

        lax.fori_loop(0, ntile, body, jnp.zeros((1, TQ), F32))

    _flash_init(m_ref, l_ref, acc_ref)
    q = q_ref[...]
    for h in range(HA):
        qm_ref[h] = _half_masked(q[:, (h // 2) * LANES:(h // 2 + 1) * LANES], h % 2 == 1)
    maps = [(h // 2, h, h) for h in range(HA)]

    def prep(j, par):
        for r0 in range(0, TK, CH):
            mb_ref[par, r0:r0 + CH, :] = jnp.where(sc_ref[j, r0:r0 + CH, :] >= theta, 0.0, NEG)

    def k_tile(j):
        row = pl.ds(pl.multiple_of(j * TK, TK), TK)
        return lambda p: k_ref[row, p * LANES:(p + 1) * LANES]

    def qk(j, s):
        return _dot_nt(k_tile(j)(maps[s][0]), qm_ref[s])

    def scores(j, par, near, only, st):
        _flash_scores(
            par, maps, k_tile(j),
            lambda r0: mb_ref[par, r0:r0 + CH, :],
            (lambda h, cc, r: _near_bias_t(bias_ref, h, i, j, cc, r)) if near else None,
            qm_ref, s_ref, m_ref, c_ref, only, st)

    def values(j, par, only, part):
        _flash_values(par, maps, lambda h: vt_ref[j, h * DA:(h + 1) * DA, :],
                      s_ref, p_ref, m_ref, l_ref, c_ref, acc_ref, only, part)

    _flash_sweep(jd, HA, qk, prep, lambda j, par, s, st: scores(j, par, False, s, st),
                 prep, lambda j, par, s, st: scores(j, par, True, s, st), values)

    for p in range(HA // 2):
        o_lo = acc_ref[2 * p] / l_ref[2 * p]
        o_hi = acc_ref[2 * p + 1] / l_ref[2 * p + 1]
        o_t = jnp.concatenate([o_lo, o_hi], axis=0)
        o_ref[:, p * LANES:(p + 1) * LANES] = o_t.T.astype(BF16)


def _dsa(q, k, vt, iq, ik2, iwt, bias, batch, seq):
    top_k = min(TOPK_MAX, seq // 4)
    nq = seq // TQ
    nt = seq // TK
    hd = HA * DA
    qrow = lambda b, i: (b * nq + i, 0)
    whole = lambda b, i: (b, 0)
    once = pl.Buffered(1)
    return pl.pallas_call(
        functools.partial(_dsa_kernel, top_k=top_k),
        grid=(batch, nq),
        in_specs=[
            pl.BlockSpec((TQ, hd), qrow),
            pl.BlockSpec((seq, hd), whole, pipeline_mode=once),
            pl.BlockSpec((nt, hd, TK), lambda b, i: (b, 0, 0), pipeline_mode=once),
            pl.BlockSpec((TQ, HI * DI), qrow),
            pl.BlockSpec((seq, LANES), whole, pipeline_mode=once),
            pl.BlockSpec((HI, TQ), lambda b, i: (0, b * nq + i)),
            pl.BlockSpec((HA, 2, LANES, LANES), lambda b, i: (0, 0, 0, 0)),
        ],
        out_specs=pl.BlockSpec((TQ, hd), qrow),
        out_shape=jax.ShapeDtypeStruct((batch * seq, hd), BF16),
        scratch_shapes=[pltpu.VMEM((nt, TK, TQ), F32)] + _flash_scratch(HA, DA),
        compiler_params=_params("parallel", "arbitrary"),
        name="dsa",
    )(q, k, vt, iq, ik2, iwt, bias)


def _diff_kernel(q_ref, k_ref, vt_ref, lam_ref, g_ref, bias_ref, o_ref,
                 qm_ref, s_ref, p_ref, m_ref, l_ref, c_ref, acc_ref, mb_ref, *, lam_init):
    i = pl.program_id(1)
    t0 = i * TQ
    jd = t0 // TK
    _flash_init(m_ref, l_ref, acc_ref)
    q = q_ref[...]
    nmap = 2 * HB
    for s in range(nmap):
        qm_ref[s] = _half_masked(q[:, (s // 2) * LANES:(s // 2 + 1) * LANES], s % 2 == 1)
    maps = [(s // 2, s // 2, s // 2) for s in range(nmap)]
    qpos = t0 + lax.broadcasted_iota(jnp.int32, (1, TQ), 1)

    def prep(j, par):
        for r0 in range(0, TK, CH):
            kpos = j * TK + r0 + lax.broadcasted_iota(jnp.int32, (CH, TQ), 0)
            mb_ref[par, r0:r0 + CH, :] = jnp.where(kpos <= qpos, 0.0, NEG)

    def k_tile(j):
        row = pl.ds(pl.multiple_of(j * TK, TK), TK)
        return lambda p: k_ref[row, p * LANES:(p + 1) * LANES]

    def qk(j, s):
        return _dot_nt(k_tile(j)(maps[s][0]), qm_ref[s])

    def scores(j, par, general, only, st):
        _flash_scores(
            par, maps, k_tile(j),
            (lambda r0: mb_ref[par, r0:r0 + CH, :]) if general else None,
            (lambda h, cc, r: _near_bias_t(bias_ref, h, i, j, cc, r)) if general else None,
            qm_ref, s_ref, m_ref, c_ref, only, st)

    def values(j, par, only, part):
        _flash_values(par, maps, lambda p: vt_ref[j, p * LANES:(p + 1) * LANES, :],
                      s_ref, p_ref, m_ref, l_ref, c_ref, acc_ref, only, part)

    _flash_sweep(jd, nmap, qk, lambda j, par: None, lambda j, par, s, st: scores(j, par, False, s, st),
                 prep, lambda j, par, s, st: scores(j, par, True, s, st), values)

    lam = lam_ref[...]
    lam_full = (jnp.exp(jnp.sum(lam[0:1] * lam[1:2], axis=1, keepdims=True))
                - jnp.exp(jnp.sum(lam[2:3] * lam[3:4], axis=1, keepdims=True)) + lam_init)
    for h in range(HB):
        o0 = acc_ref[2 * h] / l_ref[2 * h]
        o1 = acc_ref[2 * h + 1] / l_ref[2 * h + 1]
        o = (o0 - lam_full * o1).T
        y = o * lax.rsqrt(jnp.mean(o * o, axis=-1, keepdims=True) + LN_EPS) * g_ref[...]
        o_ref[:, h * LANES:(h + 1) * LANES] = (y * (1.0 - lam_init)).astype(BF16)


def _diff(q, k, vt, lam, g, bias, batch, seq, lam_init):
    nq = seq // TQ
    nt = seq // TK
    hb = HB * 2 * DB
    qrow = lambda b, i: (b * nq + i, 0)
    whole = lambda b, i: (b, 0)
    once = pl.Buffered(1)
    return pl.pallas_call(
        functools.partial(_diff_kernel, lam_init=lam_init),
        grid=(batch, nq),
        in_specs=[
            pl.BlockSpec((TQ, hb), qrow),
            pl.BlockSpec((seq, hb), whole, pipeline_mode=once),
            pl.BlockSpec((nt, hb, TK), lambda b, i: (b, 0, 0), pipeline_mode=once),
            pl.BlockSpec((4, DB), lambda b, i: (0, 0)),
            pl.BlockSpec((1, 2 * DB), lambda b, i: (0, 0)),
            pl.BlockSpec((HB, 2, LANES, LANES), lambda b, i: (0, 0, 0, 0)),
        ],
        out_specs=pl.BlockSpec((TQ, hb), qrow),
        out_shape=jax.ShapeDtypeStruct((batch * seq, hb), BF16),
        scratch_shapes=_flash_scratch(2 * HB, 2 * DB),
        compiler_params=_params("parallel", "arbitrary"),
        name="diff",
    )(q, k, vt, lam, g, bias)


def _mix_out_kernel(oa_ref, ob_ref, wa_ref, wb_ref, x_ref, g_ref, b_ref, o_ref):
    m = _dot(oa_ref[...], wa_ref[...]) + _dot(ob_ref[...], wb_ref[...])
    o_ref[...] = _layer_norm(ALPHA * x_ref[...] + m, g_ref[...], b_ref[...])


def _mix_out(oa, ob, wa, wb, x, g, b):
    t, d = x.shape
    row = lambda i: (i, 0)
    const = lambda i: (0, 0)
    return pl.pallas_call(
        _mix_out_kernel,
        grid=(t // TM_PROJ,),
        in_specs=[
            pl.BlockSpec((TM_PROJ, oa.shape[1]), row),
            pl.BlockSpec((TM_PROJ, ob.shape[1]), row),
            pl.BlockSpec(wa.shape, const),
            pl.BlockSpec(wb.shape, const),
            pl.BlockSpec((TM_PROJ, d), row),
            pl.BlockSpec((1, d), const),
            pl.BlockSpec((1, d), const),
        ],
        out_specs=pl.BlockSpec((TM_PROJ, d), row),
        out_shape=jax.ShapeDtypeStruct((t, d), F32),
        compiler_params=_params("parallel"),
        name="mix_out",
    )(oa, ob, wa, wb, x, g, b)


def _conv_in_kernel(x_ref, w_ref, o_ref):
    xb = x_ref[...].astype(BF16)
    d = o_ref.shape[1]
    a = _dot(xb, w_ref[:, 0:d])
    gate = _dot(xb, w_ref[:, d:2 * d])
    o_ref[...] = a * jax.nn.sigmoid(gate)


def _conv_in(x, w):
    t, d = x.shape
    return pl.pallas_call(
        _conv_in_kernel,
        grid=(t // TM_CONV,),
        in_specs=[pl.BlockSpec((TM_CONV, d), lambda i: (i, 0)),
                  pl.BlockSpec((d, 2 * d), lambda i: (0, 0))],
        out_specs=pl.BlockSpec((TM_CONV, d), lambda i: (i, 0)),
        out_shape=jax.ShapeDtypeStruct((t, d), F32),
        compiler_params=_params("parallel"),
        name="conv_in",
    )(x, w)


def _conv_out_kernel(h_ref, halo_ref, dw_ref, dwb_ref, cg_ref, cb_ref, w_ref, x_ref, g_ref, b_ref,
                     o_ref, buf_ref, cv_ref):
    i = pl.program_id(1)
    buf_ref[0:HALO, :] = jnp.where(i > 0, halo_ref[...], 0.0)
    buf_ref[HALO:, :] = h_ref[...]
    off = HALO - (CONV_W - 1)

    def chunk(c, carry):
        r0 = pl.multiple_of(c * CONV_ROWS, CONV_ROWS)
        nwin = CONV_ROWS + HALO
        for lt in range(dwb_ref.shape[1] // LANES):
            cols = slice(lt * LANES, (lt + 1) * LANES)
            win = buf_ref[pl.ds(r0, nwin), cols]
            acc = jnp.broadcast_to(dwb_ref[:, cols], (CONV_ROWS, LANES))
            for sub in range(SUBLANES):
                shifted = pltpu.roll(win, nwin - sub, axis=0) if sub else win
                for k in range(CONV_W):
                    if (off + k) % SUBLANES == sub:
                        a0 = off + k - sub
                        acc = acc + dw_ref[k:k + 1, cols] * shifted[a0:a0 + CONV_ROWS]
            cv_ref[pl.ds(r0, CONV_ROWS), cols] = acc
        return carry

    lax.fori_loop(0, TM_CONV // CONV_ROWS, chunk, 0)
    y = _layer_norm(cv_ref[...], cg_ref[...], cb_ref[...])
    u = (y * jax.nn.sigmoid(y)).astype(BF16)
    m = _dot(u, w_ref[...])
    o_ref[...] = _layer_norm(ALPHA * x_ref[...] + m, g_ref[...], b_ref[...])


def _conv_out(h, dw, dwb, cg, cb, w, x, g, b, batch, seq):
    t, d = x.shape
    nt = seq // TM_CONV
    per = TM_CONV // HALO
    row = lambda bb, i: (bb * nt + i, 0)
    halo = lambda bb, i: (jnp.maximum((bb * nt + i) * per - 1, 0), 0)
    const = lambda bb, i: (0, 0)
    return pl.pallas_call(
        _conv_out_kernel,
        grid=(batch, nt),
        in_specs=[
            pl.BlockSpec((TM_CONV, d), row),
            pl.BlockSpec((HALO, d), halo),
            pl.BlockSpec(dw.shape, const),
            pl.BlockSpec((1, d), const),
            pl.BlockSpec((1, d), const),
            pl.BlockSpec((1, d), const),
            pl.BlockSpec((d, d), const),
            pl.BlockSpec((TM_CONV, d), row),
            pl.BlockSpec((1, d), const),
            pl.BlockSpec((1, d), const),
        ],
        out_specs=pl.BlockSpec((TM_CONV, d), row),
        out_shape=jax.ShapeDtypeStruct((t, d), F32),
        scratch_shapes=[pltpu.VMEM((TM_CONV + HALO, d), F32), pltpu.VMEM((TM_CONV, d), F32)],
        compiler_params=_params("parallel", "arbitrary"),
        name="conv_out",
    )(h, h, dw, dwb, cg, cb, w, x, g, b)


def _attention_mixer(x, w_in, ikg, ikb, lam, subg, w_out, bias, g, b, batch, seq, layer_idx):
    c = [0]
    for n in (HA * DA, HA * DA, HA * DA, HI * DI, DI, HI, HB * 2 * DB, HB * 2 * DB, HB * 2 * DB):
        c.append(c[-1] + n)
    w_ik = w_in[:, c[4]:c[5]]
    w_iw = jnp.pad(w_in[:, c[5]:c[6]], ((0, 0), (0, LANES - HI)))
    w = jnp.concatenate([w_in[:, :c[4]], w_ik, w_ik, w_iw, w_in[:, c[6]:]], axis=1).astype(BF16)
    ikg2 = jnp.concatenate([ikg, ikg])[None]
    ikb2 = jnp.concatenate([ikb, ikb])[None]
    qa, ka, vat, iq, ik2, iwt, qb, kb, vbt = _mix_proj(x, w, ikg2, ikb2)
    o_a = _dsa(qa, ka, vat, iq, ik2, iwt, bias[:HA], batch, seq)
    lam_init = 0.8 - 0.6 * math.exp(-0.3 * layer_idx)
    o_b = _diff(qb, kb, vbt, lam, subg[None], bias[HA:], batch, seq, lam_init)
    wo = w_out.astype(BF16)
    return _mix_out(o_a, o_b, wo[:HA * DA], wo[HA * DA:], x, g, b)


def _conv_module(x, w_in, dw, dwb, cg, cb, w_out, g, b, batch, seq):
    h = _conv_in(x, w_in.astype(BF16))
    dwp = jnp.pad(dw, ((0, HALO - CONV_W), (0, 0)))
    return _conv_out(h, dwp, dwb[None], cg[None], cb[None], w_out.astype(BF16), x, g, b, batch, seq)


def kernel(x, ffn_in, ffn_out, ln_g, ln_b, rel_bias, mix_w_in, idx_k_g, idx_k_b, diff_lambda,
           diff_subln_g, mix_w_out, conv_w_in, conv_dw, conv_dw_b, conv_ln_g, conv_ln_b, conv_w_out):
    batch, seq, d = x.shape
    x = x.reshape(batch * seq, d)
    bias = _bias_tiles(rel_bias.T)
    ffn_in_b = ffn_in.astype(BF16)
    ffn_out_b = ffn_out.astype(BF16)
    g_all = ln_g.reshape(3 * DEPTH, 1, d)
    b_all = ln_b.reshape(3 * DEPTH, 1, d)
    for l in range(DEPTH):
        j = l // 2
        x = _ffn(x, ffn_in_b, ffn_out_b, g_all, b_all, l, 0)
        if l % 2 == 0:
            x = _attention_mixer(x, mix_w_in[j], idx_k_g[j], idx_k_b[j], diff_lambda[j], diff_subln_g[j],
                                 mix_w_out[j], bias, ln_g[l, 1][None], ln_b[l, 1][None], batch, seq, l)
        else:
            x = _conv_module(x, conv_w_in[j], conv_dw[j], conv_dw_b[j], conv_ln_g[j], conv_ln_b[j],
                             conv_w_out[j], ln_g[l, 1][None], ln_b[l, 1][None], batch, seq)
        x = _ffn(x, ffn_in_b, ffn_out_b, g_all, b_all, l, 1)
    return x.reshape(batch, seq, d)
```

```python
import functools
import math

import jax
import jax.numpy as jnp
from jax import lax
from jax.experimental import pallas as pl
from jax.experimental.pallas import tpu as pltpu

F32 = jnp.float32
BF16 = jnp.bfloat16

D_MODEL = 1024
DEPTH = 4
HA, DA = 8, 64
HI, DI = 8, 64
TOPK_MAX = 256
HB, DB = 4, 64
CONV_W = 31
D_FF = 2816
NUM_BUCKETS = 32
MAX_DISTANCE = 128
ALPHA = (2 * DEPTH) ** 0.25
LN_EPS = 1e-5

LANES = 128
SUBLANES = 8
BF16_ROWS = 16
VMEM_LIMIT = 56 * 1024 * 1024

TQ = 256
TK = 512
QB = TQ // LANES
KB = TK // LANES
CH = 64
TM_FFN = 512
TF_FFN = 256
TM_PROJ = TK
TM_CONV = 512
HALO = 32
CONV_ROWS = 128
NEG = -1e30
LOG2E = math.log2(math.e)
SEARCH_CAP = 1200
GRID_CAP = 300


def _params(*sem):
    return pltpu.CompilerParams(dimension_semantics=sem, vmem_limit_bytes=VMEM_LIMIT)


def _layer_norm(v, g, b):
    mu = jnp.mean(v, axis=-1, keepdims=True)
    c = v - mu
    var = jnp.mean(c * c, axis=-1, keepdims=True)
    return c * lax.rsqrt(var + LN_EPS) * g + b


def _dot(a, b):
    return jnp.dot(a, b, preferred_element_type=F32)


def _dot_nt(a, b):
    return lax.dot_general(a, b, (((1,), (1,)), ((), ())), preferred_element_type=F32)


def _ffn_kernel(x_ref, wi_ref, wo_ref, g_ref, b_ref, o_ref, xb_ref, acc_ref):
    f = wo_ref.shape[0]
    xb_ref[...] = x_ref[...].astype(BF16)
    for c in range(f // TF_FFN):
        xb = xb_ref[...]
        a = _dot(xb, wi_ref[:, c * TF_FFN:(c + 1) * TF_FFN])
        u = _dot(xb, wi_ref[:, f + c * TF_FFN:f + (c + 1) * TF_FFN])
        h = (a * jax.nn.sigmoid(a) * u).astype(BF16)
        part = _dot(h, wo_ref[c * TF_FFN:(c + 1) * TF_FFN, :])
        if c == 0:
            acc_ref[...] = part
        else:
            acc_ref[...] += part
    y = ALPHA * x_ref[...] + 0.5 * acc_ref[...]
    o_ref[...] = _layer_norm(y, g_ref[...], b_ref[...])


def _ffn(x, w_in, w_out, g, b, layer, half):
    t, d = x.shape
    f = w_out.shape[2]
    once = pl.Buffered(1)
    ln = 3 * layer + 2 * half
    return pl.pallas_call(
        _ffn_kernel,
        grid=(t // TM_FFN,),
        in_specs=[
            pl.BlockSpec((TM_FFN, d), lambda i: (i, 0)),
            pl.BlockSpec((None, None, d, 2 * f), lambda i: (layer, half, 0, 0), pipeline_mode=once),
            pl.BlockSpec((None, None, f, d), lambda i: (layer, half, 0, 0), pipeline_mode=once),
            pl.BlockSpec((None, 1, d), lambda i: (ln, 0, 0)),
            pl.BlockSpec((None, 1, d), lambda i: (ln, 0, 0)),
        ],
        out_specs=pl.BlockSpec((TM_FFN, d), lambda i: (i, 0)),
        out_shape=jax.ShapeDtypeStruct((t, d), F32),
        scratch_shapes=[pltpu.VMEM((TM_FFN, d), BF16), pltpu.VMEM((TM_FFN, d), F32)],
        compiler_params=_params("parallel"),
        name="ffn",
    )(x, w_in, w_out, g, b)


_W3 = 3 * HA * DA
_C_IQ = _W3
_C_IK = _C_IQ + HI * DI
_C_IW = _C_IK + LANES
_C_B = _C_IW + LANES
_C_END = _C_B + 3 * HB * 2 * DB


def _mix_proj_kernel(x_ref, w_ref, ikg_ref, ikb_ref,
                     qa_ref, ka_ref, vat_ref, iq_ref, ik_ref, iwt_ref, qb_ref, kb_ref, vbt_ref):
    xb = x_ref[...].astype(BF16)
    hd = HA * DA
    qa_ref[...] = (_dot(xb, w_ref[:, 0:hd]) * (DA ** -0.5 * LOG2E)).astype(BF16)
    ka_ref[...] = _dot(xb, w_ref[:, hd:2 * hd]).astype(BF16)
    vat_ref[0] = _dot(xb, w_ref[:, 2 * hd:3 * hd]).T.astype(BF16)
    iq_ref[...] = (_dot(xb, w_ref[:, _C_IQ:_C_IK]) * (DI ** -0.5)).astype(BF16)
    ik2 = _dot(xb, w_ref[:, _C_IK:_C_IW])
    ik_ref[...] = _layer_norm(ik2, ikg_ref[...], ikb_ref[...]).astype(BF16)
    iw = _dot(xb, w_ref[:, _C_IW:_C_B]) * (HI ** -0.5)
    iwt_ref[...] = iw.T[0:HI, :]
    hb = HB * 2 * DB
    qb_ref[...] = (_dot(xb, w_ref[:, _C_B:_C_B + hb]) * (DB ** -0.5 * LOG2E)).astype(BF16)
    kb_ref[...] = _dot(xb, w_ref[:, _C_B + hb:_C_B + 2 * hb]).astype(BF16)
    vbt_ref[0] = _dot(xb, w_ref[:, _C_B + 2 * hb:_C_END]).T.astype(BF16)


def _mix_proj(x, w, ikg2, ikb2):
    t, d = x.shape
    nt = t // TM_PROJ
    row = lambda i: (i, 0)
    const = lambda i: (0, 0)
    tile = lambda i: (i, 0, 0)
    wide = lambda n, dt: jax.ShapeDtypeStruct((t, n), dt)
    hd, hb = HA * DA, HB * 2 * DB
    return pl.pallas_call(
        _mix_proj_kernel,
        grid=(nt,),
        in_specs=[
            pl.BlockSpec((TM_PROJ, d), row),
            pl.BlockSpec((d, _C_END), const),
            pl.BlockSpec((1, LANES), const),
            pl.BlockSpec((1, LANES), const),
        ],
        out_specs=[
            pl.BlockSpec((TM_PROJ, hd), row), pl.BlockSpec((TM_PROJ, hd), row),
            pl.BlockSpec((1, hd, TM_PROJ), tile), pl.BlockSpec((TM_PROJ, HI * DI), row),
            pl.BlockSpec((TM_PROJ, LANES), row), pl.BlockSpec((HI, TM_PROJ), lambda i: (0, i)),
            pl.BlockSpec((TM_PROJ, hb), row), pl.BlockSpec((TM_PROJ, hb), row),
            pl.BlockSpec((1, hb, TM_PROJ), tile),
        ],
        out_shape=[wide(hd, BF16), wide(hd, BF16), jax.ShapeDtypeStruct((nt, hd, TM_PROJ), BF16),
                   wide(HI * DI, BF16), wide(LANES, BF16), jax.ShapeDtypeStruct((HI, t), F32),
                   wide(hb, BF16), wide(hb, BF16), jax.ShapeDtypeStruct((nt, hb, TM_PROJ), BF16)],
        compiler_params=_params("parallel"),
        name="mix_proj",
    )(x, w, ikg2, ikb2)


def _bias_kernel(tbl_ref, o_ref):
    nh = o_ref.shape[0]
    kk = lax.broadcasted_iota(jnp.int32, (LANES, LANES), 0)
    qq = lax.broadcasted_iota(jnp.int32, (LANES, LANES), 1)
    max_exact = NUM_BUCKETS // 2
    for blk in range(2):
        n = jnp.maximum(qq - kk + blk * LANES, 0)
        nf = jnp.maximum(n, 1).astype(F32)
        large = max_exact + (jnp.log(nf / max_exact) / math.log(MAX_DISTANCE / max_exact)
                             * (NUM_BUCKETS - max_exact)).astype(jnp.int32)
        large = jnp.minimum(large, NUM_BUCKETS - 1)
        bucket = jnp.where(n < max_exact, n, large)
        for h in range(nh):
            acc = jnp.zeros((LANES, LANES), F32)
            for k in range(NUM_BUCKETS):
                acc = jnp.where(bucket == k, tbl_ref[h, k], acc)
            o_ref[h, blk] = (acc - tbl_ref[h, NUM_BUCKETS - 1]) * LOG2E


def _bias_tiles(rel_bias_t):
    nh = rel_bias_t.shape[0]
    return pl.pallas_call(
        _bias_kernel,
        in_specs=[pl.BlockSpec(memory_space=pltpu.SMEM)],
        out_shape=jax.ShapeDtypeStruct((nh, 2, LANES, LANES), F32),
        name="bias_tiles",
    )(rel_bias_t)


def _half_masked(x, upper):
    lane = lax.broadcasted_iota(jnp.int32, x.shape, 1)
    keep = (lane >= LANES // 2) if upper else (lane < LANES // 2)
    return jnp.where(keep, x, jnp.zeros_like(x))


def _near_bias_t(bias_ref, head, i, j, cc, r0):
    b0 = bias_ref[head, 0, r0:r0 + CH, :]
    b1 = bias_ref[head, 1, r0:r0 + CH, :]
    cols = []
    for qq in range(QB):
        delta = (i * QB + qq) - (j * KB + cc)
        cols.append(jnp.where(delta == 0, b0, jnp.where(delta == 1, b1, jnp.zeros_like(b0))))
    return jnp.concatenate(cols, axis=1)


def _floor_to_bf16(x):
    b = pltpu.bitcast(x, jnp.int32)
    b = (b + ((b >> 31) & 0xFFFF)) & jnp.int32(-65536)
    return pltpu.bitcast(b, F32).astype(jnp.bfloat16)


def _fold8(x, op):
    return op(x.reshape(CH // SUBLANES, SUBLANES, x.shape[1]), axis=0)


def _flash_scores(par, maps, k_tile, mask_bias, near_bias, qm_ref, s_ref, m_ref, c_ref, only=None, st=None):
    nmap = len(maps)
    cur = par * nmap
    prv = nmap - cur
    nch = LANES // CH
    for s in (range(nmap) if only is None else (only,)):
        pair, _, head = maps[s]
        if st is None or only is None:
            st = _dot_nt(k_tile(pair), qm_ref[s])
        mx = jnp.full((SUBLANES, TQ), NEG, F32)
        for cc in range(KB):
            for hh in range(nch):
                r0 = cc * LANES + hh * CH
                piece = st[r0:r0 + CH]
                if mask_bias is not None:
                    piece = piece + mask_bias(r0)
                if near_bias is not None:
                    piece = piece + near_bias(head, cc, hh * CH)
                s_ref[cur + s, r0:r0 + CH, :] = piece
                mx = jnp.maximum(mx, _fold8(piece, jnp.max))
        m_prev = m_ref[prv + s]
        m_next = jnp.maximum(m_prev, jnp.max(mx, axis=0, keepdims=True))
        c_ref[cur + s] = jnp.exp2(m_prev - m_next)
        m_ref[cur + s] = m_next


def _flash_values(par, maps, vt_tile, s_ref, p_ref, m_ref, l_ref, c_ref, acc_ref, only=None, part=None):
    nmap = len(maps)
    cur = par * nmap
    for s in (range(nmap) if only is None else (only,)):
        if part != "pv":
            m_new = m_ref[cur + s]
            for r0 in range(0, TK, CH):
                p_ref[s, r0:r0 + CH, :] = jnp.exp2(s_ref[cur + s, r0:r0 + CH, :] - m_new).astype(BF16)
        if part == "probs":
            continue
        lhs = jnp.concatenate([vt_tile(maps[s][1]), jnp.ones((BF16_ROWS, TK), BF16)], axis=0)
        pv = _dot(lhs, p_ref[s])
        dv = acc_ref.shape[1]
        l_ref[s] = c_ref[cur + s] * l_ref[s] + pv[dv:dv + 1]
        acc_ref[s] = c_ref[cur + s] * acc_ref[s] + pv[:dv]


def _flash_sweep(jd, nmap, qk, prep_far, scores_far, prep_any, scores_any, values):
    def step(j_next, par_next, prep, scores):
        prep(j_next, par_next)
        st = qk(j_next, 0)
        for s in range(nmap):
            values(j_next - 1, 1 - par_next, s, "probs")
            st_after = qk(j_next, s + 1) if s + 1 < nmap else None
            scores(j_next, par_next, s, st)
            values(j_next - 1, 1 - par_next, s, "pv")
            st = st_after

    prep_any(0, 0)
    for s in range(nmap):
        scores_any(0, 0, s, None)

    def body(k, c):
        step(2 * k + 1, 1, prep_far, scores_far)
        step(2 * k + 2, 0, prep_far, scores_far)
        return c

    npair = jnp.maximum(jd - 2, 0) // 2
    lax.fori_loop(0, npair, body, 0)
    t = 2 * npair
    left = jd - t
    for r in range(1, 4):
        @pl.when(left >= r)
        def _(r=r):
            step(t + r, r % 2, prep_any, scores_any)

    for par in range(2):
        @pl.when(lax.rem(left, 2) == par)
        def _(par=par):
            values(jd, par, None, None)


def _flash_init(m_ref, l_ref, acc_ref):
    m_ref[...] = jnp.full(m_ref.shape, NEG, F32)
    l_ref[...] = jnp.zeros_like(l_ref)
    acc_ref[...] = jnp.zeros_like(acc_ref)


def _flash_scratch(nmap, dv):
    return [
        pltpu.VMEM((nmap, TQ, LANES), BF16),
        pltpu.VMEM((2 * nmap, TK, TQ), F32),
        pltpu.VMEM((nmap, TK, TQ), BF16),
        pltpu.VMEM((2 * nmap, 1, TQ), F32),
        pltpu.VMEM((nmap, 1, TQ), F32),
        pltpu.VMEM((2 * nmap, 1, TQ), F32),
        pltpu.VMEM((nmap, dv, TQ), F32),
        pltpu.VMEM((2, TK, TQ), F32),
    ]


def _dsa_kernel(q_ref, k_ref, vt_ref, iq_ref, ik_ref, iwt_ref, bias_ref, o_ref,
                sc_ref, sg_ref, qm_ref, s_ref, p_ref, m_ref, l_ref, c_ref, acc_ref, mb_ref, *, top_k):
    i = pl.program_id(1)
    t0 = i * TQ
    jd = t0 // TK
    ntile = jd + 1
    qpos = t0 + lax.broadcasted_iota(jnp.int32, (1, TQ), 1)

    iq = iq_ref[...]
    for h in range(HI):
        qm_ref[h] = _half_masked(iq[:, (h // 2) * LANES:(h // 2 + 1) * LANES], h % 2 == 1)
    iwt = iwt_ref[...]

    def score_tile(j, carry, diag):
        rmax, rmin = carry
        ik_t = ik_ref[pl.ds(pl.multiple_of(j * TK, TK), TK), :]
        for hp in range(HI // 2):
            da = _dot_nt(ik_t, qm_ref[2 * hp])
            db = _dot_nt(ik_t, qm_ref[2 * hp + 1])
            wa = iwt[2 * hp:2 * hp + 1, :]
            wb = iwt[2 * hp + 1:2 * hp + 2, :]
            for r0 in range(0, TK, CH):
                piece = wa * jnp.maximum(da[r0:r0 + CH], 0.0) + wb * jnp.maximum(db[r0:r0 + CH], 0.0)
                if hp > 0:
                    piece = piece + sc_ref[j, r0:r0 + CH, :]
                if hp == HI // 2 - 1:
                    if diag:
                        kpos = j * TK + r0 + lax.broadcasted_iota(jnp.int32, (CH, TQ), 0)
                        valid = kpos <= qpos
                        rmax = jnp.maximum(rmax, _fold8(jnp.where(valid, piece, -jnp.inf), jnp.max))
                        rmin = jnp.minimum(rmin, _fold8(jnp.where(valid, piece, jnp.inf), jnp.min))
                        piece = jnp.where(valid, piece, -jnp.inf)
                    else:
                        rmax = jnp.maximum(rmax, _fold8(piece, jnp.max))
                        rmin = jnp.minimum(rmin, _fold8(piece, jnp.min))
                sc_ref[j, r0:r0 + CH, :] = piece
                if hp == HI // 2 - 1:
                    sg_ref[j, r0:r0 + CH, :] = _floor_to_bf16(piece)
        return rmax, rmin

    carry = lax.fori_loop(
        0, jd, lambda j, c: score_tile(j, c, False),
        (jnp.full((SUBLANES, TQ), -jnp.inf, F32), jnp.full((SUBLANES, TQ), jnp.inf, F32)))
    rmax8, rmin8 = score_tile(jd, carry, True)
    rmax = jnp.max(rmax8, axis=0, keepdims=True)
    rmin = jnp.min(rmin8, axis=0, keepdims=True)

    def row_count(pred):
        def body(j, cnt):
            for r0 in range(0, TK, CH):
                cnt = cnt + _fold8(jnp.where(pred(sc_ref[j, r0:r0 + CH, :]), 1.0, 0.0), jnp.sum)
            return cnt
        cnt = lax.fori_loop(0, ntile, body, jnp.zeros((SUBLANES, TQ), F32))
        return jnp.sum(cnt, axis=0, keepdims=True)

    def count_ge(x):
        return row_count(lambda t: t >= x)

    def count_ge_grid(xg):
        xb = xg.astype(jnp.bfloat16)
        one = jnp.bfloat16(1)
        zero = jnp.bfloat16(0)

        def body(j, cnt):
            ngrp = CH // BF16_ROWS
            parts = [jnp.zeros((BF16_ROWS, TQ), jnp.bfloat16)] * ngrp
            for r0 in range(0, TK, CH):
                ind = jnp.where(sg_ref[j, r0:r0 + CH, :] >= xb, one, zero)
                parts = [parts[g] + ind[g * BF16_ROWS:(g + 1) * BF16_ROWS] for g in range(ngrp)]
            part = (parts[0] + parts[1]) + (parts[2] + parts[3]) if ngrp == 4 else sum(parts[1:], parts[0])
            return cnt + part.astype(F32)

        cnt = lax.fori_loop(0, ntile, body, jnp.zeros((BF16_ROWS, TQ), F32))
        return jnp.sum(cnt, axis=0, keepdims=True)

    kf = float(top_k)
    nvalid = (qpos + 1).astype(F32)
    few = nvalid <= kf
    c_ge0 = count_ge(0.0)
    c_gt0 = row_count(lambda t: t > 0.0)
    at0 = jnp.logical_and(c_gt0 < kf, c_ge0 >= kf)
    pos = c_gt0 >= kf
    lo0 = jnp.where(few, -3.0e38, jnp.where(pos, 0.0, jnp.where(at0, 0.0, rmin)))
    hi0 = jnp.where(pos, rmax * 1.000001 + 1e-37, 0.0)
    clo0 = jnp.where(few, kf, jnp.where(jnp.logical_or(pos, at0), c_ge0, nvalid))
    chi0 = jnp.where(pos, 0.0, c_ge0)

    def settled(clo, chi):
        return jnp.logical_or(clo == kf, kf - chi == 1.0)

    done0 = jnp.where(jnp.logical_or(jnp.logical_or(few, at0), settled(clo0, chi0)), 1.0, 0.0)

    def search_cond(st):
        it, _, _, _, _, done = st
        return jnp.logical_and(it < SEARCH_CAP, jnp.min(done) < 0.5)

    def search_body(st):
        it, lo, hi, clo, chi, done = st
        mid = lo + (hi - lo) * 0.5
        stuck = jnp.logical_or(mid <= lo, mid >= hi)
        cx = count_ge(mid)
        ge = cx >= kf
        act = done < 0.5
        up = jnp.logical_and(act, ge)
        dn = jnp.logical_and(act, jnp.logical_not(ge))
        lo = jnp.where(up, mid, lo)
        clo = jnp.where(up, cx, clo)
        hi = jnp.where(dn, mid, hi)
        chi = jnp.where(dn, cx, chi)
        done = jnp.where(jnp.logical_or(stuck, settled(clo, chi)), 1.0, done)
        return it + 1, lo, hi, clo, chi, done

    def grid_cond(st):
        it, _, _, _, _, done, off = st
        return jnp.logical_and(it < GRID_CAP, jnp.min(jnp.maximum(done, off)) < 0.5)

    def grid_body(st):
        it, lo, hi, clo, chi, done, off = st
        xg = (lo + (hi - lo) * 0.5).astype(jnp.bfloat16).astype(F32)
        usable = jnp.logical_and(xg > lo, xg < hi)
        cx = count_ge_grid(xg)
        ge = cx >= kf
        act = jnp.logical_and(jnp.logical_and(done < 0.5, off < 0.5), usable)
        up = jnp.logical_and(act, ge)
        dn = jnp.logical_and(act, jnp.logical_not(ge))
        lo = jnp.where(up, xg, lo)
        clo = jnp.where(up, cx, clo)
        hi = jnp.where(dn, xg, hi)
        chi = jnp.where(dn, cx, chi)
        done = jnp.where(settled(clo, chi), 1.0, done)
        off = jnp.where(usable, off, 1.0)
        return it + 1, lo, hi, clo, chi, done, off

    _, lo1, hi1, clo1, chi1, done1, _ = lax.while_loop(
        grid_cond, grid_body, (jnp.int32(0), lo0, hi0, clo0, chi0, done0, jnp.zeros((1, TQ), F32)))

    _, lo, hi, clo, chi, _ = lax.while_loop(
        search_cond, search_body, (jnp.int32(0), lo1, hi1, clo1, chi1, done1))

    def below_hi_max(j, mx):
        for r0 in range(0, TK, CH):
            t = sc_ref[j, r0:r0 + CH, :]
            mx = jnp.maximum(mx, _fold8(jnp.where(t < hi, t, -jnp.inf), jnp.max))
        return mx

    top_below = jnp.max(lax.fori_loop(0, ntile, below_hi_max, jnp.full((SUBLANES, TQ), -jnp.inf, F32)),
                        axis=0, keepdims=True)
    last_one = jnp.logical_and(jnp.logical_and(jnp.logical_not(few), clo != kf), kf - chi == 1.0)
    theta = jnp.where(last_one, top_below, lo)
    c_theta = count_ge(theta)

    surplus = jnp.where(jnp.logical_and(jnp.logical_not(few), c_theta > kf), c_theta - kf, 0.0)

    @pl.when(jnp.max(surplus) > 0.5)
    def _():
        ra = lax.broadcasted_iota(jnp.int32, (TK, TK), 0)
        ca = lax.broadcasted_iota(jnp.int32, (TK, TK), 1)
        triu = jnp.where(ca >= ra, 1.0, 0.0).astype(BF16)

        def body(jj, later):
            j = ntile - 1 - jj
            cnt = jnp.zeros((SUBLANES, TQ), F32)
            for r0 in range(0, TK, CH):
                eqf = jnp.where(sc_ref[j, r0:r0 + CH, :] == theta, 1.0, 0.0)
                p_ref[0, r0:r0 + CH, :] = eqf.astype(BF16)
                cnt = cnt + _fold8(eqf, jnp.sum)
            from_end = _dot(triu, p_ref[0])
            for r0 in range(0, TK, CH):
                t = sc_ref[j, r0:r0 + CH, :]
                cut = jnp.where(from_end[r0:r0 + CH] + later <= surplus, -jnp.inf, t)
                sc_ref[j, r0:r0 + CH, :] = jnp.where(t == theta, cut, t)
            return later + jnp.sum(cnt, axis=0, keepdims=True)

        lax.fori_loop(0, ntile, body, jnp.zeros((1, TQ), F32))

    _flash_init(m_ref, l_ref, acc_ref)
    q = q_ref[...]
    for h in range(HA):
        qm_ref[h] = _half_masked(q[:, (h // 2) * LANES:(h // 2 + 1) * LANES], h % 2 == 1)
    maps = [(h // 2, h, h) for h in range(HA)]

    def prep(j, par):
        for r0 in range(0, TK, CH):
            mb_ref[par, r0:r0 + CH, :] = jnp.where(sc_ref[j, r0:r0 + CH, :] >= theta, 0.0, NEG)

    def k_tile(j):
        row = pl.ds(pl.multiple_of(j * TK, TK), TK)
        return lambda p: k_ref[row, p * LANES:(p + 1) * LANES]

    def qk(j, s):
        return _dot_nt(k_tile(j)(maps[s][0]), qm_ref[s])

    def scores(j, par, near, only, st):
        _flash_scores(
            par, maps, k_tile(j),
            lambda r0: mb_ref[par, r0:r0 + CH, :],
            (lambda h, cc, r: _near_bias_t(bias_ref, h, i, j, cc, r)) if near else None,
            qm_ref, s_ref, m_ref, c_ref, only, st)

    def values(j, par, only, part):
        _flash_values(par, maps, lambda h: vt_ref[j, h * DA:(h + 1) * DA, :],
                      s_ref, p_ref, m_ref, l_ref, c_ref, acc_ref, only, part)

    _flash_sweep(jd, HA, qk, prep, lambda j, par, s, st: scores(j, par, False, s, st),
                 prep, lambda j, par, s, st: scores(j, par, True, s, st), values)

    for p in range(HA // 2):
        o_lo = acc_ref[2 * p] / l_ref[2 * p]
        o_hi = acc_ref[2 * p + 1] / l_ref[2 * p + 1]
        o_t = jnp.concatenate([o_lo, o_hi], axis=0)
        o_ref[:, p * LANES:(p + 1) * LANES] = o_t.T.astype(BF16)


def _dsa(q, k, vt, iq, ik2, iwt, bias, batch, seq):
    top_k = min(TOPK_MAX, seq // 4)
    nq = seq // TQ
    nt = seq // TK
    hd = HA * DA
    qrow = lambda b, i: (b * nq + i, 0)
    whole = lambda b, i: (b, 0)
    once = pl.Buffered(1)
    return pl.pallas_call(
        functools.partial(_dsa_kernel, top_k=top_k),
        grid=(batch, nq),
        in_specs=[
            pl.BlockSpec((TQ, hd), qrow),
            pl.BlockSpec((seq, hd), whole, pipeline_mode=once),
            pl.BlockSpec((nt, hd, TK), lambda b, i: (b, 0, 0), pipeline_mode=once),
            pl.BlockSpec((TQ, HI * DI), qrow),
            pl.BlockSpec((seq, LANES), whole, pipeline_mode=once),
            pl.BlockSpec((HI, TQ), lambda b, i: (0, b * nq + i)),
            pl.BlockSpec((HA, 2, LANES, LANES), lambda b, i: (0, 0, 0, 0)),
        ],
        out_specs=pl.BlockSpec((TQ, hd), qrow),
        out_shape=jax.ShapeDtypeStruct((batch * seq, hd), BF16),
        scratch_shapes=[pltpu.VMEM((nt, TK, TQ), F32), pltpu.VMEM((nt, TK, TQ), jnp.bfloat16)]
        + _flash_scratch(HA, DA),
        compiler_params=_params("parallel", "arbitrary"),
        name="dsa",
    )(q, k, vt, iq, ik2, iwt, bias)


def _diff_kernel(q_ref, k_ref, vt_ref, lam_ref, g_ref, bias_ref, o_ref,
                 qm_ref, s_ref, p_ref, m_ref, l_ref, c_ref, acc_ref, mb_ref, *, lam_init):
    i = pl.program_id(1)
    t0 = i * TQ
    jd = t0 // TK
    _flash_init(m_ref, l_ref, acc_ref)
    q = q_ref[...]
    nmap = 2 * HB
    for s in range(nmap):
        qm_ref[s] = _half_masked(q[:, (s // 2) * LANES:(s // 2 + 1) * LANES], s % 2 == 1)
    maps = [(s // 2, s // 2, s // 2) for s in range(nmap)]
    qpos = t0 + lax.broadcasted_iota(jnp.int32, (1, TQ), 1)

    def prep(j, par):
        for r0 in range(0, TK, CH):
            kpos = j * TK + r0 + lax.broadcasted_iota(jnp.int32, (CH, TQ), 0)
            mb_ref[par, r0:r0 + CH, :] = jnp.where(kpos <= qpos, 0.0, NEG)

    def k_tile(j):
        row = pl.ds(pl.multiple_of(j * TK, TK), TK)
        return lambda p: k_ref[row, p * LANES:(p + 1) * LANES]

    def qk(j, s):
        return _dot_nt(k_tile(j)(maps[s][0]), qm_ref[s])

    def scores(j, par, general, only, st):
        _flash_scores(
            par, maps, k_tile(j),
            (lambda r0: mb_ref[par, r0:r0 + CH, :]) if general else None,
            (lambda h, cc, r: _near_bias_t(bias_ref, h, i, j, cc, r)) if general else None,
            qm_ref, s_ref, m_ref, c_ref, only, st)

    def values(j, par, only, part):
        _flash_values(par, maps, lambda p: vt_ref[j, p * LANES:(p + 1) * LANES, :],
                      s_ref, p_ref, m_ref, l_ref, c_ref, acc_ref, only, part)

    _flash_sweep(jd, nmap, qk, lambda j, par: None, lambda j, par, s, st: scores(j, par, False, s, st),
                 prep, lambda j, par, s, st: scores(j, par, True, s, st), values)

    lam = lam_ref[...]
    lam_full = (jnp.exp(jnp.sum(lam[0:1] * lam[1:2], axis=1, keepdims=True))
                - jnp.exp(jnp.sum(lam[2:3] * lam[3:4], axis=1, keepdims=True)) + lam_init)
    for h in range(HB):
        o0 = acc_ref[2 * h] / l_ref[2 * h]
        o1 = acc_ref[2 * h + 1] / l_ref[2 * h + 1]
        o = (o0 - lam_full * o1).T
        y = o * lax.rsqrt(jnp.mean(o * o, axis=-1, keepdims=True) + LN_EPS) * g_ref[...]
        o_ref[:, h * LANES:(h + 1) * LANES] = (y * (1.0 - lam_init)).astype(BF16)


def _diff(q, k, vt, lam, g, bias, batch, seq, lam_init):
    nq = seq // TQ
    nt = seq // TK
    hb = HB * 2 * DB
    qrow = lambda b, i: (b * nq + i, 0)
    whole = lambda b, i: (b, 0)
    once = pl.Buffered(1)
    return pl.pallas_call(
        functools.partial(_diff_kernel, lam_init=lam_init),
        grid=(batch, nq),
        in_specs=[
            pl.BlockSpec((TQ, hb), qrow),
            pl.BlockSpec((seq, hb), whole, pipeline_mode=once),
            pl.BlockSpec((nt, hb, TK), lambda b, i: (b, 0, 0), pipeline_mode=once),
            pl.BlockSpec((4, DB), lambda b, i: (0, 0)),
            pl.BlockSpec((1, 2 * DB), lambda b, i: (0, 0)),
            pl.BlockSpec((HB, 2, LANES, LANES), lambda b, i: (0, 0, 0, 0)),
        ],
        out_specs=pl.BlockSpec((TQ, hb), qrow),
        out_shape=jax.ShapeDtypeStruct((batch * seq, hb), BF16),
        scratch_shapes=_flash_scratch(2 * HB, 2 * DB),
        compiler_params=_params("parallel", "arbitrary"),
        name="diff",
    )(q, k, vt, lam, g, bias)


def _mix_out_kernel(oa_ref, ob_ref, wa_ref, wb_ref, x_ref, g_ref, b_ref, o_ref):
    m = _dot(oa_ref[...], wa_ref[...]) + _dot(ob_ref[...], wb_ref[...])
    o_ref[...] = _layer_norm(ALPHA * x_ref[...] + m, g_ref[...], b_ref[...])


def _mix_out(oa, ob, wa, wb, x, g, b):
    t, d = x.shape
    row = lambda i: (i, 0)
    const = lambda i: (0, 0)
    return pl.pallas_call(
        _mix_out_kernel,
        grid=(t // TM_PROJ,),
        in_specs=[
            pl.BlockSpec((TM_PROJ, oa.shape[1]), row),
            pl.BlockSpec((TM_PROJ, ob.shape[1]), row),
            pl.BlockSpec(wa.shape, const),
            pl.BlockSpec(wb.shape, const),
            pl.BlockSpec((TM_PROJ, d), row),
            pl.BlockSpec((1, d), const),
            pl.BlockSpec((1, d), const),
        ],
        out_specs=pl.BlockSpec((TM_PROJ, d), row),
        out_shape=jax.ShapeDtypeStruct((t, d), F32),
        compiler_params=_params("parallel"),
        name="mix_out",
    )(oa, ob, wa, wb, x, g, b)


def _conv_in_kernel(x_ref, w_ref, o_ref):
    xb = x_ref[...].astype(BF16)
    d = o_ref.shape[1]
    a = _dot(xb, w_ref[:, 0:d])
    gate = _dot(xb, w_ref[:, d:2 * d])
    o_ref[...] = a * jax.nn.sigmoid(gate)


def _conv_in(x, w):
    t, d = x.shape
    return pl.pallas_call(
        _conv_in_kernel,
        grid=(t // TM_CONV,),
        in_specs=[pl.BlockSpec((TM_CONV, d), lambda i: (i, 0)),
                  pl.BlockSpec((d, 2 * d), lambda i: (0, 0))],
        out_specs=pl.BlockSpec((TM_CONV, d), lambda i: (i, 0)),
        out_shape=jax.ShapeDtypeStruct((t, d), F32),
        compiler_params=_params("parallel"),
        name="conv_in",
    )(x, w)


def _conv_out_kernel(h_ref, halo_ref, dw_ref, dwb_ref, cg_ref, cb_ref, w_ref, x_ref, g_ref, b_ref,
                     o_ref, buf_ref, cv_ref):
    i = pl.program_id(1)
    buf_ref[0:HALO, :] = jnp.where(i > 0, halo_ref[...], 0.0)
    buf_ref[HALO:, :] = h_ref[...]
    off = HALO - (CONV_W - 1)

    def chunk(c, carry):
        r0 = pl.multiple_of(c * CONV_ROWS, CONV_ROWS)
        nwin = CONV_ROWS + HALO
        for lt in range(dwb_ref.shape[1] // LANES):
            cols = slice(lt * LANES, (lt + 1) * LANES)
            win = buf_ref[pl.ds(r0, nwin), cols]
            acc = jnp.broadcast_to(dwb_ref[:, cols], (CONV_ROWS, LANES))
            for sub in range(SUBLANES):
                shifted = pltpu.roll(win, nwin - sub, axis=0) if sub else win
                for k in range(CONV_W):
                    if (off + k) % SUBLANES == sub:
                        a0 = off + k - sub
                        acc = acc + dw_ref[k:k + 1, cols] * shifted[a0:a0 + CONV_ROWS]
            cv_ref[pl.ds(r0, CONV_ROWS), cols] = acc
        return carry

    lax.fori_loop(0, TM_CONV // CONV_ROWS, chunk, 0)
    y = _layer_norm(cv_ref[...], cg_ref[...], cb_ref[...])
    u = (y * jax.nn.sigmoid(y)).astype(BF16)
    m = _dot(u, w_ref[...])
    o_ref[...] = _layer_norm(ALPHA * x_ref[...] + m, g_ref[...], b_ref[...])


def _conv_out(h, dw, dwb, cg, cb, w, x, g, b, batch, seq):
    t, d = x.shape
    nt = seq // TM_CONV
    per = TM_CONV // HALO
    row = lambda bb, i: (bb * nt + i, 0)
    halo = lambda bb, i: (jnp.maximum((bb * nt + i) * per - 1, 0), 0)
    const = lambda bb, i: (0, 0)
    return pl.pallas_call(
        _conv_out_kernel,
        grid=(batch, nt),
        in_specs=[
            pl.BlockSpec((TM_CONV, d), row),
            pl.BlockSpec((HALO, d), halo),
            pl.BlockSpec(dw.shape, const),
            pl.BlockSpec((1, d), const),
            pl.BlockSpec((1, d), const),
            pl.BlockSpec((1, d), const),
            pl.BlockSpec((d, d), const),
            pl.BlockSpec((TM_CONV, d), row),
            pl.BlockSpec((1, d), const),
            pl.BlockSpec((1, d), const),
        ],
        out_specs=pl.BlockSpec((TM_CONV, d), row),
        out_shape=jax.ShapeDtypeStruct((t, d), F32),
        scratch_shapes=[pltpu.VMEM((TM_CONV + HALO, d), F32), pltpu.VMEM((TM_CONV, d), F32)],
        compiler_params=_params("parallel", "arbitrary"),
        name="conv_out",
    )(h, h, dw, dwb, cg, cb, w, x, g, b)


def _attention_mixer(x, w_in, ikg, ikb, lam, subg, w_out, bias, g, b, batch, seq, layer_idx):
    c = [0]
    for n in (HA * DA, HA * DA, HA * DA, HI * DI, DI, HI, HB * 2 * DB, HB * 2 * DB, HB * 2 * DB):
        c.append(c[-1] + n)
    w_ik = w_in[:, c[4]:c[5]]
    w_iw = jnp.pad(w_in[:, c[5]:c[6]], ((0, 0), (0, LANES - HI)))
    w = jnp.concatenate([w_in[:, :c[4]], w_ik, w_ik, w_iw, w_in[:, c[6]:]], axis=1).astype(BF16)
    ikg2 = jnp.concatenate([ikg, ikg])[None]
    ikb2 = jnp.concatenate([ikb, ikb])[None]
    qa, ka, vat, iq, ik2, iwt, qb, kb, vbt = _mix_proj(x, w, ikg2, ikb2)
    o_a = _dsa(qa, ka, vat, iq, ik2, iwt, bias[:HA], batch, seq)
    lam_init = 0.8 - 0.6 * math.exp(-0.3 * layer_idx)
    o_b = _diff(qb, kb, vbt, lam, subg[None], bias[HA:], batch, seq, lam_init)
    wo = w_out.astype(BF16)
    return _mix_out(o_a, o_b, wo[:HA * DA], wo[HA * DA:], x, g, b)


def _conv_module(x, w_in, dw, dwb, cg, cb, w_out, g, b, batch, seq):
    h = _conv_in(x, w_in.astype(BF16))
    dwp = jnp.pad(dw, ((0, HALO - CONV_W), (0, 0)))
    return _conv_out(h, dwp, dwb[None], cg[None], cb[None], w_out.astype(BF16), x, g, b, batch, seq)


def kernel(x, ffn_in, ffn_out, ln_g, ln_b, rel_bias, mix_w_in, idx_k_g, idx_k_b, diff_lambda,
           diff_subln_g, mix_w_out, conv_w_in, conv_dw, conv_dw_b, conv_ln_g, conv_ln_b, conv_w_out):
    batch, seq, d = x.shape
    x = x.reshape(batch * seq, d)
    bias = _bias_tiles(rel_bias.T)
    ffn_in_b = ffn_in.astype(BF16)
    ffn_out_b = ffn_out.astype(BF16)
    g_all = ln_g.reshape(3 * DEPTH, 1, d)
    b_all = ln_b.reshape(3 * DEPTH, 1, d)
    for l in range(DEPTH):
        j = l // 2
        x = _ffn(x, ffn_in_b, ffn_out_b, g_all, b_all, l, 0)
        if l % 2 == 0:
            x = _attention_mixer(x, mix_w_in[j], idx_k_g[j], idx_k_b[j], diff_lambda[j], diff_subln_g[j],
                                 mix_w_out[j], bias, ln_g[l, 1][None], ln_b[l, 1][None], batch, seq, l)
        else:
            x = _conv_module(x, conv_w_in[j], conv_dw[j], conv_dw_b[j], conv_ln_g[j], conv_ln_b[j],
                             conv_w_out[j], ln_g[l, 1][None], ln_b[l, 1][None], batch, seq)
        x = _ffn(x, ffn_in_b, ffn_out_b, g_all, b_all, l, 1)
    return x.reshape(batch, seq, d)
```

```python
import functools
import math

import jax
import jax.numpy as jnp
from jax import lax
from jax.experimental import pallas as pl
from jax.experimental.pallas import tpu as pltpu

F32 = jnp.float32
BF16 = jnp.bfloat16

D_MODEL = 1024
DEPTH = 4
HA, DA = 8, 64
HI, DI = 8, 64
TOPK_MAX = 256
HB, DB = 4, 64
CONV_W = 31
D_FF = 2816
NUM_BUCKETS = 32
MAX_DISTANCE = 128
ALPHA = (2 * DEPTH) ** 0.25
LN_EPS = 1e-5

LANES = 128
SUBLANES = 8
BF16_ROWS = 16
VMEM_LIMIT = 56 * 1024 * 1024

TQ = 256
TK = 512
QB = TQ // LANES
KB = TK // LANES
CH = 64
TM_FFN = 512
TF_FFN = 256
TM_PROJ = TK
TM_CONV = 512
HALO = 32
CONV_ROWS = 128
NEG = -1e30
LOG2E = math.log2(math.e)
SEARCH_CAP = 1200


def _params(*sem):
    return pltpu.CompilerParams(dimension_semantics=sem, vmem_limit_bytes=VMEM_LIMIT)


def _layer_norm(v, g, b):
    mu = jnp.mean(v, axis=-1, keepdims=True)
    c = v - mu
    var = jnp.mean(c * c, axis=-1, keepdims=True)
    return c * lax.rsqrt(var + LN_EPS) * g + b


def _dot(a, b):
    return jnp.dot(a, b, preferred_element_type=F32)


def _dot_nt(a, b):
    return lax.dot_general(a, b, (((1,), (1,)), ((), ())), preferred_element_type=F32)


def _ffn_kernel(x_ref, wi_ref, wo_ref, g_ref, b_ref, o_ref, xb_ref, acc_ref):
    f = wo_ref.shape[0]
    xb_ref[...] = x_ref[...].astype(BF16)
    for c in range(f // TF_FFN):
        xb = xb_ref[...]
        a = _dot(xb, wi_ref[:, c * TF_FFN:(c + 1) * TF_FFN])
        u = _dot(xb, wi_ref[:, f + c * TF_FFN:f + (c + 1) * TF_FFN])
        h = (a * jax.nn.sigmoid(a) * u).astype(BF16)
        part = _dot(h, wo_ref[c * TF_FFN:(c + 1) * TF_FFN, :])
        if c == 0:
            acc_ref[...] = part
        else:
            acc_ref[...] += part
    y = ALPHA * x_ref[...] + 0.5 * acc_ref[...]
    o_ref[...] = _layer_norm(y, g_ref[...], b_ref[...])


def _ffn_specs(d, f, layer, half):
    once = pl.Buffered(1)
    ln = 3 * layer + 2 * half
    specs = [
        pl.BlockSpec((None, None, d, 2 * f), lambda *_: (layer, half, 0, 0), pipeline_mode=once),
        pl.BlockSpec((None, None, f, d), lambda *_: (layer, half, 0, 0), pipeline_mode=once),
        pl.BlockSpec((None, 1, d), lambda *_: (ln, 0, 0)),
        pl.BlockSpec((None, 1, d), lambda *_: (ln, 0, 0)),
    ]
    scratch = [pltpu.VMEM((TM_FFN, d), BF16), pltpu.VMEM((TM_FFN, d), F32)]
    return specs, scratch


def _ffn(x, w_in, w_out, g, b, layer, half):
    t, d = x.shape
    specs, scratch = _ffn_specs(d, w_out.shape[2], layer, half)
    return pl.pallas_call(
        _ffn_kernel,
        grid=(t // TM_FFN,),
        in_specs=[pl.BlockSpec((TM_FFN, d), lambda i: (i, 0))] + specs,
        out_specs=pl.BlockSpec((TM_FFN, d), lambda i: (i, 0)),
        out_shape=jax.ShapeDtypeStruct((t, d), F32),
        scratch_shapes=scratch,
        compiler_params=_params("parallel"),
        name="ffn",
    )(x, w_in, w_out, g, b)


_W3 = 3 * HA * DA
_C_IQ = _W3
_C_IK = _C_IQ + HI * DI
_C_IW = _C_IK + LANES
_C_B = _C_IW + LANES
_C_END = _C_B + 3 * HB * 2 * DB


def _mix_proj_kernel(x_ref, w_ref, ikg_ref, ikb_ref,
                     qa_ref, ka_ref, vat_ref, iq_ref, ik_ref, iwt_ref, qb_ref, kb_ref, vbt_ref):
    xb = x_ref[...].astype(BF16)
    hd = HA * DA
    qa_ref[...] = (_dot(xb, w_ref[:, 0:hd]) * (DA ** -0.5 * LOG2E)).astype(BF16)
    ka_ref[...] = _dot(xb, w_ref[:, hd:2 * hd]).astype(BF16)
    vat_ref[0] = _dot(xb, w_ref[:, 2 * hd:3 * hd]).T.astype(BF16)
    iq_ref[...] = (_dot(xb, w_ref[:, _C_IQ:_C_IK]) * (DI ** -0.5)).astype(BF16)
    ik2 = _dot(xb, w_ref[:, _C_IK:_C_IW])
    ik_ref[...] = _layer_norm(ik2, ikg_ref[...], ikb_ref[...]).astype(BF16)
    iw = _dot(xb, w_ref[:, _C_IW:_C_B]) * (HI ** -0.5)
    iwt_ref[...] = iw.T[0:HI, :]
    hb = HB * 2 * DB
    qb_ref[...] = (_dot(xb, w_ref[:, _C_B:_C_B + hb]) * (DB ** -0.5 * LOG2E)).astype(BF16)
    kb_ref[...] = _dot(xb, w_ref[:, _C_B + hb:_C_B + 2 * hb]).astype(BF16)
    vbt_ref[0] = _dot(xb, w_ref[:, _C_B + 2 * hb:_C_END]).T.astype(BF16)


def _mix_proj(x, w, ikg2, ikb2):
    t, d = x.shape
    nt = t // TM_PROJ
    row = lambda i: (i, 0)
    const = lambda i: (0, 0)
    tile = lambda i: (i, 0, 0)
    wide = lambda n, dt: jax.ShapeDtypeStruct((t, n), dt)
    hd, hb = HA * DA, HB * 2 * DB
    return pl.pallas_call(
        _mix_proj_kernel,
        grid=(nt,),
        in_specs=[
            pl.BlockSpec((TM_PROJ, d), row),
            pl.BlockSpec((d, _C_END), const),
            pl.BlockSpec((1, LANES), const),
            pl.BlockSpec((1, LANES), const),
        ],
        out_specs=[
            pl.BlockSpec((TM_PROJ, hd), row), pl.BlockSpec((TM_PROJ, hd), row),
            pl.BlockSpec((1, hd, TM_PROJ), tile), pl.BlockSpec((TM_PROJ, HI * DI), row),
            pl.BlockSpec((TM_PROJ, LANES), row), pl.BlockSpec((HI, TM_PROJ), lambda i: (0, i)),
            pl.BlockSpec((TM_PROJ, hb), row), pl.BlockSpec((TM_PROJ, hb), row),
            pl.BlockSpec((1, hb, TM_PROJ), tile),
        ],
        out_shape=[wide(hd, BF16), wide(hd, BF16), jax.ShapeDtypeStruct((nt, hd, TM_PROJ), BF16),
                   wide(HI * DI, BF16), wide(LANES, BF16), jax.ShapeDtypeStruct((HI, t), F32),
                   wide(hb, BF16), wide(hb, BF16), jax.ShapeDtypeStruct((nt, hb, TM_PROJ), BF16)],
        compiler_params=_params("parallel"),
        name="mix_proj",
    )(x, w, ikg2, ikb2)


def _bias_kernel(tbl_ref, o_ref):
    nh = o_ref.shape[0]
    kk = lax.broadcasted_iota(jnp.int32, (LANES, LANES), 0)
    qq = lax.broadcasted_iota(jnp.int32, (LANES, LANES), 1)
    max_exact = NUM_BUCKETS // 2
    for blk in range(2):
        n = jnp.maximum(qq - kk + blk * LANES, 0)
        nf = jnp.maximum(n, 1).astype(F32)
        large = max_exact + (jnp.log(nf / max_exact) / math.log(MAX_DISTANCE / max_exact)
                             * (NUM_BUCKETS - max_exact)).astype(jnp.int32)
        large = jnp.minimum(large, NUM_BUCKETS - 1)
        bucket = jnp.where(n < max_exact, n, large)
        for h in range(nh):
            acc = jnp.zeros((LANES, LANES), F32)
            for k in range(NUM_BUCKETS):
                acc = jnp.where(bucket == k, tbl_ref[h, k], acc)
            o_ref[h, blk] = (acc - tbl_ref[h, NUM_BUCKETS - 1]) * LOG2E


def _bias_tiles(rel_bias_t):
    nh = rel_bias_t.shape[0]
    return pl.pallas_call(
        _bias_kernel,
        in_specs=[pl.BlockSpec(memory_space=pltpu.SMEM)],
        out_shape=jax.ShapeDtypeStruct((nh, 2, LANES, LANES), F32),
        name="bias_tiles",
    )(rel_bias_t)


def _half_masked(x, upper):
    lane = lax.broadcasted_iota(jnp.int32, x.shape, 1)
    keep = (lane >= LANES // 2) if upper else (lane < LANES // 2)
    return jnp.where(keep, x, jnp.zeros_like(x))


def _near_bias_t(bias_ref, head, i, j, cc, r0):
    b0 = bias_ref[head, 0, r0:r0 + CH, :]
    b1 = bias_ref[head, 1, r0:r0 + CH, :]
    cols = []
    for qq in range(QB):
        delta = (i * QB + qq) - (j * KB + cc)
        cols.append(jnp.where(delta == 0, b0, jnp.where(delta == 1, b1, jnp.zeros_like(b0))))
    return jnp.concatenate(cols, axis=1)


def _fold8(x, op):
    return op(x.reshape(CH // SUBLANES, SUBLANES, x.shape[1]), axis=0)


def _flash_scores(par, maps, k_tile, mask_bias, near_bias, qm_ref, s_ref, m_ref, c_ref, only=None, st=None):
    nmap = len(maps)
    cur = par * nmap
    prv = nmap - cur
    nch = LANES // CH
    for s in (range(nmap) if only is None else (only,)):
        pair, _, head = maps[s]
        if st is None or only is None:
            st = _dot_nt(k_tile(pair), qm_ref[s])
        mx = jnp.full((SUBLANES, TQ), NEG, F32)
        for cc in range(KB):
            for hh in range(nch):
                r0 = cc * LANES + hh * CH
                piece = st[r0:r0 + CH]
                if mask_bias is not None:
                    piece = piece + mask_bias(r0)
                if near_bias is not None:
                    piece = piece + near_bias(head, cc, hh * CH)
                s_ref[cur + s, r0:r0 + CH, :] = piece
                mx = jnp.maximum(mx, _fold8(piece, jnp.max))
        m_prev = m_ref[prv + s]
        m_next = jnp.maximum(m_prev, jnp.max(mx, axis=0, keepdims=True))
        c_ref[cur + s] = jnp.exp2(m_prev - m_next)
        m_ref[cur + s] = m_next


def _flash_values(par, maps, vt_tile, s_ref, p_ref, m_ref, l_ref, c_ref, acc_ref, only=None, part=None):
    nmap = len(maps)
    cur = par * nmap
    for s in (range(nmap) if only is None else (only,)):
        if part != "pv":
            m_new = m_ref[cur + s]
            for r0 in range(0, TK, CH):
                p_ref[s, r0:r0 + CH, :] = jnp.exp2(s_ref[cur + s, r0:r0 + CH, :] - m_new).astype(BF16)
        if part == "probs":
            continue
        lhs = jnp.concatenate([vt_tile(maps[s][1]), jnp.ones((BF16_ROWS, TK), BF16)], axis=0)
        pv = _dot(lhs, p_ref[s])
        dv = acc_ref.shape[1]
        l_ref[s] = c_ref[cur + s] * l_ref[s] + pv[dv:dv + 1]
        acc_ref[s] = c_ref[cur + s] * acc_ref[s] + pv[:dv]


def _flash_sweep(jd, nmap, qk, prep_far, scores_far, prep_any, scores_any, values):
    def step(j_next, par_next, prep, scores):
        prep(j_next, par_next)
        st = qk(j_next, 0)
        for s in range(nmap):
            values(j_next - 1, 1 - par_next, s, "probs")
            st_after = qk(j_next, s + 1) if s + 1 < nmap else None
            scores(j_next, par_next, s, st)
            values(j_next - 1, 1 - par_next, s, "pv")
            st = st_after

    prep_any(0, 0)
    for s in range(nmap):
        scores_any(0, 0, s, None)

    def body(k, c):
        step(2 * k + 1, 1, prep_far, scores_far)
        step(2 * k + 2, 0, prep_far, scores_far)
        return c

    npair = jnp.maximum(jd - 2, 0) // 2
    lax.fori_loop(0, npair, body, 0)
    t = 2 * npair
    left = jd - t
    for r in range(1, 4):
        @pl.when(left >= r)
        def _(r=r):
            step(t + r, r % 2, prep_any, scores_any)

    for par in range(2):
        @pl.when(lax.rem(left, 2) == par)
        def _(par=par):
            values(jd, par, None, None)


def _flash_init(m_ref, l_ref, acc_ref):
    m_ref[...] = jnp.full(m_ref.shape, NEG, F32)
    l_ref[...] = jnp.zeros_like(l_ref)
    acc_ref[...] = jnp.zeros_like(acc_ref)


def _flash_scratch(nmap, dv):
    return [
        pltpu.VMEM((nmap, TQ, LANES), BF16),
        pltpu.VMEM((2 * nmap, TK, TQ), F32),
        pltpu.VMEM((nmap, TK, TQ), BF16),
        pltpu.VMEM((2 * nmap, 1, TQ), F32),
        pltpu.VMEM((nmap, 1, TQ), F32),
        pltpu.VMEM((2 * nmap, 1, TQ), F32),
        pltpu.VMEM((nmap, dv, TQ), F32),
        pltpu.VMEM((2, TK, TQ), F32),
    ]


def _dsa_kernel(q_ref, k_ref, vt_ref, iq_ref, ik_ref, iwt_ref, bias_ref, o_ref,
                sc_ref, qm_ref, s_ref, p_ref, m_ref, l_ref, c_ref, acc_ref, mb_ref, *, top_k):
    i = pl.program_id(1)
    t0 = i * TQ
    jd = t0 // TK
    ntile = jd + 1
    qpos = t0 + lax.broadcasted_iota(jnp.int32, (1, TQ), 1)

    iq = iq_ref[...]
    for h in range(HI):
        qm_ref[h] = _half_masked(iq[:, (h // 2) * LANES:(h // 2 + 1) * LANES], h % 2 == 1)
    iwt = iwt_ref[...]

    def score_tile(j, carry, diag):
        rmax, rmin = carry
        ik_t = ik_ref[pl.ds(pl.multiple_of(j * TK, TK), TK), :]
        for hp in range(HI // 2):
            da = _dot_nt(ik_t, qm_ref[2 * hp])
            db = _dot_nt(ik_t, qm_ref[2 * hp + 1])
            wa = iwt[2 * hp:2 * hp + 1, :]
            wb = iwt[2 * hp + 1:2 * hp + 2, :]
            for r0 in range(0, TK, CH):
                piece = wa * jnp.maximum(da[r0:r0 + CH], 0.0) + wb * jnp.maximum(db[r0:r0 + CH], 0.0)
                if hp > 0:
                    piece = piece + sc_ref[j, r0:r0 + CH, :]
                if hp == HI // 2 - 1:
                    if diag:
                        kpos = j * TK + r0 + lax.broadcasted_iota(jnp.int32, (CH, TQ), 0)
                        valid = kpos <= qpos
                        rmax = jnp.maximum(rmax, _fold8(jnp.where(valid, piece, -jnp.inf), jnp.max))
                        rmin = jnp.minimum(rmin, _fold8(jnp.where(valid, piece, jnp.inf), jnp.min))
                        piece = jnp.where(valid, piece, -jnp.inf)
                    else:
                        rmax = jnp.maximum(rmax, _fold8(piece, jnp.max))
                        rmin = jnp.minimum(rmin, _fold8(piece, jnp.min))
                sc_ref[j, r0:r0 + CH, :] = piece
        return rmax, rmin

    carry = lax.fori_loop(
        0, jd, lambda j, c: score_tile(j, c, False),
        (jnp.full((SUBLANES, TQ), -jnp.inf, F32), jnp.full((SUBLANES, TQ), jnp.inf, F32)))
    rmax8, rmin8 = score_tile(jd, carry, True)
    rmax = jnp.max(rmax8, axis=0, keepdims=True)
    rmin = jnp.min(rmin8, axis=0, keepdims=True)

    def row_count(pred):
        def body(j, cnt):
            for r0 in range(0, TK, CH):
                cnt = cnt + _fold8(jnp.where(pred(sc_ref[j, r0:r0 + CH, :]), 1.0, 0.0), jnp.sum)
            return cnt
        cnt = lax.fori_loop(0, ntile, body, jnp.zeros((SUBLANES, TQ), F32))
        return jnp.sum(cnt, axis=0, keepdims=True)

    def count_ge(x):
        return row_count(lambda t: t >= x)

    kf = float(top_k)
    nvalid = (qpos + 1).astype(F32)
    few = nvalid <= kf
    c_ge0 = count_ge(0.0)
    c_gt0 = row_count(lambda t: t > 0.0)
    at0 = jnp.logical_and(c_gt0 < kf, c_ge0 >= kf)
    pos = c_gt0 >= kf
    lo0 = jnp.where(few, -3.0e38, jnp.where(pos, 0.0, jnp.where(at0, 0.0, rmin)))
    hi0 = jnp.where(pos, rmax * 1.000001 + 1e-37, 0.0)
    clo0 = jnp.where(few, kf, jnp.where(jnp.logical_or(pos, at0), c_ge0, nvalid))
    chi0 = jnp.where(pos, 0.0, c_ge0)

    def settled(clo, chi):
        return jnp.logical_or(clo == kf, kf - chi == 1.0)

    done0 = jnp.where(jnp.logical_or(jnp.logical_or(few, at0), settled(clo0, chi0)), 1.0, 0.0)

    def search_cond(st):
        it, _, _, _, _, done = st
        return jnp.logical_and(it < SEARCH_CAP, jnp.min(done) < 0.5)

    def search_body(st):
        it, lo, hi, clo, chi, done = st
        mid = lo + (hi - lo) * 0.5
        stuck = jnp.logical_or(mid <= lo, mid >= hi)
        cx = count_ge(mid)
        ge = cx >= kf
        act = done < 0.5
        up = jnp.logical_and(act, ge)
        dn = jnp.logical_and(act, jnp.logical_not(ge))
        lo = jnp.where(up, mid, lo)
        clo = jnp.where(up, cx, clo)
        hi = jnp.where(dn, mid, hi)
        chi = jnp.where(dn, cx, chi)
        done = jnp.where(jnp.logical_or(stuck, settled(clo, chi)), 1.0, done)
        return it + 1, lo, hi, clo, chi, done

    _, lo, hi, clo, chi, _ = lax.while_loop(
        search_cond, search_body, (jnp.int32(0), lo0, hi0, clo0, chi0, done0))

    def below_hi_max(j, mx):
        for r0 in range(0, TK, CH):
            t = sc_ref[j, r0:r0 + CH, :]
            mx = jnp.maximum(mx, _fold8(jnp.where(t < hi, t, -jnp.inf), jnp.max))
        return mx

    top_below = jnp.max(lax.fori_loop(0, ntile, below_hi_max, jnp.full((SUBLANES, TQ), -jnp.inf, F32)),
                        axis=0, keepdims=True)
    last_one = jnp.logical_and(jnp.logical_and(jnp.logical_not(few), clo != kf), kf - chi == 1.0)
    theta = jnp.where(last_one, top_below, lo)
    c_theta = count_ge(theta)

    surplus = jnp.where(jnp.logical_and(jnp.logical_not(few), c_theta > kf), c_theta - kf, 0.0)

    @pl.when(jnp.max(surplus) > 0.5)
    def _():
        ra = lax.broadcasted_iota(jnp.int32, (TK, TK), 0)
        ca = lax.broadcasted_iota(jnp.int32, (TK, TK), 1)
        triu = jnp.where(ca >= ra, 1.0, 0.0).astype(BF16)

        def body(jj, later):
            j = ntile - 1 - jj
            cnt = jnp.zeros((SUBLANES, TQ), F32)
            for r0 in range(0, TK, CH):
                eqf = jnp.where(sc_ref[j, r0:r0 + CH, :] == theta, 1.0, 0.0)
                p_ref[0, r0:r0 + CH, :] = eqf.astype(BF16)
                cnt = cnt + _fold8(eqf, jnp.sum)
            from_end = _dot(triu, p_ref[0])
            for r0 in range(0, TK, CH):
                t = sc_ref[j, r0:r0 + CH, :]
                cut = jnp.where(from_end[r0:r0 + CH] + later <= surplus, -jnp.inf, t)
                sc_ref[j, r0:r0 + CH, :] = jnp.where(t == theta, cut, t)
            return later + jnp.sum(cnt, axis=0, keepdims=True)

        lax.fori_loop(0, ntile, body, jnp.zeros((1, TQ), F32))

    _flash_init(m_ref, l_ref, acc_ref)
    q = q_ref[...]
    for h in range(HA):
        qm_ref[h] = _half_masked(q[:, (h // 2) * LANES:(h // 2 + 1) * LANES], h % 2 == 1)
    maps = [(h // 2, h, h) for h in range(HA)]

    def prep(j, par):
        for r0 in range(0, TK, CH):
            mb_ref[par, r0:r0 + CH, :] = jnp.where(sc_ref[j, r0:r0 + CH, :] >= theta, 0.0, NEG)

    def k_tile(j):
        row = pl.ds(pl.multiple_of(j * TK, TK), TK)
        return lambda p: k_ref[row, p * LANES:(p + 1) * LANES]

    def qk(j, s):
        return _dot_nt(k_tile(j)(maps[s][0]), qm_ref[s])

    def scores(j, par, near, only, st):
        _flash_scores(
            par, maps, k_tile(j),
            lambda r0: mb_ref[par, r0:r0 + CH, :],
            (lambda h, cc, r: _near_bias_t(bias_ref, h, i, j, cc, r)) if near else None,
            qm_ref, s_ref, m_ref, c_ref, only, st)

    def values(j, par, only, part):
        _flash_values(par, maps, lambda h: vt_ref[j, h * DA:(h + 1) * DA, :],
                      s_ref, p_ref, m_ref, l_ref, c_ref, acc_ref, only, part)

    _flash_sweep(jd, HA, qk, prep, lambda j, par, s, st: scores(j, par, False, s, st),
                 prep, lambda j, par, s, st: scores(j, par, True, s, st), values)

    for p in range(HA // 2):
        o_lo = acc_ref[2 * p] / l_ref[2 * p]
        o_hi = acc_ref[2 * p + 1] / l_ref[2 * p + 1]
        o_t = jnp.concatenate([o_lo, o_hi], axis=0)
        o_ref[:, p * LANES:(p + 1) * LANES] = o_t.T.astype(BF16)


def _dsa(q, k, vt, iq, ik2, iwt, bias, batch, seq):
    top_k = min(TOPK_MAX, seq // 4)
    nq = seq // TQ
    nt = seq // TK
    hd = HA * DA
    qrow = lambda b, i: (b * nq + i, 0)
    whole = lambda b, i: (b, 0)
    once = pl.Buffered(1)
    return pl.pallas_call(
        functools.partial(_dsa_kernel, top_k=top_k),
        grid=(batch, nq),
        in_specs=[
            pl.BlockSpec((TQ, hd), qrow),
            pl.BlockSpec((seq, hd), whole, pipeline_mode=once),
            pl.BlockSpec((nt, hd, TK), lambda b, i: (b, 0, 0), pipeline_mode=once),
            pl.BlockSpec((TQ, HI * DI), qrow),
            pl.BlockSpec((seq, LANES), whole, pipeline_mode=once),
            pl.BlockSpec((HI, TQ), lambda b, i: (0, b * nq + i)),
            pl.BlockSpec((HA, 2, LANES, LANES), lambda b, i: (0, 0, 0, 0)),
        ],
        out_specs=pl.BlockSpec((TQ, hd), qrow),
        out_shape=jax.ShapeDtypeStruct((batch * seq, hd), BF16),
        scratch_shapes=[pltpu.VMEM((nt, TK, TQ), F32)] + _flash_scratch(HA, DA),
        compiler_params=_params("parallel", "arbitrary"),
        name="dsa",
    )(q, k, vt, iq, ik2, iwt, bias)


def _diff_kernel(q_ref, k_ref, vt_ref, lam_ref, g_ref, bias_ref, o_ref,
                 qm_ref, s_ref, p_ref, m_ref, l_ref, c_ref, acc_ref, mb_ref, *, lam_init):
    i = pl.program_id(1)
    t0 = i * TQ
    jd = t0 // TK
    _flash_init(m_ref, l_ref, acc_ref)
    q = q_ref[...]
    nmap = 2 * HB
    for s in range(nmap):
        qm_ref[s] = _half_masked(q[:, (s // 2) * LANES:(s // 2 + 1) * LANES], s % 2 == 1)
    maps = [(s // 2, s // 2, s // 2) for s in range(nmap)]
    qpos = t0 + lax.broadcasted_iota(jnp.int32, (1, TQ), 1)

    def prep(j, par):
        for r0 in range(0, TK, CH):
            kpos = j * TK + r0 + lax.broadcasted_iota(jnp.int32, (CH, TQ), 0)
            mb_ref[par, r0:r0 + CH, :] = jnp.where(kpos <= qpos, 0.0, NEG)

    def k_tile(j):
        row = pl.ds(pl.multiple_of(j * TK, TK), TK)
        return lambda p: k_ref[row, p * LANES:(p + 1) * LANES]

    def qk(j, s):
        return _dot_nt(k_tile(j)(maps[s][0]), qm_ref[s])

    def scores(j, par, general, only, st):
        _flash_scores(
            par, maps, k_tile(j),
            (lambda r0: mb_ref[par, r0:r0 + CH, :]) if general else None,
            (lambda h, cc, r: _near_bias_t(bias_ref, h, i, j, cc, r)) if general else None,
            qm_ref, s_ref, m_ref, c_ref, only, st)

    def values(j, par, only, part):
        _flash_values(par, maps, lambda p: vt_ref[j, p * LANES:(p + 1) * LANES, :],
                      s_ref, p_ref, m_ref, l_ref, c_ref, acc_ref, only, part)

    _flash_sweep(jd, nmap, qk, lambda j, par: None, lambda j, par, s, st: scores(j, par, False, s, st),
                 prep, lambda j, par, s, st: scores(j, par, True, s, st), values)

    lam = lam_ref[...]
    lam_full = (jnp.exp(jnp.sum(lam[0:1] * lam[1:2], axis=1, keepdims=True))
                - jnp.exp(jnp.sum(lam[2:3] * lam[3:4], axis=1, keepdims=True)) + lam_init)
    for h in range(HB):
        o0 = acc_ref[2 * h] / l_ref[2 * h]
        o1 = acc_ref[2 * h + 1] / l_ref[2 * h + 1]
        o = (o0 - lam_full * o1).T
        y = o * lax.rsqrt(jnp.mean(o * o, axis=-1, keepdims=True) + LN_EPS) * g_ref[...]
        o_ref[:, h * LANES:(h + 1) * LANES] = (y * (1.0 - lam_init)).astype(BF16)


def _diff(q, k, vt, lam, g, bias, batch, seq, lam_init):
    nq = seq // TQ
    nt = seq // TK
    hb = HB * 2 * DB
    qrow = lambda b, i: (b * nq + i, 0)
    whole = lambda b, i: (b, 0)
    once = pl.Buffered(1)
    return pl.pallas_call(
        functools.partial(_diff_kernel, lam_init=lam_init),
        grid=(batch, nq),
        in_specs=[
            pl.BlockSpec((TQ, hb), qrow),
            pl.BlockSpec((seq, hb), whole, pipeline_mode=once),
            pl.BlockSpec((nt, hb, TK), lambda b, i: (b, 0, 0), pipeline_mode=once),
            pl.BlockSpec((4, DB), lambda b, i: (0, 0)),
            pl.BlockSpec((1, 2 * DB), lambda b, i: (0, 0)),
            pl.BlockSpec((HB, 2, LANES, LANES), lambda b, i: (0, 0, 0, 0)),
        ],
        out_specs=pl.BlockSpec((TQ, hb), qrow),
        out_shape=jax.ShapeDtypeStruct((batch * seq, hb), BF16),
        scratch_shapes=_flash_scratch(2 * HB, 2 * DB),
        compiler_params=_params("parallel", "arbitrary"),
        name="diff",
    )(q, k, vt, lam, g, bias)


def _mix_out_kernel(oa_ref, ob_ref, wa_ref, wb_ref, x_ref, g_ref, b_ref, wi_ref, wo_ref, g2_ref, b2_ref,
                    o_ref, x1_ref, xb_ref, acc_ref):
    m = _dot(oa_ref[...], wa_ref[...]) + _dot(ob_ref[...], wb_ref[...])
    x1_ref[...] = _layer_norm(ALPHA * x_ref[...] + m, g_ref[...], b_ref[...])
    _ffn_kernel(x1_ref, wi_ref, wo_ref, g2_ref, b2_ref, o_ref, xb_ref, acc_ref)


def _mix_out(oa, ob, wa, wb, x, g, b, ffn_w_in, ffn_w_out, g_all, b_all, layer):
    t, d = x.shape
    row = lambda i: (i, 0)
    const = lambda i: (0, 0)
    once = pl.Buffered(1)
    specs, scratch = _ffn_specs(d, ffn_w_out.shape[2], layer, 1)
    return pl.pallas_call(
        _mix_out_kernel,
        grid=(t // TM_FFN,),
        in_specs=[
            pl.BlockSpec((TM_FFN, oa.shape[1]), row),
            pl.BlockSpec((TM_FFN, ob.shape[1]), row),
            pl.BlockSpec(wa.shape, const, pipeline_mode=once),
            pl.BlockSpec(wb.shape, const, pipeline_mode=once),
            pl.BlockSpec((TM_FFN, d), row),
            pl.BlockSpec((1, d), const),
            pl.BlockSpec((1, d), const),
        ] + specs,
        out_specs=pl.BlockSpec((TM_FFN, d), row),
        out_shape=jax.ShapeDtypeStruct((t, d), F32),
        scratch_shapes=[pltpu.VMEM((TM_FFN, d), F32)] + scratch,
        compiler_params=_params("parallel"),
        name="mix_out_ffn",
    )(oa, ob, wa, wb, x, g, b, ffn_w_in, ffn_w_out, g_all, b_all)


def _conv_in_kernel(x_ref, w_ref, o_ref):
    xb = x_ref[...].astype(BF16)
    d = o_ref.shape[1]
    a = _dot(xb, w_ref[:, 0:d])
    gate = _dot(xb, w_ref[:, d:2 * d])
    o_ref[...] = a * jax.nn.sigmoid(gate)


def _conv_in(x, w):
    t, d = x.shape
    return pl.pallas_call(
        _conv_in_kernel,
        grid=(t // TM_CONV,),
        in_specs=[pl.BlockSpec((TM_CONV, d), lambda i: (i, 0)),
                  pl.BlockSpec((d, 2 * d), lambda i: (0, 0))],
        out_specs=pl.BlockSpec((TM_CONV, d), lambda i: (i, 0)),
        out_shape=jax.ShapeDtypeStruct((t, d), F32),
        compiler_params=_params("parallel"),
        name="conv_in",
    )(x, w)


def _conv_out_kernel(h_ref, halo_ref, dw_ref, dwb_ref, cg_ref, cb_ref, w_ref, x_ref, g_ref, b_ref,
                     wi_ref, wo_ref, g2_ref, b2_ref, o_ref, buf_ref, cv_ref, x1_ref, xb_ref, acc_ref):
    i = pl.program_id(1)
    buf_ref[0:HALO, :] = jnp.where(i > 0, halo_ref[...], 0.0)
    buf_ref[HALO:, :] = h_ref[...]
    off = HALO - (CONV_W - 1)

    def chunk(c, carry):
        r0 = pl.multiple_of(c * CONV_ROWS, CONV_ROWS)
        nwin = CONV_ROWS + HALO
        for lt in range(dwb_ref.shape[1] // LANES):
            cols = slice(lt * LANES, (lt + 1) * LANES)
            win = buf_ref[pl.ds(r0, nwin), cols]
            acc = jnp.broadcast_to(dwb_ref[:, cols], (CONV_ROWS, LANES))
            for sub in range(SUBLANES):
                shifted = pltpu.roll(win, nwin - sub, axis=0) if sub else win
                for k in range(CONV_W):
                    if (off + k) % SUBLANES == sub:
                        a0 = off + k - sub
                        acc = acc + dw_ref[k:k + 1, cols] * shifted[a0:a0 + CONV_ROWS]
            cv_ref[pl.ds(r0, CONV_ROWS), cols] = acc
        return carry

    lax.fori_loop(0, TM_CONV // CONV_ROWS, chunk, 0)
    y = _layer_norm(cv_ref[...], cg_ref[...], cb_ref[...])
    u = (y * jax.nn.sigmoid(y)).astype(BF16)
    m = _dot(u, w_ref[...])
    x1_ref[...] = _layer_norm(ALPHA * x_ref[...] + m, g_ref[...], b_ref[...])
    _ffn_kernel(x1_ref, wi_ref, wo_ref, g2_ref, b2_ref, o_ref, xb_ref, acc_ref)


def _conv_out(h, dw, dwb, cg, cb, w, x, g, b, ffn_w_in, ffn_w_out, g_all, b_all, layer, batch, seq):
    t, d = x.shape
    specs, scratch = _ffn_specs(d, ffn_w_out.shape[2], layer, 1)
    nt = seq // TM_CONV
    per = TM_CONV // HALO
    row = lambda bb, i: (bb * nt + i, 0)
    halo = lambda bb, i: (jnp.maximum((bb * nt + i) * per - 1, 0), 0)
    const = lambda bb, i: (0, 0)
    return pl.pallas_call(
        _conv_out_kernel,
        grid=(batch, nt),
        in_specs=[
            pl.BlockSpec((TM_CONV, d), row),
            pl.BlockSpec((HALO, d), halo),
            pl.BlockSpec(dw.shape, const),
            pl.BlockSpec((1, d), const),
            pl.BlockSpec((1, d), const),
            pl.BlockSpec((1, d), const),
            pl.BlockSpec((d, d), const, pipeline_mode=pl.Buffered(1)),
            pl.BlockSpec((TM_CONV, d), row),
            pl.BlockSpec((1, d), const),
            pl.BlockSpec((1, d), const),
        ] + specs,
        out_specs=pl.BlockSpec((TM_CONV, d), row),
        out_shape=jax.ShapeDtypeStruct((t, d), F32),
        scratch_shapes=[pltpu.VMEM((TM_CONV + HALO, d), F32), pltpu.VMEM((TM_CONV, d), F32),
                        pltpu.VMEM((TM_CONV, d), F32)] + scratch,
        compiler_params=_params("parallel", "arbitrary"),
        name="conv_out_ffn",
    )(h, h, dw, dwb, cg, cb, w, x, g, b, ffn_w_in, ffn_w_out, g_all, b_all)


def _attention_mixer(x, w_in, ikg, ikb, lam, subg, w_out, bias, g, b, ffn_tail, batch, seq, layer_idx):
    c = [0]
    for n in (HA * DA, HA * DA, HA * DA, HI * DI, DI, HI, HB * 2 * DB, HB * 2 * DB, HB * 2 * DB):
        c.append(c[-1] + n)
    w_ik = w_in[:, c[4]:c[5]]
    w_iw = jnp.pad(w_in[:, c[5]:c[6]], ((0, 0), (0, LANES - HI)))
    w = jnp.concatenate([w_in[:, :c[4]], w_ik, w_ik, w_iw, w_in[:, c[6]:]], axis=1).astype(BF16)
    ikg2 = jnp.concatenate([ikg, ikg])[None]
    ikb2 = jnp.concatenate([ikb, ikb])[None]
    qa, ka, vat, iq, ik2, iwt, qb, kb, vbt = _mix_proj(x, w, ikg2, ikb2)
    o_a = _dsa(qa, ka, vat, iq, ik2, iwt, bias[:HA], batch, seq)
    lam_init = 0.8 - 0.6 * math.exp(-0.3 * layer_idx)
    o_b = _diff(qb, kb, vbt, lam, subg[None], bias[HA:], batch, seq, lam_init)
    wo = w_out.astype(BF16)
    return _mix_out(o_a, o_b, wo[:HA * DA], wo[HA * DA:], x, g, b, *ffn_tail, layer_idx)


def _conv_module(x, w_in, dw, dwb, cg, cb, w_out, g, b, ffn_tail, layer_idx, batch, seq):
    h = _conv_in(x, w_in.astype(BF16))
    dwp = jnp.pad(dw, ((0, HALO - CONV_W), (0, 0)))
    return _conv_out(h, dwp, dwb[None], cg[None], cb[None], w_out.astype(BF16), x, g, b,
                     *ffn_tail, layer_idx, batch, seq)


def kernel(x, ffn_in, ffn_out, ln_g, ln_b, rel_bias, mix_w_in, idx_k_g, idx_k_b, diff_lambda,
           diff_subln_g, mix_w_out, conv_w_in, conv_dw, conv_dw_b, conv_ln_g, conv_ln_b, conv_w_out):
    batch, seq, d = x.shape
    x = x.reshape(batch * seq, d)
    bias = _bias_tiles(rel_bias.T)
    ffn_in_b = ffn_in.astype(BF16)
    ffn_out_b = ffn_out.astype(BF16)
    g_all = ln_g.reshape(3 * DEPTH, 1, d)
    b_all = ln_b.reshape(3 * DEPTH, 1, d)
    ffn_tail = (ffn_in_b, ffn_out_b, g_all, b_all)
    for l in range(DEPTH):
        j = l // 2
        x = _ffn(x, ffn_in_b, ffn_out_b, g_all, b_all, l, 0)
        if l % 2 == 0:
            x = _attention_mixer(x, mix_w_in[j], idx_k_g[j], idx_k_b[j], diff_lambda[j], diff_subln_g[j],
                                 mix_w_out[j], bias, ln_g[l, 1][None], ln_b[l, 1][None], ffn_tail, batch, seq, l)
        else:
            x = _conv_module(x, conv_w_in[j], conv_dw[j], conv_dw_b[j], conv_ln_g[j], conv_ln_b[j],
                             conv_w_out[j], ln_g[l, 1][None], ln_b[l, 1][None], ffn_tail, l, batch, seq)
    return x.reshape(batch, seq, d)
```

```python
import functools
import math

import jax
import jax.numpy as jnp
from jax import lax
from jax.experimental import pallas as pl
from jax.experimental.pallas import tpu as pltpu

F32 = jnp.float32
BF16 = jnp.bfloat16

D_MODEL = 1024
DEPTH = 4
HA, DA = 8, 64
HI, DI = 8, 64
TOPK_MAX = 256
HB, DB = 4, 64
CONV_W = 31
D_FF = 2816
NUM_BUCKETS = 32
MAX_DISTANCE = 128
ALPHA = (2 * DEPTH) ** 0.25
LN_EPS = 1e-5

LANES = 128
SUBLANES = 8
BF16_ROWS = 16
VMEM_LIMIT = 56 * 1024 * 1024

TQ = 256
TK = 512
QB = TQ // LANES
KB = TK // LANES
CH = 64
TM_FFN = 512
TF_FFN = 256
TM_PROJ = TK
TM_CONV = 512
HALO = 32
CONV_ROWS = 128
NEG = -1e30
LOG2E = math.log2(math.e)
SEARCH_CAP = 1200


def _params(*sem):
    return pltpu.CompilerParams(dimension_semantics=sem, vmem_limit_bytes=VMEM_LIMIT)


def _layer_norm(v, g, b):
    mu = jnp.mean(v, axis=-1, keepdims=True)
    c = v - mu
    var = jnp.mean(c * c, axis=-1, keepdims=True)
    return c * lax.rsqrt(var + LN_EPS) * g + b


def _dot(a, b):
    return jnp.dot(a, b, preferred_element_type=F32)


def _dot_nt(a, b):
    return lax.dot_general(a, b, (((1,), (1,)), ((), ())), preferred_element_type=F32)


def _ffn_kernel(x_ref, wi_ref, wo_ref, g_ref, b_ref, o_ref, xb_ref, acc_ref):
    f = wo_ref.shape[0]
    xb_ref[...] = x_ref[...].astype(BF16)
    for c in range(f // TF_FFN):
        xb = xb_ref[...]
        a = _dot(xb, wi_ref[:, c * TF_FFN:(c + 1) * TF_FFN])
        u = _dot(xb, wi_ref[:, f + c * TF_FFN:f + (c + 1) * TF_FFN])
        h = (a * jax.nn.sigmoid(a) * u).astype(BF16)
        part = _dot(h, wo_ref[c * TF_FFN:(c + 1) * TF_FFN, :])
        if c == 0:
            acc_ref[...] = part
        else:
            acc_ref[...] += part
    y = ALPHA * x_ref[...] + 0.5 * acc_ref[...]
    o_ref[...] = _layer_norm(y, g_ref[...], b_ref[...])


def _ffn_specs(d, f, layer, half):
    once = pl.Buffered(1)
    ln = 3 * layer + 2 * half
    specs = [
        pl.BlockSpec((None, None, d, 2 * f), lambda *_: (layer, half, 0, 0), pipeline_mode=once),
        pl.BlockSpec((None, None, f, d), lambda *_: (layer, half, 0, 0), pipeline_mode=once),
        pl.BlockSpec((None, 1, d), lambda *_: (ln, 0, 0)),
        pl.BlockSpec((None, 1, d), lambda *_: (ln, 0, 0)),
    ]
    scratch = [pltpu.VMEM((TM_FFN, d), BF16), pltpu.VMEM((TM_FFN, d), F32)]
    return specs, scratch


def _ffn(x, w_in, w_out, g, b, layer, half):
    t, d = x.shape
    specs, scratch = _ffn_specs(d, w_out.shape[2], layer, half)
    return pl.pallas_call(
        _ffn_kernel,
        grid=(t // TM_FFN,),
        in_specs=[pl.BlockSpec((TM_FFN, d), lambda i: (i, 0))] + specs,
        out_specs=pl.BlockSpec((TM_FFN, d), lambda i: (i, 0)),
        out_shape=jax.ShapeDtypeStruct((t, d), F32),
        scratch_shapes=scratch,
        compiler_params=_params("parallel"),
        name="ffn",
    )(x, w_in, w_out, g, b)


_W3 = 3 * HA * DA
_C_IQ = _W3
_C_IK = _C_IQ + HI * DI
_C_IW = _C_IK + LANES
_C_B = _C_IW + LANES
_C_END = _C_B + 3 * HB * 2 * DB


def _mix_proj_kernel(x_ref, w_ref, ikg_ref, ikb_ref,
                     qa_ref, ka_ref, vat_ref, iq_ref, ik_ref, iwt_ref, qb_ref, kb_ref, vbt_ref):
    xb = x_ref[...].astype(BF16)
    hd = HA * DA
    qa_ref[...] = (_dot(xb, w_ref[:, 0:hd]) * (DA ** -0.5 * LOG2E)).astype(BF16)
    ka_ref[...] = _dot(xb, w_ref[:, hd:2 * hd]).astype(BF16)
    vat_ref[0] = _dot(xb, w_ref[:, 2 * hd:3 * hd]).T.astype(BF16)
    iq_ref[...] = (_dot(xb, w_ref[:, _C_IQ:_C_IK]) * (DI ** -0.5)).astype(BF16)
    ik2 = _dot(xb, w_ref[:, _C_IK:_C_IW])
    ik_ref[...] = _layer_norm(ik2, ikg_ref[...], ikb_ref[...]).astype(BF16)
    iw = _dot(xb, w_ref[:, _C_IW:_C_B]) * (HI ** -0.5)
    iwt_ref[...] = iw.T[0:HI, :]
    hb = HB * 2 * DB
    qb_ref[...] = (_dot(xb, w_ref[:, _C_B:_C_B + hb]) * (DB ** -0.5 * LOG2E)).astype(BF16)
    kb_ref[...] = _dot(xb, w_ref[:, _C_B + hb:_C_B + 2 * hb]).astype(BF16)
    vbt_ref[0] = _dot(xb, w_ref[:, _C_B + 2 * hb:_C_END]).T.astype(BF16)


def _mix_proj(x, w, ikg2, ikb2):
    t, d = x.shape
    nt = t // TM_PROJ
    row = lambda i: (i, 0)
    const = lambda i: (0, 0)
    tile = lambda i: (i, 0, 0)
    wide = lambda n, dt: jax.ShapeDtypeStruct((t, n), dt)
    hd, hb = HA * DA, HB * 2 * DB
    return pl.pallas_call(
        _mix_proj_kernel,
        grid=(nt,),
        in_specs=[
            pl.BlockSpec((TM_PROJ, d), row),
            pl.BlockSpec((d, _C_END), const),
            pl.BlockSpec((1, LANES), const),
            pl.BlockSpec((1, LANES), const),
        ],
        out_specs=[
            pl.BlockSpec((TM_PROJ, hd), row), pl.BlockSpec((TM_PROJ, hd), row),
            pl.BlockSpec((1, hd, TM_PROJ), tile), pl.BlockSpec((TM_PROJ, HI * DI), row),
            pl.BlockSpec((TM_PROJ, LANES), row), pl.BlockSpec((HI, TM_PROJ), lambda i: (0, i)),
            pl.BlockSpec((TM_PROJ, hb), row), pl.BlockSpec((TM_PROJ, hb), row),
            pl.BlockSpec((1, hb, TM_PROJ), tile),
        ],
        out_shape=[wide(hd, BF16), wide(hd, BF16), jax.ShapeDtypeStruct((nt, hd, TM_PROJ), BF16),
                   wide(HI * DI, BF16), wide(LANES, BF16), jax.ShapeDtypeStruct((HI, t), F32),
                   wide(hb, BF16), wide(hb, BF16), jax.ShapeDtypeStruct((nt, hb, TM_PROJ), BF16)],
        compiler_params=_params("parallel"),
        name="mix_proj",
    )(x, w, ikg2, ikb2)


def _bias_kernel(tbl_ref, o_ref):
    nh = o_ref.shape[0]
    kk = lax.broadcasted_iota(jnp.int32, (LANES, LANES), 0)
    qq = lax.broadcasted_iota(jnp.int32, (LANES, LANES), 1)
    max_exact = NUM_BUCKETS // 2
    for blk in range(2):
        n = jnp.maximum(qq - kk + blk * LANES, 0)
        nf = jnp.maximum(n, 1).astype(F32)
        large = max_exact + (jnp.log(nf / max_exact) / math.log(MAX_DISTANCE / max_exact)
                             * (NUM_BUCKETS - max_exact)).astype(jnp.int32)
        large = jnp.minimum(large, NUM_BUCKETS - 1)
        bucket = jnp.where(n < max_exact, n, large)
        for h in range(nh):
            acc = jnp.zeros((LANES, LANES), F32)
            for k in range(NUM_BUCKETS):
                acc = jnp.where(bucket == k, tbl_ref[h, k], acc)
            o_ref[h, blk] = (acc - tbl_ref[h, NUM_BUCKETS - 1]) * LOG2E


def _bias_tiles(rel_bias_t):
    nh = rel_bias_t.shape[0]
    return pl.pallas_call(
        _bias_kernel,
        in_specs=[pl.BlockSpec(memory_space=pltpu.SMEM)],
        out_shape=jax.ShapeDtypeStruct((nh, 2, LANES, LANES), F32),
        name="bias_tiles",
    )(rel_bias_t)


def _half_masked(x, upper):
    lane = lax.broadcasted_iota(jnp.int32, x.shape, 1)
    keep = (lane >= LANES // 2) if upper else (lane < LANES // 2)
    return jnp.where(keep, x, jnp.zeros_like(x))


def _near_bias_t(bias_ref, head, i, j, cc, r0):
    b0 = bias_ref[head, 0, r0:r0 + CH, :]
    b1 = bias_ref[head, 1, r0:r0 + CH, :]
    cols = []
    for qq in range(QB):
        delta = (i * QB + qq) - (j * KB + cc)
        cols.append(jnp.where(delta == 0, b0, jnp.where(delta == 1, b1, jnp.zeros_like(b0))))
    return jnp.concatenate(cols, axis=1)


def _fold8(x, op):
    return op(x.reshape(CH // SUBLANES, SUBLANES, x.shape[1]), axis=0)


def _flash_scores(par, maps, k_tile, mask_bias, near_bias, qm_ref, s_ref, m_ref, c_ref, only=None, st=None,
                  rows=TK):
    nmap = len(maps)
    cur = par * nmap
    prv = nmap - cur
    nch = LANES // CH
    for s in (range(nmap) if only is None else (only,)):
        pair, _, head = maps[s]
        if st is None or only is None:
            st = _dot_nt(k_tile(pair, rows), qm_ref[s])
        mx = jnp.full((SUBLANES, TQ), NEG, F32)
        for cc in range(rows // LANES):
            for hh in range(nch):
                r0 = cc * LANES + hh * CH
                piece = st[r0:r0 + CH]
                if mask_bias is not None:
                    piece = piece + mask_bias(r0)
                if near_bias is not None:
                    piece = piece + near_bias(head, cc, hh * CH)
                s_ref[cur + s, r0:r0 + CH, :] = piece
                mx = jnp.maximum(mx, _fold8(piece, jnp.max))
        m_prev = m_ref[prv + s]
        m_next = jnp.maximum(m_prev, jnp.max(mx, axis=0, keepdims=True))
        c_ref[cur + s] = jnp.exp2(m_prev - m_next)
        m_ref[cur + s] = m_next


def _flash_values(par, maps, vt_tile, s_ref, p_ref, m_ref, l_ref, c_ref, acc_ref, only=None, part=None, rows=TK):
    nmap = len(maps)
    cur = par * nmap
    for s in (range(nmap) if only is None else (only,)):
        if part != "pv":
            m_new = m_ref[cur + s]
            for r0 in range(0, rows, CH):
                p_ref[s, r0:r0 + CH, :] = jnp.exp2(s_ref[cur + s, r0:r0 + CH, :] - m_new).astype(BF16)
        if part == "probs":
            continue
        lhs = jnp.concatenate([vt_tile(maps[s][1], rows), jnp.ones((BF16_ROWS, rows), BF16)], axis=0)
        pv = _dot(lhs, p_ref[s, 0:rows, :])
        dv = acc_ref.shape[1]
        l_ref[s] = c_ref[cur + s] * l_ref[s] + pv[dv:dv + 1]
        acc_ref[s] = c_ref[cur + s] * acc_ref[s] + pv[:dv]


def _flash_sweep(jd, half_last, nmap, qk, prep_far, scores_far, prep_any, scores_any, values):
    def step(j_next, par_next, prep, scores, rows=TK):
        prep(j_next, par_next)
        st = qk(j_next, 0, rows)
        for s in range(nmap):
            values(j_next - 1, 1 - par_next, s, "probs", TK)
            st_after = qk(j_next, s + 1, rows) if s + 1 < nmap else None
            scores(j_next, par_next, s, st, rows)
            values(j_next - 1, 1 - par_next, s, "pv", TK)
            st = st_after

    prep_any(0, 0)
    for s in range(nmap):
        scores_any(0, 0, s, None, TK)

    def body(k, c):
        step(2 * k + 1, 1, prep_far, scores_far)
        step(2 * k + 2, 0, prep_far, scores_far)
        return c

    npair = jnp.maximum(jd - 2, 0) // 2
    lax.fori_loop(0, npair, body, 0)
    t = 2 * npair
    left = jd - t
    for r in range(1, 4):
        short = jnp.logical_and(left == r, half_last)

        @pl.when(jnp.logical_and(left >= r, jnp.logical_not(short)))
        def _(r=r):
            step(t + r, r % 2, prep_any, scores_any)

        @pl.when(short)
        def _(r=r):
            step(t + r, r % 2, prep_any, scores_any, TK // 2)

    for par in range(2):
        last_par = lax.rem(left, 2) == par

        @pl.when(jnp.logical_and(last_par, jnp.logical_not(half_last)))
        def _(par=par):
            values(jd, par, None, None, TK)

        @pl.when(jnp.logical_and(last_par, half_last))
        def _(par=par):
            values(jd, par, None, None, TK // 2)


def _flash_init(m_ref, l_ref, acc_ref):
    m_ref[...] = jnp.full(m_ref.shape, NEG, F32)
    l_ref[...] = jnp.zeros_like(l_ref)
    acc_ref[...] = jnp.zeros_like(acc_ref)


def _flash_scratch(nmap, dv):
    return [
        pltpu.VMEM((nmap, TQ, LANES), BF16),
        pltpu.VMEM((2 * nmap, TK, TQ), F32),
        pltpu.VMEM((nmap, TK, TQ), BF16),
        pltpu.VMEM((2 * nmap, 1, TQ), F32),
        pltpu.VMEM((nmap, 1, TQ), F32),
        pltpu.VMEM((2 * nmap, 1, TQ), F32),
        pltpu.VMEM((nmap, dv, TQ), F32),
        pltpu.VMEM((2, TK, TQ), F32),
    ]


def _dsa_kernel(q_ref, k_ref, vt_ref, iq_ref, ik_ref, iwt_ref, bias_ref, o_ref,
                sc_ref, qm_ref, s_ref, p_ref, m_ref, l_ref, c_ref, acc_ref, mb_ref, *, top_k):
    i = pl.program_id(1)
    t0 = i * TQ
    jd = t0 // TK
    ntile = jd + 1
    qpos = t0 + lax.broadcasted_iota(jnp.int32, (1, TQ), 1)

    iq = iq_ref[...]
    for h in range(HI):
        qm_ref[h] = _half_masked(iq[:, (h // 2) * LANES:(h // 2 + 1) * LANES], h % 2 == 1)
    iwt = iwt_ref[...]

    def score_tile(j, carry, diag):
        rmax, rmin = carry
        ik_t = ik_ref[pl.ds(pl.multiple_of(j * TK, TK), TK), :]
        for hp in range(HI // 2):
            da = _dot_nt(ik_t, qm_ref[2 * hp])
            db = _dot_nt(ik_t, qm_ref[2 * hp + 1])
            wa = iwt[2 * hp:2 * hp + 1, :]
            wb = iwt[2 * hp + 1:2 * hp + 2, :]
            for r0 in range(0, TK, CH):
                piece = wa * jnp.maximum(da[r0:r0 + CH], 0.0) + wb * jnp.maximum(db[r0:r0 + CH], 0.0)
                if hp > 0:
                    piece = piece + sc_ref[j, r0:r0 + CH, :]
                if hp == HI // 2 - 1:
                    if diag:
                        kpos = j * TK + r0 + lax.broadcasted_iota(jnp.int32, (CH, TQ), 0)
                        valid = kpos <= qpos
                        rmax = jnp.maximum(rmax, _fold8(jnp.where(valid, piece, -jnp.inf), jnp.max))
                        rmin = jnp.minimum(rmin, _fold8(jnp.where(valid, piece, jnp.inf), jnp.min))
                        piece = jnp.where(valid, piece, -jnp.inf)
                    else:
                        rmax = jnp.maximum(rmax, _fold8(piece, jnp.max))
                        rmin = jnp.minimum(rmin, _fold8(piece, jnp.min))
                sc_ref[j, r0:r0 + CH, :] = piece
        return rmax, rmin

    carry = lax.fori_loop(
        0, jd, lambda j, c: score_tile(j, c, False),
        (jnp.full((SUBLANES, TQ), -jnp.inf, F32), jnp.full((SUBLANES, TQ), jnp.inf, F32)))
    rmax8, rmin8 = score_tile(jd, carry, True)
    rmax = jnp.max(rmax8, axis=0, keepdims=True)
    rmin = jnp.min(rmin8, axis=0, keepdims=True)

    def row_count(pred):
        def body(j, cnt):
            for r0 in range(0, TK, CH):
                cnt = cnt + _fold8(jnp.where(pred(sc_ref[j, r0:r0 + CH, :]), 1.0, 0.0), jnp.sum)
            return cnt
        cnt = lax.fori_loop(0, ntile, body, jnp.zeros((SUBLANES, TQ), F32))
        return jnp.sum(cnt, axis=0, keepdims=True)

    def count_ge(x):
        return row_count(lambda t: t >= x)

    kf = float(top_k)
    nvalid = (qpos + 1).astype(F32)
    few = nvalid <= kf
    c_ge0 = count_ge(0.0)
    c_gt0 = row_count(lambda t: t > 0.0)
    at0 = jnp.logical_and(c_gt0 < kf, c_ge0 >= kf)
    pos = c_gt0 >= kf
    lo0 = jnp.where(few, -3.0e38, jnp.where(pos, 0.0, jnp.where(at0, 0.0, rmin)))
    hi0 = jnp.where(pos, rmax * 1.000001 + 1e-37, 0.0)
    clo0 = jnp.where(few, kf, jnp.where(jnp.logical_or(pos, at0), c_ge0, nvalid))
    chi0 = jnp.where(pos, 0.0, c_ge0)

    def settled(clo, chi):
        return jnp.logical_or(clo == kf, kf - chi == 1.0)

    done0 = jnp.where(jnp.logical_or(jnp.logical_or(few, at0), settled(clo0, chi0)), 1.0, 0.0)

    def search_cond(st):
        it, _, _, _, _, done = st
        return jnp.logical_and(it < SEARCH_CAP, jnp.min(done) < 0.5)

    def search_body(st):
        it, lo, hi, clo, chi, done = st
        mid = lo + (hi - lo) * 0.5
        stuck = jnp.logical_or(mid <= lo, mid >= hi)
        cx = count_ge(mid)
        ge = cx >= kf
        act = done < 0.5
        up = jnp.logical_and(act, ge)
        dn = jnp.logical_and(act, jnp.logical_not(ge))
        lo = jnp.where(up, mid, lo)
        clo = jnp.where(up, cx, clo)
        hi = jnp.where(dn, mid, hi)
        chi = jnp.where(dn, cx, chi)
        done = jnp.where(jnp.logical_or(stuck, settled(clo, chi)), 1.0, done)
        return it + 1, lo, hi, clo, chi, done

    _, lo, hi, clo, chi, _ = lax.while_loop(
        search_cond, search_body, (jnp.int32(0), lo0, hi0, clo0, chi0, done0))

    def below_hi_max(j, mx):
        for r0 in range(0, TK, CH):
            t = sc_ref[j, r0:r0 + CH, :]
            mx = jnp.maximum(mx, _fold8(jnp.where(t < hi, t, -jnp.inf), jnp.max))
        return mx

    top_below = jnp.max(lax.fori_loop(0, ntile, below_hi_max, jnp.full((SUBLANES, TQ), -jnp.inf, F32)),
                        axis=0, keepdims=True)
    last_one = jnp.logical_and(jnp.logical_and(jnp.logical_not(few), clo != kf), kf - chi == 1.0)
    theta = jnp.where(last_one, top_below, lo)
    c_theta = count_ge(theta)

    surplus = jnp.where(jnp.logical_and(jnp.logical_not(few), c_theta > kf), c_theta - kf, 0.0)

    @pl.when(jnp.max(surplus) > 0.5)
    def _():
        ra = lax.broadcasted_iota(jnp.int32, (TK, TK), 0)
        ca = lax.broadcasted_iota(jnp.int32, (TK, TK), 1)
        triu = jnp.where(ca >= ra, 1.0, 0.0).astype(BF16)

        def body(jj, later):
            j = ntile - 1 - jj
            cnt = jnp.zeros((SUBLANES, TQ), F32)
            for r0 in range(0, TK, CH):
                eqf = jnp.where(sc_ref[j, r0:r0 + CH, :] == theta, 1.0, 0.0)
                p_ref[0, r0:r0 + CH, :] = eqf.astype(BF16)
                cnt = cnt + _fold8(eqf, jnp.sum)
            from_end = _dot(triu, p_ref[0])
            for r0 in range(0, TK, CH):
                t = sc_ref[j, r0:r0 + CH, :]
                cut = jnp.where(from_end[r0:r0 + CH] + later <= surplus, -jnp.inf, t)
                sc_ref[j, r0:r0 + CH, :] = jnp.where(t == theta, cut, t)
            return later + jnp.sum(cnt, axis=0, keepdims=True)

        lax.fori_loop(0, ntile, body, jnp.zeros((1, TQ), F32))

    _flash_init(m_ref, l_ref, acc_ref)
    q = q_ref[...]
    for h in range(HA):
        qm_ref[h] = _half_masked(q[:, (h // 2) * LANES:(h // 2 + 1) * LANES], h % 2 == 1)
    maps = [(h // 2, h, h) for h in range(HA)]

    def prep(j, par):
        for r0 in range(0, TK, CH):
            mb_ref[par, r0:r0 + CH, :] = jnp.where(sc_ref[j, r0:r0 + CH, :] >= theta, 0.0, NEG)

    def k_tile(j):
        base = pl.multiple_of(j * TK, TK)
        return lambda p, rows: k_ref[pl.ds(base, rows), p * LANES:(p + 1) * LANES]

    def qk(j, s, rows):
        return _dot_nt(k_tile(j)(maps[s][0], rows), qm_ref[s])

    def scores(j, par, near, only, st, rows):
        _flash_scores(
            par, maps, k_tile(j),
            lambda r0: mb_ref[par, r0:r0 + CH, :],
            (lambda h, cc, r: _near_bias_t(bias_ref, h, i, j, cc, r)) if near else None,
            qm_ref, s_ref, m_ref, c_ref, only, st, rows)

    def values(j, par, only, part, rows):
        _flash_values(par, maps, lambda h, rows: vt_ref[j, h * DA:(h + 1) * DA, 0:rows],
                      s_ref, p_ref, m_ref, l_ref, c_ref, acc_ref, only, part, rows)

    half_last = lax.rem(i, TK // TQ) == 0
    _flash_sweep(jd, half_last, HA, qk, prep, lambda j, par, s, st, rows: scores(j, par, False, s, st, rows),
                 prep, lambda j, par, s, st, rows: scores(j, par, True, s, st, rows), values)

    for p in range(HA // 2):
        o_lo = acc_ref[2 * p] / l_ref[2 * p]
        o_hi = acc_ref[2 * p + 1] / l_ref[2 * p + 1]
        o_t = jnp.concatenate([o_lo, o_hi], axis=0)
        o_ref[:, p * LANES:(p + 1) * LANES] = o_t.T.astype(BF16)


def _dsa(q, k, vt, iq, ik2, iwt, bias, batch, seq):
    top_k = min(TOPK_MAX, seq // 4)
    nq = seq // TQ
    nt = seq // TK
    hd = HA * DA
    qrow = lambda b, i: (b * nq + i, 0)
    whole = lambda b, i: (b, 0)
    once = pl.Buffered(1)
    return pl.pallas_call(
        functools.partial(_dsa_kernel, top_k=top_k),
        grid=(batch, nq),
        in_specs=[
            pl.BlockSpec((TQ, hd), qrow),
            pl.BlockSpec((seq, hd), whole, pipeline_mode=once),
            pl.BlockSpec((nt, hd, TK), lambda b, i: (b, 0, 0), pipeline_mode=once),
            pl.BlockSpec((TQ, HI * DI), qrow),
            pl.BlockSpec((seq, LANES), whole, pipeline_mode=once),
            pl.BlockSpec((HI, TQ), lambda b, i: (0, b * nq + i)),
            pl.BlockSpec((HA, 2, LANES, LANES), lambda b, i: (0, 0, 0, 0)),
        ],
        out_specs=pl.BlockSpec((TQ, hd), qrow),
        out_shape=jax.ShapeDtypeStruct((batch * seq, hd), BF16),
        scratch_shapes=[pltpu.VMEM((nt, TK, TQ), F32)] + _flash_scratch(HA, DA),
        compiler_params=_params("parallel", "arbitrary"),
        name="dsa",
    )(q, k, vt, iq, ik2, iwt, bias)


def _diff_kernel(q_ref, k_ref, vt_ref, lam_ref, g_ref, bias_ref, o_ref,
                 qm_ref, s_ref, p_ref, m_ref, l_ref, c_ref, acc_ref, mb_ref, *, lam_init):
    i = pl.program_id(1)
    t0 = i * TQ
    jd = t0 // TK
    _flash_init(m_ref, l_ref, acc_ref)
    q = q_ref[...]
    nmap = 2 * HB
    for s in range(nmap):
        qm_ref[s] = _half_masked(q[:, (s // 2) * LANES:(s // 2 + 1) * LANES], s % 2 == 1)
    maps = [(s // 2, s // 2, s // 2) for s in range(nmap)]
    qpos = t0 + lax.broadcasted_iota(jnp.int32, (1, TQ), 1)

    def prep(j, par):
        for r0 in range(0, TK, CH):
            kpos = j * TK + r0 + lax.broadcasted_iota(jnp.int32, (CH, TQ), 0)
            mb_ref[par, r0:r0 + CH, :] = jnp.where(kpos <= qpos, 0.0, NEG)

    def k_tile(j):
        base = pl.multiple_of(j * TK, TK)
        return lambda p, rows: k_ref[pl.ds(base, rows), p * LANES:(p + 1) * LANES]

    def qk(j, s, rows):
        return _dot_nt(k_tile(j)(maps[s][0], rows), qm_ref[s])

    def scores(j, par, general, only, st, rows):
        _flash_scores(
            par, maps, k_tile(j),
            (lambda r0: mb_ref[par, r0:r0 + CH, :]) if general else None,
            (lambda h, cc, r: _near_bias_t(bias_ref, h, i, j, cc, r)) if general else None,
            qm_ref, s_ref, m_ref, c_ref, only, st, rows)

    def values(j, par, only, part, rows):
        _flash_values(par, maps, lambda p, rows: vt_ref[j, p * LANES:(p + 1) * LANES, 0:rows],
                      s_ref, p_ref, m_ref, l_ref, c_ref, acc_ref, only, part, rows)

    half_last = lax.rem(i, TK // TQ) == 0
    _flash_sweep(jd, half_last, nmap, qk, lambda j, par: None,
                 lambda j, par, s, st, rows: scores(j, par, False, s, st, rows),
                 prep, lambda j, par, s, st, rows: scores(j, par, True, s, st, rows), values)

    lam = lam_ref[...]
    lam_full = (jnp.exp(jnp.sum(lam[0:1] * lam[1:2], axis=1, keepdims=True))
                - jnp.exp(jnp.sum(lam[2:3] * lam[3:4], axis=1, keepdims=True)) + lam_init)
    for h in range(HB):
        o0 = acc_ref[2 * h] / l_ref[2 * h]
        o1 = acc_ref[2 * h + 1] / l_ref[2 * h + 1]
        o = (o0 - lam_full * o1).T
        y = o * lax.rsqrt(jnp.mean(o * o, axis=-1, keepdims=True) + LN_EPS) * g_ref[...]
        o_ref[:, h * LANES:(h + 1) * LANES] = (y * (1.0 - lam_init)).astype(BF16)


def _diff(q, k, vt, lam, g, bias, batch, seq, lam_init):
    nq = seq // TQ
    nt = seq // TK
    hb = HB * 2 * DB
    qrow = lambda b, i: (b * nq + i, 0)
    whole = lambda b, i: (b, 0)
    once = pl.Buffered(1)
    return pl.pallas_call(
        functools.partial(_diff_kernel, lam_init=lam_init),
        grid=(batch, nq),
        in_specs=[
            pl.BlockSpec((TQ, hb), qrow),
            pl.BlockSpec((seq, hb), whole, pipeline_mode=once),
            pl.BlockSpec((nt, hb, TK), lambda b, i: (b, 0, 0), pipeline_mode=once),
            pl.BlockSpec((4, DB), lambda b, i: (0, 0)),
            pl.BlockSpec((1, 2 * DB), lambda b, i: (0, 0)),
            pl.BlockSpec((HB, 2, LANES, LANES), lambda b, i: (0, 0, 0, 0)),
        ],
        out_specs=pl.BlockSpec((TQ, hb), qrow),
        out_shape=jax.ShapeDtypeStruct((batch * seq, hb), BF16),
        scratch_shapes=_flash_scratch(2 * HB, 2 * DB),
        compiler_params=_params("parallel", "arbitrary"),
        name="diff",
    )(q, k, vt, lam, g, bias)


def _mix_out_kernel(oa_ref, ob_ref, wa_ref, wb_ref, x_ref, g_ref, b_ref, wi_ref, wo_ref, g2_ref, b2_ref,
                    o_ref, x1_ref, xb_ref, acc_ref):
    m = _dot(oa_ref[...], wa_ref[...]) + _dot(ob_ref[...], wb_ref[...])
    x1_ref[...] = _layer_norm(ALPHA * x_ref[...] + m, g_ref[...], b_ref[...])
    _ffn_kernel(x1_ref, wi_ref, wo_ref, g2_ref, b2_ref, o_ref, xb_ref, acc_ref)


def _mix_out(oa, ob, wa, wb, x, g, b, ffn_w_in, ffn_w_out, g_all, b_all, layer):
    t, d = x.shape
    row = lambda i: (i, 0)
    const = lambda i: (0, 0)
    once = pl.Buffered(1)
    specs, scratch = _ffn_specs(d, ffn_w_out.shape[2], layer, 1)
    return pl.pallas_call(
        _mix_out_kernel,
        grid=(t // TM_FFN,),
        in_specs=[
            pl.BlockSpec((TM_FFN, oa.shape[1]), row),
            pl.BlockSpec((TM_FFN, ob.shape[1]), row),
            pl.BlockSpec(wa.shape, const, pipeline_mode=once),
            pl.BlockSpec(wb.shape, const, pipeline_mode=once),
            pl.BlockSpec((TM_FFN, d), row),
            pl.BlockSpec((1, d), const),
            pl.BlockSpec((1, d), const),
        ] + specs,
        out_specs=pl.BlockSpec((TM_FFN, d), row),
        out_shape=jax.ShapeDtypeStruct((t, d), F32),
        scratch_shapes=[pltpu.VMEM((TM_FFN, d), F32)] + scratch,
        compiler_params=_params("parallel"),
        name="mix_out_ffn",
    )(oa, ob, wa, wb, x, g, b, ffn_w_in, ffn_w_out, g_all, b_all)


def _conv_in_kernel(x_ref, w_ref, o_ref):
    xb = x_ref[...].astype(BF16)
    d = o_ref.shape[1]
    a = _dot(xb, w_ref[:, 0:d])
    gate = _dot(xb, w_ref[:, d:2 * d])
    o_ref[...] = a * jax.nn.sigmoid(gate)


def _conv_in(x, w):
    t, d = x.shape
    return pl.pallas_call(
        _conv_in_kernel,
        grid=(t // TM_CONV,),
        in_specs=[pl.BlockSpec((TM_CONV, d), lambda i: (i, 0)),
                  pl.BlockSpec((d, 2 * d), lambda i: (0, 0))],
        out_specs=pl.BlockSpec((TM_CONV, d), lambda i: (i, 0)),
        out_shape=jax.ShapeDtypeStruct((t, d), F32),
        compiler_params=_params("parallel"),
        name="conv_in",
    )(x, w)


def _conv_out_kernel(h_ref, halo_ref, dw_ref, dwb_ref, cg_ref, cb_ref, w_ref, x_ref, g_ref, b_ref,
                     wi_ref, wo_ref, g2_ref, b2_ref, o_ref, buf_ref, cv_ref, x1_ref, xb_ref, acc_ref):
    i = pl.program_id(1)
    buf_ref[0:HALO, :] = jnp.where(i > 0, halo_ref[...], 0.0)
    buf_ref[HALO:, :] = h_ref[...]
    off = HALO - (CONV_W - 1)

    def chunk(c, carry):
        r0 = pl.multiple_of(c * CONV_ROWS, CONV_ROWS)
        nwin = CONV_ROWS + HALO
        for lt in range(dwb_ref.shape[1] // LANES):
            cols = slice(lt * LANES, (lt + 1) * LANES)
            win = buf_ref[pl.ds(r0, nwin), cols]
            acc = jnp.broadcast_to(dwb_ref[:, cols], (CONV_ROWS, LANES))
            for sub in range(SUBLANES):
                shifted = pltpu.roll(win, nwin - sub, axis=0) if sub else win
                for k in range(CONV_W):
                    if (off + k) % SUBLANES == sub:
                        a0 = off + k - sub
                        acc = acc + dw_ref[k:k + 1, cols] * shifted[a0:a0 + CONV_ROWS]
            cv_ref[pl.ds(r0, CONV_ROWS), cols] = acc
        return carry

    lax.fori_loop(0, TM_CONV // CONV_ROWS, chunk, 0)
    y = _layer_norm(cv_ref[...], cg_ref[...], cb_ref[...])
    u = (y * jax.nn.sigmoid(y)).astype(BF16)
    m = _dot(u, w_ref[...])
    x1_ref[...] = _layer_norm(ALPHA * x_ref[...] + m, g_ref[...], b_ref[...])
    _ffn_kernel(x1_ref, wi_ref, wo_ref, g2_ref, b2_ref, o_ref, xb_ref, acc_ref)


def _conv_out(h, dw, dwb, cg, cb, w, x, g, b, ffn_w_in, ffn_w_out, g_all, b_all, layer, batch, seq):
    t, d = x.shape
    specs, scratch = _ffn_specs(d, ffn_w_out.shape[2], layer, 1)
    nt = seq // TM_CONV
    per = TM_CONV // HALO
    row = lambda bb, i: (bb * nt + i, 0)
    halo = lambda bb, i: (jnp.maximum((bb * nt + i) * per - 1, 0), 0)
    const = lambda bb, i: (0, 0)
    return pl.pallas_call(
        _conv_out_kernel,
        grid=(batch, nt),
        in_specs=[
            pl.BlockSpec((TM_CONV, d), row),
            pl.BlockSpec((HALO, d), halo),
            pl.BlockSpec(dw.shape, const),
            pl.BlockSpec((1, d), const),
            pl.BlockSpec((1, d), const),
            pl.BlockSpec((1, d), const),
            pl.BlockSpec((d, d), const, pipeline_mode=pl.Buffered(1)),
            pl.BlockSpec((TM_CONV, d), row),
            pl.BlockSpec((1, d), const),
            pl.BlockSpec((1, d), const),
        ] + specs,
        out_specs=pl.BlockSpec((TM_CONV, d), row),
        out_shape=jax.ShapeDtypeStruct((t, d), F32),
        scratch_shapes=[pltpu.VMEM((TM_CONV + HALO, d), F32), pltpu.VMEM((TM_CONV, d), F32),
                        pltpu.VMEM((TM_CONV, d), F32)] + scratch,
        compiler_params=_params("parallel", "arbitrary"),
        name="conv_out_ffn",
    )(h, h, dw, dwb, cg, cb, w, x, g, b, ffn_w_in, ffn_w_out, g_all, b_all)


def _attention_mixer(x, w_in, ikg, ikb, lam, subg, w_out, bias, g, b, ffn_tail, batch, seq, layer_idx):
    c = [0]
    for n in (HA * DA, HA * DA, HA * DA, HI * DI, DI, HI, HB * 2 * DB, HB * 2 * DB, HB * 2 * DB):
        c.append(c[-1] + n)
    w_ik = w_in[:, c[4]:c[5]]
    w_iw = jnp.pad(w_in[:, c[5]:c[6]], ((0, 0), (0, LANES - HI)))
    w = jnp.concatenate([w_in[:, :c[4]], w_ik, w_ik, w_iw, w_in[:, c[6]:]], axis=1).astype(BF16)
    ikg2 = jnp.concatenate([ikg, ikg])[None]
    ikb2 = jnp.concatenate([ikb, ikb])[None]
    qa, ka, vat, iq, ik2, iwt, qb, kb, vbt = _mix_proj(x, w, ikg2, ikb2)
    o_a = _dsa(qa, ka, vat, iq, ik2, iwt, bias[:HA], batch, seq)
    lam_init = 0.8 - 0.6 * math.exp(-0.3 * layer_idx)
    o_b = _diff(qb, kb, vbt, lam, subg[None], bias[HA:], batch, seq, lam_init)
    wo = w_out.astype(BF16)
    return _mix_out(o_a, o_b, wo[:HA * DA], wo[HA * DA:], x, g, b, *ffn_tail, layer_idx)


def _conv_module(x, w_in, dw, dwb, cg, cb, w_out, g, b, ffn_tail, layer_idx, batch, seq):
    h = _conv_in(x, w_in.astype(BF16))
    dwp = jnp.pad(dw, ((0, HALO - CONV_W), (0, 0)))
    return _conv_out(h, dwp, dwb[None], cg[None], cb[None], w_out.astype(BF16), x, g, b,
                     *ffn_tail, layer_idx, batch, seq)


def kernel(x, ffn_in, ffn_out, ln_g, ln_b, rel_bias, mix_w_in, idx_k_g, idx_k_b, diff_lambda,
           diff_subln_g, mix_w_out, conv_w_in, conv_dw, conv_dw_b, conv_ln_g, conv_ln_b, conv_w_out):
    batch, seq, d = x.shape
    x = x.reshape(batch * seq, d)
    bias = _bias_tiles(rel_bias.T)
    ffn_in_b = ffn_in.astype(BF16)
    ffn_out_b = ffn_out.astype(BF16)
    g_all = ln_g.reshape(3 * DEPTH, 1, d)
    b_all = ln_b.reshape(3 * DEPTH, 1, d)
    ffn_tail = (ffn_in_b, ffn_out_b, g_all, b_all)
    for l in range(DEPTH):
        j = l // 2
        x = _ffn(x, ffn_in_b, ffn_out_b, g_all, b_all, l, 0)
        if l % 2 == 0:
            x = _attention_mixer(x, mix_w_in[j], idx_k_g[j], idx_k_b[j], diff_lambda[j], diff_subln_g[j],
                                 mix_w_out[j], bias, ln_g[l, 1][None], ln_b[l, 1][None], ffn_tail, batch, seq, l)
        else:
            x = _conv_module(x, conv_w_in[j], conv_dw[j], conv_dw_b[j], conv_ln_g[j], conv_ln_b[j],
                             conv_w_out[j], ln_g[l, 1][None], ln_b[l, 1][None], ffn_tail, l, batch, seq)
    return x.reshape(batch, seq, d)
```

```python
import functools
import math

import jax
import jax.numpy as jnp
from jax import lax
from jax.experimental import pallas as pl
from jax.experimental.pallas import tpu as pltpu

F32 = jnp.float32
BF16 = jnp.bfloat16

D_MODEL = 1024
DEPTH = 4
HA, DA = 8, 64
HI, DI = 8, 64
TOPK_MAX = 256
HB, DB = 4, 64
CONV_W = 31
D_FF = 2816
NUM_BUCKETS = 32
MAX_DISTANCE = 128
ALPHA = (2 * DEPTH) ** 0.25
LN_EPS = 1e-5

LANES = 128
SUBLANES = 8
BF16_ROWS = 16
VMEM_LIMIT = 56 * 1024 * 1024

TQ = 256
TK = 512
QB = TQ // LANES
KB = TK // LANES
CH = 64
TM_FFN = 512
TF_FFN = 256
TM_PROJ = TK
TM_CONV = 512
HALO = 32
CONV_ROWS = 128
NEG = -1e30
LOG2E = math.log2(math.e)
SEARCH_CAP = 1200


def _params(*sem):
    return pltpu.CompilerParams(dimension_semantics=sem, vmem_limit_bytes=VMEM_LIMIT)


def _layer_norm(v, g, b):
    mu = jnp.mean(v, axis=-1, keepdims=True)
    c = v - mu
    var = jnp.mean(c * c, axis=-1, keepdims=True)
    return c * lax.rsqrt(var + LN_EPS) * g + b


def _dot(a, b):
    return jnp.dot(a, b, preferred_element_type=F32)


def _dot_nt(a, b):
    return lax.dot_general(a, b, (((1,), (1,)), ((), ())), preferred_element_type=F32)


def _ffn_kernel(x_ref, wi_ref, wo_ref, g_ref, b_ref, o_ref, xb_ref, acc_ref):
    f = wo_ref.shape[0]
    xb_ref[...] = x_ref[...].astype(BF16)
    for c in range(f // TF_FFN):
        xb = xb_ref[...]
        a = _dot(xb, wi_ref[:, c * TF_FFN:(c + 1) * TF_FFN])
        u = _dot(xb, wi_ref[:, f + c * TF_FFN:f + (c + 1) * TF_FFN])
        h = (a * jax.nn.sigmoid(a) * u).astype(BF16)
        part = _dot(h, wo_ref[c * TF_FFN:(c + 1) * TF_FFN, :])
        if c == 0:
            acc_ref[...] = part
        else:
            acc_ref[...] += part
    y = ALPHA * x_ref[...] + 0.5 * acc_ref[...]
    o_ref[...] = _layer_norm(y, g_ref[...], b_ref[...])


def _ffn_specs(d, f, layer, half):
    once = pl.Buffered(1)
    ln = 3 * layer + 2 * half
    specs = [
        pl.BlockSpec((None, None, d, 2 * f), lambda *_: (layer, half, 0, 0), pipeline_mode=once),
        pl.BlockSpec((None, None, f, d), lambda *_: (layer, half, 0, 0), pipeline_mode=once),
        pl.BlockSpec((None, 1, d), lambda *_: (ln, 0, 0)),
        pl.BlockSpec((None, 1, d), lambda *_: (ln, 0, 0)),
    ]
    scratch = [pltpu.VMEM((TM_FFN, d), BF16), pltpu.VMEM((TM_FFN, d), F32)]
    return specs, scratch


def _ffn(x, w_in, w_out, g, b, layer, half):
    t, d = x.shape
    specs, scratch = _ffn_specs(d, w_out.shape[2], layer, half)
    return pl.pallas_call(
        _ffn_kernel,
        grid=(t // TM_FFN,),
        in_specs=[pl.BlockSpec((TM_FFN, d), lambda i: (i, 0))] + specs,
        out_specs=pl.BlockSpec((TM_FFN, d), lambda i: (i, 0)),
        out_shape=jax.ShapeDtypeStruct((t, d), F32),
        scratch_shapes=scratch,
        compiler_params=_params("parallel"),
        name="ffn",
    )(x, w_in, w_out, g, b)


_W3 = 3 * HA * DA
_C_IQ = _W3
_C_IK = _C_IQ + HI * DI
_C_IW = _C_IK + LANES
_C_B = _C_IW + LANES
_C_END = _C_B + 3 * HB * 2 * DB


def _mix_proj_kernel(x_ref, w_ref, ikg_ref, ikb_ref,
                     qa_ref, ka_ref, vat_ref, iq_ref, ik_ref, iwt_ref, qb_ref, kb_ref, vbt_ref):
    xb = x_ref[...].astype(BF16)
    hd = HA * DA
    qa_ref[...] = (_dot(xb, w_ref[:, 0:hd]) * (DA ** -0.5 * LOG2E)).astype(BF16)
    ka_ref[...] = _dot(xb, w_ref[:, hd:2 * hd]).astype(BF16)
    vat_ref[0] = _dot(xb, w_ref[:, 2 * hd:3 * hd]).T.astype(BF16)
    iq_ref[...] = (_dot(xb, w_ref[:, _C_IQ:_C_IK]) * (DI ** -0.5)).astype(BF16)
    ik2 = _dot(xb, w_ref[:, _C_IK:_C_IW])
    ik_ref[...] = _layer_norm(ik2, ikg_ref[...], ikb_ref[...]).astype(BF16)
    iw = _dot(xb, w_ref[:, _C_IW:_C_B]) * (HI ** -0.5)
    iwt_ref[...] = iw.T[0:HI, :]
    hb = HB * 2 * DB
    qb_ref[...] = (_dot(xb, w_ref[:, _C_B:_C_B + hb]) * (DB ** -0.5 * LOG2E)).astype(BF16)
    kb_ref[...] = _dot(xb, w_ref[:, _C_B + hb:_C_B + 2 * hb]).astype(BF16)
    vbt_ref[0] = _dot(xb, w_ref[:, _C_B + 2 * hb:_C_END]).T.astype(BF16)


def _mix_proj(x, w, ikg2, ikb2):
    t, d = x.shape
    nt = t // TM_PROJ
    row = lambda i: (i, 0)
    const = lambda i: (0, 0)
    tile = lambda i: (i, 0, 0)
    wide = lambda n, dt: jax.ShapeDtypeStruct((t, n), dt)
    hd, hb = HA * DA, HB * 2 * DB
    return pl.pallas_call(
        _mix_proj_kernel,
        grid=(nt,),
        in_specs=[
            pl.BlockSpec((TM_PROJ, d), row),
            pl.BlockSpec((d, _C_END), const),
            pl.BlockSpec((1, LANES), const),
            pl.BlockSpec((1, LANES), const),
        ],
        out_specs=[
            pl.BlockSpec((TM_PROJ, hd), row), pl.BlockSpec((TM_PROJ, hd), row),
            pl.BlockSpec((1, hd, TM_PROJ), tile), pl.BlockSpec((TM_PROJ, HI * DI), row),
            pl.BlockSpec((TM_PROJ, LANES), row), pl.BlockSpec((HI, TM_PROJ), lambda i: (0, i)),
            pl.BlockSpec((TM_PROJ, hb), row), pl.BlockSpec((TM_PROJ, hb), row),
            pl.BlockSpec((1, hb, TM_PROJ), tile),
        ],
        out_shape=[wide(hd, BF16), wide(hd, BF16), jax.ShapeDtypeStruct((nt, hd, TM_PROJ), BF16),
                   wide(HI * DI, BF16), wide(LANES, BF16), jax.ShapeDtypeStruct((HI, t), F32),
                   wide(hb, BF16), wide(hb, BF16), jax.ShapeDtypeStruct((nt, hb, TM_PROJ), BF16)],
        compiler_params=_params("parallel"),
        name="mix_proj",
    )(x, w, ikg2, ikb2)


def _bias_kernel(tbl_ref, o_ref):
    nh = o_ref.shape[0]
    kk = lax.broadcasted_iota(jnp.int32, (LANES, LANES), 0)
    qq = lax.broadcasted_iota(jnp.int32, (LANES, LANES), 1)
    max_exact = NUM_BUCKETS // 2
    for blk in range(2):
        n = jnp.maximum(qq - kk + blk * LANES, 0)
        nf = jnp.maximum(n, 1).astype(F32)
        large = max_exact + (jnp.log(nf / max_exact) / math.log(MAX_DISTANCE / max_exact)
                             * (NUM_BUCKETS - max_exact)).astype(jnp.int32)
        large = jnp.minimum(large, NUM_BUCKETS - 1)
        bucket = jnp.where(n < max_exact, n, large)
        for h in range(nh):
            acc = jnp.zeros((LANES, LANES), F32)
            for k in range(NUM_BUCKETS):
                acc = jnp.where(bucket == k, tbl_ref[h, k], acc)
            o_ref[h, blk] = (acc - tbl_ref[h, NUM_BUCKETS - 1]) * LOG2E


def _bias_tiles(rel_bias_t):
    nh = rel_bias_t.shape[0]
    return pl.pallas_call(
        _bias_kernel,
        in_specs=[pl.BlockSpec(memory_space=pltpu.SMEM)],
        out_shape=jax.ShapeDtypeStruct((nh, 2, LANES, LANES), F32),
        name="bias_tiles",
    )(rel_bias_t)


def _half_masked(x, upper):
    lane = lax.broadcasted_iota(jnp.int32, x.shape, 1)
    keep = (lane >= LANES // 2) if upper else (lane < LANES // 2)
    return jnp.where(keep, x, jnp.zeros_like(x))


def _near_bias_t(bias_ref, head, i, j, cc, r0):
    b0 = bias_ref[head, 0, r0:r0 + CH, :]
    b1 = bias_ref[head, 1, r0:r0 + CH, :]
    cols = []
    for qq in range(QB):
        delta = (i * QB + qq) - (j * KB + cc)
        cols.append(jnp.where(delta == 0, b0, jnp.where(delta == 1, b1, jnp.zeros_like(b0))))
    return jnp.concatenate(cols, axis=1)


def _fold8(x, op):
    return op(x.reshape(CH // SUBLANES, SUBLANES, x.shape[1]), axis=0)


def _flash_scores(par, maps, k_tile, mask_bias, near_bias, qm_ref, s_ref, m_ref, c_ref, only=None, st=None,
                  rows=TK):
    nmap = len(maps)
    cur = par * nmap
    prv = nmap - cur
    nch = LANES // CH
    for s in (range(nmap) if only is None else (only,)):
        pair, _, head = maps[s]
        if st is None or only is None:
            st = _dot_nt(k_tile(pair, rows), qm_ref[s])
        mx = jnp.full((SUBLANES, TQ), NEG, F32)
        for cc in range(rows // LANES):
            for hh in range(nch):
                r0 = cc * LANES + hh * CH
                piece = st[r0:r0 + CH]
                if mask_bias is not None:
                    piece = piece + mask_bias(head, r0)
                if near_bias is not None:
                    piece = piece + near_bias(head, cc, hh * CH)
                s_ref[cur + s, r0:r0 + CH, :] = piece
                mx = jnp.maximum(mx, _fold8(piece, jnp.max))
        m_prev = m_ref[prv + s]
        m_next = jnp.maximum(m_prev, jnp.max(mx, axis=0, keepdims=True))
        c_ref[cur + s] = jnp.exp2(m_prev - m_next)
        m_ref[cur + s] = m_next


def _flash_values(par, maps, vt_tile, s_ref, p_ref, m_ref, l_ref, c_ref, acc_ref, only=None, part=None, rows=TK):
    nmap = len(maps)
    cur = par * nmap
    for s in (range(nmap) if only is None else (only,)):
        if part != "pv":
            m_new = m_ref[cur + s]
            for r0 in range(0, rows, CH):
                p_ref[s, r0:r0 + CH, :] = jnp.exp2(s_ref[cur + s, r0:r0 + CH, :] - m_new).astype(BF16)
        if part == "probs":
            continue
        lhs = jnp.concatenate([vt_tile(maps[s][1], rows), jnp.ones((BF16_ROWS, rows), BF16)], axis=0)
        pv = _dot(lhs, p_ref[s, 0:rows, :])
        dv = acc_ref.shape[1]
        l_ref[s] = c_ref[cur + s] * l_ref[s] + pv[dv:dv + 1]
        acc_ref[s] = c_ref[cur + s] * acc_ref[s] + pv[:dv]


def _flash_sweep(jd, half_last, nmap, qk, prep_far, scores_far, prep_any, scores_any, values):
    def step(j_next, par_next, prep, scores, rows=TK):
        prep(j_next, par_next)
        st = qk(j_next, 0, rows)
        for s in range(nmap):
            values(j_next - 1, 1 - par_next, s, "probs", TK)
            st_after = qk(j_next, s + 1, rows) if s + 1 < nmap else None
            scores(j_next, par_next, s, st, rows)
            values(j_next - 1, 1 - par_next, s, "pv", TK)
            st = st_after

    prep_any(0, 0)
    for s in range(nmap):
        scores_any(0, 0, s, None, TK)

    def body(k, c):
        step(2 * k + 1, 1, prep_far, scores_far)
        step(2 * k + 2, 0, prep_far, scores_far)
        return c

    npair = jnp.maximum(jd - 2, 0) // 2
    lax.fori_loop(0, npair, body, 0)
    t = 2 * npair
    left = jd - t
    for r in range(1, 4):
        short = jnp.logical_and(left == r, half_last)

        @pl.when(jnp.logical_and(left >= r, jnp.logical_not(short)))
        def _(r=r):
            step(t + r, r % 2, prep_any, scores_any)

        @pl.when(short)
        def _(r=r):
            step(t + r, r % 2, prep_any, scores_any, TK // 2)

    for par in range(2):
        last_par = lax.rem(left, 2) == par

        @pl.when(jnp.logical_and(last_par, jnp.logical_not(half_last)))
        def _(par=par):
            values(jd, par, None, None, TK)

        @pl.when(jnp.logical_and(last_par, half_last))
        def _(par=par):
            values(jd, par, None, None, TK // 2)


def _flash_init(m_ref, l_ref, acc_ref):
    m_ref[...] = jnp.full(m_ref.shape, NEG, F32)
    l_ref[...] = jnp.zeros_like(l_ref)
    acc_ref[...] = jnp.zeros_like(acc_ref)


def _flash_scratch(nmap, dv):
    return [
        pltpu.VMEM((nmap, TQ, LANES), BF16),
        pltpu.VMEM((2 * nmap, TK, TQ), F32),
        pltpu.VMEM((nmap, TK, TQ), BF16),
        pltpu.VMEM((2 * nmap, 1, TQ), F32),
        pltpu.VMEM((nmap, 1, TQ), F32),
        pltpu.VMEM((2 * nmap, 1, TQ), F32),
        pltpu.VMEM((nmap, dv, TQ), F32),
        pltpu.VMEM((2, TK, TQ), F32),
    ]


def _dsa_kernel(q_ref, k_ref, vt_ref, iq_ref, ik_ref, iwt_ref, bias_ref, o_ref,
                sc_ref, qm_ref, s_ref, p_ref, m_ref, l_ref, c_ref, acc_ref, mb_ref, *, top_k):
    i = pl.program_id(1)
    t0 = i * TQ
    jd = t0 // TK
    ntile = jd + 1
    qpos = t0 + lax.broadcasted_iota(jnp.int32, (1, TQ), 1)

    iq = iq_ref[...]
    for h in range(HI):
        qm_ref[h] = _half_masked(iq[:, (h // 2) * LANES:(h // 2 + 1) * LANES], h % 2 == 1)
    iwt = iwt_ref[...]

    def score_tile(j, carry, diag):
        rmax, rmin = carry
        ik_t = ik_ref[pl.ds(pl.multiple_of(j * TK, TK), TK), :]
        for hp in range(HI // 2):
            da = _dot_nt(ik_t, qm_ref[2 * hp])
            db = _dot_nt(ik_t, qm_ref[2 * hp + 1])
            wa = iwt[2 * hp:2 * hp + 1, :]
            wb = iwt[2 * hp + 1:2 * hp + 2, :]
            for r0 in range(0, TK, CH):
                piece = wa * jnp.maximum(da[r0:r0 + CH], 0.0) + wb * jnp.maximum(db[r0:r0 + CH], 0.0)
                if hp > 0:
                    piece = piece + sc_ref[j, r0:r0 + CH, :]
                if hp == HI // 2 - 1:
                    if diag:
                        kpos = j * TK + r0 + lax.broadcasted_iota(jnp.int32, (CH, TQ), 0)
                        valid = kpos <= qpos
                        rmax = jnp.maximum(rmax, _fold8(jnp.where(valid, piece, -jnp.inf), jnp.max))
                        rmin = jnp.minimum(rmin, _fold8(jnp.where(valid, piece, jnp.inf), jnp.min))
                        piece = jnp.where(valid, piece, -jnp.inf)
                    else:
                        rmax = jnp.maximum(rmax, _fold8(piece, jnp.max))
                        rmin = jnp.minimum(rmin, _fold8(piece, jnp.min))
                sc_ref[j, r0:r0 + CH, :] = piece
        return rmax, rmin

    carry = lax.fori_loop(
        0, jd, lambda j, c: score_tile(j, c, False),
        (jnp.full((SUBLANES, TQ), -jnp.inf, F32), jnp.full((SUBLANES, TQ), jnp.inf, F32)))
    rmax8, rmin8 = score_tile(jd, carry, True)
    rmax = jnp.max(rmax8, axis=0, keepdims=True)
    rmin = jnp.min(rmin8, axis=0, keepdims=True)

    def row_count(pred):
        def body(j, cnt):
            for r0 in range(0, TK, CH):
                cnt = cnt + _fold8(jnp.where(pred(sc_ref[j, r0:r0 + CH, :]), 1.0, 0.0), jnp.sum)
            return cnt
        cnt = lax.fori_loop(0, ntile, body, jnp.zeros((SUBLANES, TQ), F32))
        return jnp.sum(cnt, axis=0, keepdims=True)

    def count_ge(x):
        return row_count(lambda t: t >= x)

    kf = float(top_k)
    nvalid = (qpos + 1).astype(F32)
    few = nvalid <= kf
    c_ge0 = count_ge(0.0)
    c_gt0 = row_count(lambda t: t > 0.0)
    at0 = jnp.logical_and(c_gt0 < kf, c_ge0 >= kf)
    pos = c_gt0 >= kf
    lo0 = jnp.where(few, -3.0e38, jnp.where(pos, 0.0, jnp.where(at0, 0.0, rmin)))
    hi0 = jnp.where(pos, rmax * 1.000001 + 1e-37, 0.0)
    clo0 = jnp.where(few, kf, jnp.where(jnp.logical_or(pos, at0), c_ge0, nvalid))
    chi0 = jnp.where(pos, 0.0, c_ge0)

    def settled(clo, chi):
        return jnp.logical_or(clo == kf, kf - chi == 1.0)

    done0 = jnp.where(jnp.logical_or(jnp.logical_or(few, at0), settled(clo0, chi0)), 1.0, 0.0)

    def search_cond(st):
        it, _, _, _, _, done = st
        return jnp.logical_and(it < SEARCH_CAP, jnp.min(done) < 0.5)

    def search_body(st):
        it, lo, hi, clo, chi, done = st
        mid = lo + (hi - lo) * 0.5
        stuck = jnp.logical_or(mid <= lo, mid >= hi)
        cx = count_ge(mid)
        ge = cx >= kf
        act = done < 0.5
        up = jnp.logical_and(act, ge)
        dn = jnp.logical_and(act, jnp.logical_not(ge))
        lo = jnp.where(up, mid, lo)
        clo = jnp.where(up, cx, clo)
        hi = jnp.where(dn, mid, hi)
        chi = jnp.where(dn, cx, chi)
        done = jnp.where(jnp.logical_or(stuck, settled(clo, chi)), 1.0, done)
        return it + 1, lo, hi, clo, chi, done

    _, lo, hi, clo, chi, _ = lax.while_loop(
        search_cond, search_body, (jnp.int32(0), lo0, hi0, clo0, chi0, done0))

    def below_hi_max(j, mx):
        for r0 in range(0, TK, CH):
            t = sc_ref[j, r0:r0 + CH, :]
            mx = jnp.maximum(mx, _fold8(jnp.where(t < hi, t, -jnp.inf), jnp.max))
        return mx

    top_below = jnp.max(lax.fori_loop(0, ntile, below_hi_max, jnp.full((SUBLANES, TQ), -jnp.inf, F32)),
                        axis=0, keepdims=True)
    last_one = jnp.logical_and(jnp.logical_and(jnp.logical_not(few), clo != kf), kf - chi == 1.0)
    theta = jnp.where(last_one, top_below, lo)
    c_theta = count_ge(theta)

    surplus = jnp.where(jnp.logical_and(jnp.logical_not(few), c_theta > kf), c_theta - kf, 0.0)

    @pl.when(jnp.max(surplus) > 0.5)
    def _():
        ra = lax.broadcasted_iota(jnp.int32, (TK, TK), 0)
        ca = lax.broadcasted_iota(jnp.int32, (TK, TK), 1)
        triu = jnp.where(ca >= ra, 1.0, 0.0).astype(BF16)

        def body(jj, later):
            j = ntile - 1 - jj
            cnt = jnp.zeros((SUBLANES, TQ), F32)
            for r0 in range(0, TK, CH):
                eqf = jnp.where(sc_ref[j, r0:r0 + CH, :] == theta, 1.0, 0.0)
                p_ref[0, r0:r0 + CH, :] = eqf.astype(BF16)
                cnt = cnt + _fold8(eqf, jnp.sum)
            from_end = _dot(triu, p_ref[0])
            for r0 in range(0, TK, CH):
                t = sc_ref[j, r0:r0 + CH, :]
                cut = jnp.where(from_end[r0:r0 + CH] + later <= surplus, -jnp.inf, t)
                sc_ref[j, r0:r0 + CH, :] = jnp.where(t == theta, cut, t)
            return later + jnp.sum(cnt, axis=0, keepdims=True)

        lax.fori_loop(0, ntile, body, jnp.zeros((1, TQ), F32))

    _flash_init(m_ref, l_ref, acc_ref)
    q = q_ref[...]
    for h in range(HA):
        qm_ref[h] = _half_masked(q[:, (h // 2) * LANES:(h // 2 + 1) * LANES], h % 2 == 1)
    maps = [(h // 2, h, h) for h in range(HA)]

    def prep(j, par):
        for r0 in range(0, TK, CH):
            mb_ref[par, r0:r0 + CH, :] = jnp.where(sc_ref[j, r0:r0 + CH, :] >= theta, 0.0, NEG)

    def k_tile(j):
        base = pl.multiple_of(j * TK, TK)
        return lambda p, rows: k_ref[pl.ds(base, rows), p * LANES:(p + 1) * LANES]

    def qk(j, s, rows):
        return _dot_nt(k_tile(j)(maps[s][0], rows), qm_ref[s])

    def scores(j, par, near, only, st, rows):
        _flash_scores(
            par, maps, k_tile(j),
            lambda h, r0: mb_ref[par, r0:r0 + CH, :],
            (lambda h, cc, r: _near_bias_t(bias_ref, h, i, j, cc, r)) if near else None,
            qm_ref, s_ref, m_ref, c_ref, only, st, rows)

    def values(j, par, only, part, rows):
        _flash_values(par, maps, lambda h, rows: vt_ref[j, h * DA:(h + 1) * DA, 0:rows],
                      s_ref, p_ref, m_ref, l_ref, c_ref, acc_ref, only, part, rows)

    half_last = lax.rem(i, TK // TQ) == 0
    _flash_sweep(jd, half_last, HA, qk, prep, lambda j, par, s, st, rows: scores(j, par, False, s, st, rows),
                 prep, lambda j, par, s, st, rows: scores(j, par, True, s, st, rows), values)

    for p in range(HA // 2):
        o_lo = acc_ref[2 * p] / l_ref[2 * p]
        o_hi = acc_ref[2 * p + 1] / l_ref[2 * p + 1]
        o_t = jnp.concatenate([o_lo, o_hi], axis=0)
        o_ref[:, p * LANES:(p + 1) * LANES] = o_t.T.astype(BF16)


def _dsa(q, k, vt, iq, ik2, iwt, bias, batch, seq):
    top_k = min(TOPK_MAX, seq // 4)
    nq = seq // TQ
    nt = seq // TK
    hd = HA * DA
    qrow = lambda b, i: (b * nq + i, 0)
    whole = lambda b, i: (b, 0)
    once = pl.Buffered(1)
    return pl.pallas_call(
        functools.partial(_dsa_kernel, top_k=top_k),
        grid=(batch, nq),
        in_specs=[
            pl.BlockSpec((TQ, hd), qrow),
            pl.BlockSpec((seq, hd), whole, pipeline_mode=once),
            pl.BlockSpec((nt, hd, TK), lambda b, i: (b, 0, 0), pipeline_mode=once),
            pl.BlockSpec((TQ, HI * DI), qrow),
            pl.BlockSpec((seq, LANES), whole, pipeline_mode=once),
            pl.BlockSpec((HI, TQ), lambda b, i: (0, b * nq + i)),
            pl.BlockSpec((HA, 2, LANES, LANES), lambda b, i: (0, 0, 0, 0)),
        ],
        out_specs=pl.BlockSpec((TQ, hd), qrow),
        out_shape=jax.ShapeDtypeStruct((batch * seq, hd), BF16),
        scratch_shapes=[pltpu.VMEM((nt, TK, TQ), F32)] + _flash_scratch(HA, DA),
        compiler_params=_params("parallel", "arbitrary"),
        name="dsa",
    )(q, k, vt, iq, ik2, iwt, bias)


def _diff_kernel(q_ref, k_ref, vt_ref, lam_ref, g_ref, bias_ref, o_ref,
                 qm_ref, s_ref, p_ref, m_ref, l_ref, c_ref, acc_ref, mb_ref, nb_ref, *, lam_init):
    i = pl.program_id(1)
    t0 = i * TQ
    jd = t0 // TK
    _flash_init(m_ref, l_ref, acc_ref)
    q = q_ref[...]
    nmap = 2 * HB
    for s in range(nmap):
        qm_ref[s] = _half_masked(q[:, (s // 2) * LANES:(s // 2 + 1) * LANES], s % 2 == 1)
    maps = [(s // 2, s // 2, s // 2) for s in range(nmap)]
    qpos = t0 + lax.broadcasted_iota(jnp.int32, (1, TQ), 1)

    def prep(j, par):
        for r0 in range(0, TK, CH):
            kpos = j * TK + r0 + lax.broadcasted_iota(jnp.int32, (CH, TQ), 0)
            mask = jnp.where(kpos <= qpos, 0.0, NEG)
            for h in range(HB):
                nb_ref[par * HB + h, r0:r0 + CH, :] = mask + _near_bias_t(bias_ref, h, i, j, r0 // LANES, r0 % LANES)

    def k_tile(j):
        base = pl.multiple_of(j * TK, TK)
        return lambda p, rows: k_ref[pl.ds(base, rows), p * LANES:(p + 1) * LANES]

    def qk(j, s, rows):
        return _dot_nt(k_tile(j)(maps[s][0], rows), qm_ref[s])

    def scores(j, par, general, only, st, rows):
        _flash_scores(
            par, maps, k_tile(j),
            (lambda h, r0: nb_ref[par * HB + h, r0:r0 + CH, :]) if general else None, None,
            qm_ref, s_ref, m_ref, c_ref, only, st, rows)

    def values(j, par, only, part, rows):
        _flash_values(par, maps, lambda p, rows: vt_ref[j, p * LANES:(p + 1) * LANES, 0:rows],
                      s_ref, p_ref, m_ref, l_ref, c_ref, acc_ref, only, part, rows)

    half_last = lax.rem(i, TK // TQ) == 0
    _flash_sweep(jd, half_last, nmap, qk, lambda j, par: None,
                 lambda j, par, s, st, rows: scores(j, par, False, s, st, rows),
                 prep, lambda j, par, s, st, rows: scores(j, par, True, s, st, rows), values)

    lam = lam_ref[...]
    lam_full = (jnp.exp(jnp.sum(lam[0:1] * lam[1:2], axis=1, keepdims=True))
                - jnp.exp(jnp.sum(lam[2:3] * lam[3:4], axis=1, keepdims=True)) + lam_init)
    for h in range(HB):
        o0 = acc_ref[2 * h] / l_ref[2 * h]
        o1 = acc_ref[2 * h + 1] / l_ref[2 * h + 1]
        o = (o0 - lam_full * o1).T
        y = o * lax.rsqrt(jnp.mean(o * o, axis=-1, keepdims=True) + LN_EPS) * g_ref[...]
        o_ref[:, h * LANES:(h + 1) * LANES] = (y * (1.0 - lam_init)).astype(BF16)


def _diff(q, k, vt, lam, g, bias, batch, seq, lam_init):
    nq = seq // TQ
    nt = seq // TK
    hb = HB * 2 * DB
    qrow = lambda b, i: (b * nq + i, 0)
    whole = lambda b, i: (b, 0)
    once = pl.Buffered(1)
    return pl.pallas_call(
        functools.partial(_diff_kernel, lam_init=lam_init),
        grid=(batch, nq),
        in_specs=[
            pl.BlockSpec((TQ, hb), qrow),
            pl.BlockSpec((seq, hb), whole, pipeline_mode=once),
            pl.BlockSpec((nt, hb, TK), lambda b, i: (b, 0, 0), pipeline_mode=once),
            pl.BlockSpec((4, DB), lambda b, i: (0, 0)),
            pl.BlockSpec((1, 2 * DB), lambda b, i: (0, 0)),
            pl.BlockSpec((HB, 2, LANES, LANES), lambda b, i: (0, 0, 0, 0)),
        ],
        out_specs=pl.BlockSpec((TQ, hb), qrow),
        out_shape=jax.ShapeDtypeStruct((batch * seq, hb), BF16),
        scratch_shapes=_flash_scratch(2 * HB, 2 * DB) + [pltpu.VMEM((2 * HB, TK, TQ), F32)],
        compiler_params=_params("parallel", "arbitrary"),
        name="diff",
    )(q, k, vt, lam, g, bias)


def _mix_out_kernel(oa_ref, ob_ref, wa_ref, wb_ref, x_ref, g_ref, b_ref, wi_ref, wo_ref, g2_ref, b2_ref,
                    o_ref, x1_ref, xb_ref, acc_ref):
    m = _dot(oa_ref[...], wa_ref[...]) + _dot(ob_ref[...], wb_ref[...])
    x1_ref[...] = _layer_norm(ALPHA * x_ref[...] + m, g_ref[...], b_ref[...])
    _ffn_kernel(x1_ref, wi_ref, wo_ref, g2_ref, b2_ref, o_ref, xb_ref, acc_ref)


def _mix_out(oa, ob, wa, wb, x, g, b, ffn_w_in, ffn_w_out, g_all, b_all, layer):
    t, d = x.shape
    row = lambda i: (i, 0)
    const = lambda i: (0, 0)
    once = pl.Buffered(1)
    specs, scratch = _ffn_specs(d, ffn_w_out.shape[2], layer, 1)
    return pl.pallas_call(
        _mix_out_kernel,
        grid=(t // TM_FFN,),
        in_specs=[
            pl.BlockSpec((TM_FFN, oa.shape[1]), row),
            pl.BlockSpec((TM_FFN, ob.shape[1]), row),
            pl.BlockSpec(wa.shape, const, pipeline_mode=once),
            pl.BlockSpec(wb.shape, const, pipeline_mode=once),
            pl.BlockSpec((TM_FFN, d), row),
            pl.BlockSpec((1, d), const),
            pl.BlockSpec((1, d), const),
        ] + specs,
        out_specs=pl.BlockSpec((TM_FFN, d), row),
        out_shape=jax.ShapeDtypeStruct((t, d), F32),
        scratch_shapes=[pltpu.VMEM((TM_FFN, d), F32)] + scratch,
        compiler_params=_params("parallel"),
        name="mix_out_ffn",
    )(oa, ob, wa, wb, x, g, b, ffn_w_in, ffn_w_out, g_all, b_all)


def _conv_in_kernel(x_ref, w_ref, o_ref):
    xb = x_ref[...].astype(BF16)
    d = o_ref.shape[1]
    a = _dot(xb, w_ref[:, 0:d])
    gate = _dot(xb, w_ref[:, d:2 * d])
    o_ref[...] = a * jax.nn.sigmoid(gate)


def _conv_in(x, w):
    t, d = x.shape
    return pl.pallas_call(
        _conv_in_kernel,
        grid=(t // TM_CONV,),
        in_specs=[pl.BlockSpec((TM_CONV, d), lambda i: (i, 0)),
                  pl.BlockSpec((d, 2 * d), lambda i: (0, 0))],
        out_specs=pl.BlockSpec((TM_CONV, d), lambda i: (i, 0)),
        out_shape=jax.ShapeDtypeStruct((t, d), F32),
        compiler_params=_params("parallel"),
        name="conv_in",
    )(x, w)


def _conv_out_kernel(h_ref, halo_ref, dw_ref, dwb_ref, cg_ref, cb_ref, w_ref, x_ref, g_ref, b_ref,
                     wi_ref, wo_ref, g2_ref, b2_ref, o_ref, buf_ref, cv_ref, x1_ref, xb_ref, acc_ref):
    i = pl.program_id(1)
    buf_ref[0:HALO, :] = jnp.where(i > 0, halo_ref[...], 0.0)
    buf_ref[HALO:, :] = h_ref[...]
    off = HALO - (CONV_W - 1)

    def chunk(c, carry):
        r0 = pl.multiple_of(c * CONV_ROWS, CONV_ROWS)
        nwin = CONV_ROWS + HALO
        for lt in range(dwb_ref.shape[1] // LANES):
            cols = slice(lt * LANES, (lt + 1) * LANES)
            win = buf_ref[pl.ds(r0, nwin), cols]
            acc = jnp.broadcast_to(dwb_ref[:, cols], (CONV_ROWS, LANES))
            for sub in range(SUBLANES):
                shifted = pltpu.roll(win, nwin - sub, axis=0) if sub else win
                for k in range(CONV_W):
                    if (off + k) % SUBLANES == sub:
                        a0 = off + k - sub
                        acc = acc + dw_ref[k:k + 1, cols] * shifted[a0:a0 + CONV_ROWS]
            cv_ref[pl.ds(r0, CONV_ROWS), cols] = acc
        return carry

    lax.fori_loop(0, TM_CONV // CONV_ROWS, chunk, 0)
    y = _layer_norm(cv_ref[...], cg_ref[...], cb_ref[...])
    u = (y * jax.nn.sigmoid(y)).astype(BF16)
    m = _dot(u, w_ref[...])
    x1_ref[...] = _layer_norm(ALPHA * x_ref[...] + m, g_ref[...], b_ref[...])
    _ffn_kernel(x1_ref, wi_ref, wo_ref, g2_ref, b2_ref, o_ref, xb_ref, acc_ref)


def _conv_out(h, dw, dwb, cg, cb, w, x, g, b, ffn_w_in, ffn_w_out, g_all, b_all, layer, batch, seq):
    t, d = x.shape
    specs, scratch = _ffn_specs(d, ffn_w_out.shape[2], layer, 1)
    nt = seq // TM_CONV
    per = TM_CONV // HALO
    row = lambda bb, i: (bb * nt + i, 0)
    halo = lambda bb, i: (jnp.maximum((bb * nt + i) * per - 1, 0), 0)
    const = lambda bb, i: (0, 0)
    return pl.pallas_call(
        _conv_out_kernel,
        grid=(batch, nt),
        in_specs=[
            pl.BlockSpec((TM_CONV, d), row),
            pl.BlockSpec((HALO, d), halo),
            pl.BlockSpec(dw.shape, const),
            pl.BlockSpec((1, d), const),
            pl.BlockSpec((1, d), const),
            pl.BlockSpec((1, d), const),
            pl.BlockSpec((d, d), const, pipeline_mode=pl.Buffered(1)),
            pl.BlockSpec((TM_CONV, d), row),
            pl.BlockSpec((1, d), const),
            pl.BlockSpec((1, d), const),
        ] + specs,
        out_specs=pl.BlockSpec((TM_CONV, d), row),
        out_shape=jax.ShapeDtypeStruct((t, d), F32),
        scratch_shapes=[pltpu.VMEM((TM_CONV + HALO, d), F32), pltpu.VMEM((TM_CONV, d), F32),
                        pltpu.VMEM((TM_CONV, d), F32)] + scratch,
        compiler_params=_params("parallel", "arbitrary"),
        name="conv_out_ffn",
    )(h, h, dw, dwb, cg, cb, w, x, g, b, ffn_w_in, ffn_w_out, g_all, b_all)


def _attention_mixer(x, w_in, ikg, ikb, lam, subg, w_out, bias, g, b, ffn_tail, batch, seq, layer_idx):
    c = [0]
    for n in (HA * DA, HA * DA, HA * DA, HI * DI, DI, HI, HB * 2 * DB, HB * 2 * DB, HB * 2 * DB):
        c.append(c[-1] + n)
    w_ik = w_in[:, c[4]:c[5]]
    w_iw = jnp.pad(w_in[:, c[5]:c[6]], ((0, 0), (0, LANES - HI)))
    w = jnp.concatenate([w_in[:, :c[4]], w_ik, w_ik, w_iw, w_in[:, c[6]:]], axis=1).astype(BF16)
    ikg2 = jnp.concatenate([ikg, ikg])[None]
    ikb2 = jnp.concatenate([ikb, ikb])[None]
    qa, ka, vat, iq, ik2, iwt, qb, kb, vbt = _mix_proj(x, w, ikg2, ikb2)
    o_a = _dsa(qa, ka, vat, iq, ik2, iwt, bias[:HA], batch, seq)
    lam_init = 0.8 - 0.6 * math.exp(-0.3 * layer_idx)
    o_b = _diff(qb, kb, vbt, lam, subg[None], bias[HA:], batch, seq, lam_init)
    wo = w_out.astype(BF16)
    return _mix_out(o_a, o_b, wo[:HA * DA], wo[HA * DA:], x, g, b, *ffn_tail, layer_idx)


def _conv_module(x, w_in, dw, dwb, cg, cb, w_out, g, b, ffn_tail, layer_idx, batch, seq):
    h = _conv_in(x, w_in.astype(BF16))
    dwp = jnp.pad(dw, ((0, HALO - CONV_W), (0, 0)))
    return _conv_out(h, dwp, dwb[None], cg[None], cb[None], w_out.astype(BF16), x, g, b,
                     *ffn_tail, layer_idx, batch, seq)


def kernel(x, ffn_in, ffn_out, ln_g, ln_b, rel_bias, mix_w_in, idx_k_g, idx_k_b, diff_lambda,
           diff_subln_g, mix_w_out, conv_w_in, conv_dw, conv_dw_b, conv_ln_g, conv_ln_b, conv_w_out):
    batch, seq, d = x.shape
    x = x.reshape(batch * seq, d)
    bias = _bias_tiles(rel_bias.T)
    ffn_in_b = ffn_in.astype(BF16)
    ffn_out_b = ffn_out.astype(BF16)
    g_all = ln_g.reshape(3 * DEPTH, 1, d)
    b_all = ln_b.reshape(3 * DEPTH, 1, d)
    ffn_tail = (ffn_in_b, ffn_out_b, g_all, b_all)
    for l in range(DEPTH):
        j = l // 2
        x = _ffn(x, ffn_in_b, ffn_out_b, g_all, b_all, l, 0)
        if l % 2 == 0:
            x = _attention_mixer(x, mix_w_in[j], idx_k_g[j], idx_k_b[j], diff_lambda[j], diff_subln_g[j],
                                 mix_w_out[j], bias, ln_g[l, 1][None], ln_b[l, 1][None], ffn_tail, batch, seq, l)
        else:
            x = _conv_module(x, conv_w_in[j], conv_dw[j], conv_dw_b[j], conv_ln_g[j], conv_ln_b[j],
                             conv_w_out[j], ln_g[l, 1][None], ln_b[l, 1][None], ffn_tail, l, batch, seq)
    return x.reshape(batch, seq, d)
```

```python
import functools
import math

import jax
import jax.numpy as jnp
from jax import lax
from jax.experimental import pallas as pl
from jax.experimental.pallas import tpu as pltpu

F32 = jnp.float32
BF16 = jnp.bfloat16

D_MODEL = 1024
DEPTH = 4
HA, DA = 8, 64
HI, DI = 8, 64
TOPK_MAX = 256
HB, DB = 4, 64
CONV_W = 31
D_FF = 2816
NUM_BUCKETS = 32
MAX_DISTANCE = 128
ALPHA = (2 * DEPTH) ** 0.25
LN_EPS = 1e-5

LANES = 128
SUBLANES = 8
BF16_ROWS = 16
VMEM_LIMIT = 56 * 1024 * 1024

TQ = 256
TK = 512
QB = TQ // LANES
KB = TK // LANES
CH = 64
TM_FFN = 512
TF_FFN = 256
TM_PROJ = TK
TM_CONV = 512
HALO = 32
CONV_ROWS = 128
NEG = -1e30
LOG2E = math.log2(math.e)
SEARCH_CAP = 1200


def _params(*sem):
    return pltpu.CompilerParams(dimension_semantics=sem, vmem_limit_bytes=VMEM_LIMIT)


def _layer_norm(v, g, b):
    mu = jnp.mean(v, axis=-1, keepdims=True)
    c = v - mu
    var = jnp.mean(c * c, axis=-1, keepdims=True)
    return c * lax.rsqrt(var + LN_EPS) * g + b


def _dot(a, b):
    return jnp.dot(a, b, preferred_element_type=F32)


def _dot_nt(a, b):
    return lax.dot_general(a, b, (((1,), (1,)), ((), ())), preferred_element_type=F32)


def _ffn_kernel(x_ref, wi_ref, wo_ref, g_ref, b_ref, o_ref, xb_ref, acc_ref):
    f = wo_ref.shape[0]
    xb_ref[...] = x_ref[...].astype(BF16)
    for c in range(f // TF_FFN):
        xb = xb_ref[...]
        a = _dot(xb, wi_ref[:, c * TF_FFN:(c + 1) * TF_FFN])
        u = _dot(xb, wi_ref[:, f + c * TF_FFN:f + (c + 1) * TF_FFN])
        h = (a * jax.nn.sigmoid(a) * u).astype(BF16)
        part = _dot(h, wo_ref[c * TF_FFN:(c + 1) * TF_FFN, :])
        if c == 0:
            acc_ref[...] = part
        else:
            acc_ref[...] += part
    y = ALPHA * x_ref[...] + 0.5 * acc_ref[...]
    o_ref[...] = _layer_norm(y, g_ref[...], b_ref[...])


def _ffn_specs(d, f, layer, half):
    once = pl.Buffered(1)
    ln = 3 * layer + 2 * half
    specs = [
        pl.BlockSpec((None, None, d, 2 * f), lambda *_: (layer, half, 0, 0), pipeline_mode=once),
        pl.BlockSpec((None, None, f, d), lambda *_: (layer, half, 0, 0), pipeline_mode=once),
        pl.BlockSpec((None, 1, d), lambda *_: (ln, 0, 0)),
        pl.BlockSpec((None, 1, d), lambda *_: (ln, 0, 0)),
    ]
    scratch = [pltpu.VMEM((TM_FFN, d), BF16), pltpu.VMEM((TM_FFN, d), F32)]
    return specs, scratch


def _ffn(x, w_in, w_out, g, b, layer, half):
    t, d = x.shape
    specs, scratch = _ffn_specs(d, w_out.shape[2], layer, half)
    return pl.pallas_call(
        _ffn_kernel,
        grid=(t // TM_FFN,),
        in_specs=[pl.BlockSpec((TM_FFN, d), lambda i: (i, 0))] + specs,
        out_specs=pl.BlockSpec((TM_FFN, d), lambda i: (i, 0)),
        out_shape=jax.ShapeDtypeStruct((t, d), F32),
        scratch_shapes=scratch,
        compiler_params=_params("parallel"),
        name="ffn",
    )(x, w_in, w_out, g, b)


_W3 = 3 * HA * DA
_C_IQ = _W3
_C_IK = _C_IQ + HI * DI
_C_IW = _C_IK + LANES
_C_B = _C_IW + LANES
_C_END = _C_B + 3 * HB * 2 * DB


def _mix_proj_kernel(x_ref, w_ref, ikg_ref, ikb_ref,
                     qa_ref, ka_ref, vat_ref, iq_ref, ik_ref, iwt_ref, qb_ref, kb_ref, vbt_ref):
    xb = x_ref[...].astype(BF16)
    hd = HA * DA
    qa_ref[...] = (_dot(xb, w_ref[:, 0:hd]) * (DA ** -0.5 * LOG2E)).astype(BF16)
    ka_ref[...] = _dot(xb, w_ref[:, hd:2 * hd]).astype(BF16)
    vat_ref[0] = _dot(xb, w_ref[:, 2 * hd:3 * hd]).T.astype(BF16)
    iq_ref[...] = (_dot(xb, w_ref[:, _C_IQ:_C_IK]) * (DI ** -0.5)).astype(BF16)
    ik2 = _dot(xb, w_ref[:, _C_IK:_C_IW])
    ik_ref[...] = _layer_norm(ik2, ikg_ref[...], ikb_ref[...]).astype(BF16)
    iw = _dot(xb, w_ref[:, _C_IW:_C_B]) * (HI ** -0.5)
    iwt_ref[...] = iw.T[0:HI, :]
    hb = HB * 2 * DB
    qb_ref[...] = (_dot(xb, w_ref[:, _C_B:_C_B + hb]) * (DB ** -0.5 * LOG2E)).astype(BF16)
    kb_ref[...] = _dot(xb, w_ref[:, _C_B + hb:_C_B + 2 * hb]).astype(BF16)
    vbt_ref[0] = _dot(xb, w_ref[:, _C_B + 2 * hb:_C_END]).T.astype(BF16)


def _mix_proj(x, w, ikg2, ikb2):
    t, d = x.shape
    nt = t // TM_PROJ
    row = lambda i: (i, 0)
    const = lambda i: (0, 0)
    tile = lambda i: (i, 0, 0)
    wide = lambda n, dt: jax.ShapeDtypeStruct((t, n), dt)
    hd, hb = HA * DA, HB * 2 * DB
    return pl.pallas_call(
        _mix_proj_kernel,
        grid=(nt,),
        in_specs=[
            pl.BlockSpec((TM_PROJ, d), row),
            pl.BlockSpec((d, _C_END), const),
            pl.BlockSpec((1, LANES), const),
            pl.BlockSpec((1, LANES), const),
        ],
        out_specs=[
            pl.BlockSpec((TM_PROJ, hd), row), pl.BlockSpec((TM_PROJ, hd), row),
            pl.BlockSpec((1, hd, TM_PROJ), tile), pl.BlockSpec((TM_PROJ, HI * DI), row),
            pl.BlockSpec((TM_PROJ, LANES), row), pl.BlockSpec((HI, TM_PROJ), lambda i: (0, i)),
            pl.BlockSpec((TM_PROJ, hb), row), pl.BlockSpec((TM_PROJ, hb), row),
            pl.BlockSpec((1, hb, TM_PROJ), tile),
        ],
        out_shape=[wide(hd, BF16), wide(hd, BF16), jax.ShapeDtypeStruct((nt, hd, TM_PROJ), BF16),
                   wide(HI * DI, BF16), wide(LANES, BF16), jax.ShapeDtypeStruct((HI, t), F32),
                   wide(hb, BF16), wide(hb, BF16), jax.ShapeDtypeStruct((nt, hb, TM_PROJ), BF16)],
        compiler_params=_params("parallel"),
        name="mix_proj",
    )(x, w, ikg2, ikb2)


def _bias_kernel(tbl_ref, o_ref):
    nh = o_ref.shape[0]
    kk = lax.broadcasted_iota(jnp.int32, (LANES, LANES), 0)
    qq = lax.broadcasted_iota(jnp.int32, (LANES, LANES), 1)
    max_exact = NUM_BUCKETS // 2
    for blk in range(2):
        n = jnp.maximum(qq - kk + blk * LANES, 0)
        nf = jnp.maximum(n, 1).astype(F32)
        large = max_exact + (jnp.log(nf / max_exact) / math.log(MAX_DISTANCE / max_exact)
                             * (NUM_BUCKETS - max_exact)).astype(jnp.int32)
        large = jnp.minimum(large, NUM_BUCKETS - 1)
        bucket = jnp.where(n < max_exact, n, large)
        for h in range(nh):
            acc = jnp.zeros((LANES, LANES), F32)
            for k in range(NUM_BUCKETS):
                acc = jnp.where(bucket == k, tbl_ref[h, k], acc)
            o_ref[h, blk] = (acc - tbl_ref[h, NUM_BUCKETS - 1]) * LOG2E


def _bias_tiles(rel_bias_t):
    nh = rel_bias_t.shape[0]
    return pl.pallas_call(
        _bias_kernel,
        in_specs=[pl.BlockSpec(memory_space=pltpu.SMEM)],
        out_shape=jax.ShapeDtypeStruct((nh, 2, LANES, LANES), F32),
        name="bias_tiles",
    )(rel_bias_t)


def _half_masked(x, upper):
    lane = lax.broadcasted_iota(jnp.int32, x.shape, 1)
    keep = (lane >= LANES // 2) if upper else (lane < LANES // 2)
    return jnp.where(keep, x, jnp.zeros_like(x))


def _near_bias_t(bias_ref, head, i, j, cc, r0):
    b0 = bias_ref[head, 0, r0:r0 + CH, :]
    b1 = bias_ref[head, 1, r0:r0 + CH, :]
    cols = []
    for qq in range(QB):
        delta = (i * QB + qq) - (j * KB + cc)
        cols.append(jnp.where(delta == 0, b0, jnp.where(delta == 1, b1, jnp.zeros_like(b0))))
    return jnp.concatenate(cols, axis=1)


def _fold8(x, op):
    return op(x.reshape(CH // SUBLANES, SUBLANES, x.shape[1]), axis=0)


def _flash_scores(par, maps, k_tile, mask_bias, near_bias, qm_ref, s_ref, m_ref, c_ref, only=None, st=None,
                  rows=TK):
    nmap = len(maps)
    cur = par * nmap
    prv = nmap - cur
    nch = LANES // CH
    for s in (range(nmap) if only is None else (only,)):
        pair, _, head = maps[s]
        if st is None or only is None:
            st = _dot_nt(k_tile(pair, rows), qm_ref[s])
        mx = jnp.full((SUBLANES, TQ), NEG, F32)
        for cc in range(rows // LANES):
            for hh in range(nch):
                r0 = cc * LANES + hh * CH
                piece = st[r0:r0 + CH]
                if mask_bias is not None:
                    piece = piece + mask_bias(head, r0)
                if near_bias is not None:
                    piece = piece + near_bias(head, cc, hh * CH)
                s_ref[cur + s, r0:r0 + CH, :] = piece
                mx = jnp.maximum(mx, _fold8(piece, jnp.max))
        m_prev = m_ref[prv + s]
        m_next = jnp.maximum(m_prev, jnp.max(mx, axis=0, keepdims=True))
        c_ref[cur + s] = jnp.exp2(m_prev - m_next)
        m_ref[cur + s] = m_next


def _flash_values(par, maps, vt_tile, s_ref, p_ref, m_ref, l_ref, c_ref, acc_ref, only=None, part=None, rows=TK):
    nmap = len(maps)
    cur = par * nmap
    for s in (range(nmap) if only is None else (only,)):
        if part != "pv":
            m_new = m_ref[cur + s]
            for r0 in range(0, rows, CH):
                p_ref[s, r0:r0 + CH, :] = jnp.exp2(s_ref[cur + s, r0:r0 + CH, :] - m_new).astype(BF16)
        if part == "probs":
            continue
        lhs = jnp.concatenate([vt_tile(maps[s][1], rows), jnp.ones((BF16_ROWS, rows), BF16)], axis=0)
        pv = _dot(lhs, p_ref[s, 0:rows, :])
        dv = acc_ref.shape[1]
        l_ref[s] = c_ref[cur + s] * l_ref[s] + pv[dv:dv + 1]
        acc_ref[s] = c_ref[cur + s] * acc_ref[s] + pv[:dv]


def _flash_sweep(jd, half_last, nmap, qk, prep_far, scores_far, prep_any, scores_any, values):
    def step(j_next, par_next, prep, scores, rows=TK):
        prep(j_next, par_next)
        st = qk(j_next, 0, rows)
        for s in range(nmap):
            values(j_next - 1, 1 - par_next, s, "probs", TK)
            st_after = qk(j_next, s + 1, rows) if s + 1 < nmap else None
            scores(j_next, par_next, s, st, rows)
            values(j_next - 1, 1 - par_next, s, "pv", TK)
            st = st_after

    prep_any(0, 0)
    for s in range(nmap):
        scores_any(0, 0, s, None, TK)

    def body(k, c):
        step(2 * k + 1, 1, prep_far, scores_far)
        step(2 * k + 2, 0, prep_far, scores_far)
        return c

    npair = jnp.maximum(jd - 2, 0) // 2
    lax.fori_loop(0, npair, body, 0)
    t = 2 * npair
    left = jd - t
    for r in range(1, 4):
        short = jnp.logical_and(left == r, half_last)

        @pl.when(jnp.logical_and(left >= r, jnp.logical_not(short)))
        def _(r=r):
            step(t + r, r % 2, prep_any, scores_any)

        @pl.when(short)
        def _(r=r):
            step(t + r, r % 2, prep_any, scores_any, TK // 2)

    for par in range(2):
        last_par = lax.rem(left, 2) == par

        @pl.when(jnp.logical_and(last_par, jnp.logical_not(half_last)))
        def _(par=par):
            values(jd, par, None, None, TK)

        @pl.when(jnp.logical_and(last_par, half_last))
        def _(par=par):
            values(jd, par, None, None, TK // 2)


def _flash_init(m_ref, l_ref, acc_ref):
    m_ref[...] = jnp.full(m_ref.shape, NEG, F32)
    l_ref[...] = jnp.zeros_like(l_ref)
    acc_ref[...] = jnp.zeros_like(acc_ref)


def _flash_scratch(nmap, dv, nadd):
    return [
        pltpu.VMEM((nmap, TQ, LANES), BF16),
        pltpu.VMEM((2 * nmap, TK, TQ), F32),
        pltpu.VMEM((nmap, TK, TQ), BF16),
        pltpu.VMEM((2 * nmap, 1, TQ), F32),
        pltpu.VMEM((nmap, 1, TQ), F32),
        pltpu.VMEM((2 * nmap, 1, TQ), F32),
        pltpu.VMEM((nmap, dv, TQ), F32),
        pltpu.VMEM((nadd, TK, TQ), F32),
    ]


def _dsa_kernel(q_ref, k_ref, vt_ref, iq_ref, ik_ref, iwt_ref, bias_ref, o_ref,
                sc_ref, qm_ref, s_ref, p_ref, m_ref, l_ref, c_ref, acc_ref, mb_ref, *, top_k):
    i = pl.program_id(1)
    t0 = i * TQ
    jd = t0 // TK
    ntile = jd + 1
    qpos = t0 + lax.broadcasted_iota(jnp.int32, (1, TQ), 1)

    iq = iq_ref[...]
    for h in range(HI):
        qm_ref[h] = _half_masked(iq[:, (h // 2) * LANES:(h // 2 + 1) * LANES], h % 2 == 1)
    iwt = iwt_ref[...]

    def score_tile(j, carry, diag):
        rmax, rmin = carry
        ik_t = ik_ref[pl.ds(pl.multiple_of(j * TK, TK), TK), :]
        for hp in range(HI // 2):
            da = _dot_nt(ik_t, qm_ref[2 * hp])
            db = _dot_nt(ik_t, qm_ref[2 * hp + 1])
            wa = iwt[2 * hp:2 * hp + 1, :]
            wb = iwt[2 * hp + 1:2 * hp + 2, :]
            for r0 in range(0, TK, CH):
                piece = wa * jnp.maximum(da[r0:r0 + CH], 0.0) + wb * jnp.maximum(db[r0:r0 + CH], 0.0)
                if hp > 0:
                    piece = piece + sc_ref[j, r0:r0 + CH, :]
                if hp == HI // 2 - 1:
                    if diag:
                        kpos = j * TK + r0 + lax.broadcasted_iota(jnp.int32, (CH, TQ), 0)
                        valid = kpos <= qpos
                        rmax = jnp.maximum(rmax, _fold8(jnp.where(valid, piece, -jnp.inf), jnp.max))
                        rmin = jnp.minimum(rmin, _fold8(jnp.where(valid, piece, jnp.inf), jnp.min))
                        piece = jnp.where(valid, piece, -jnp.inf)
                    else:
                        rmax = jnp.maximum(rmax, _fold8(piece, jnp.max))
                        rmin = jnp.minimum(rmin, _fold8(piece, jnp.min))
                sc_ref[j, r0:r0 + CH, :] = piece
        return rmax, rmin

    carry = lax.fori_loop(
        0, jd, lambda j, c: score_tile(j, c, False),
        (jnp.full((SUBLANES, TQ), -jnp.inf, F32), jnp.full((SUBLANES, TQ), jnp.inf, F32)))
    rmax8, rmin8 = score_tile(jd, carry, True)
    rmax = jnp.max(rmax8, axis=0, keepdims=True)
    rmin = jnp.min(rmin8, axis=0, keepdims=True)

    def row_count(pred):
        def body(j, cnt):
            for r0 in range(0, TK, CH):
                cnt = cnt + _fold8(jnp.where(pred(sc_ref[j, r0:r0 + CH, :]), 1.0, 0.0), jnp.sum)
            return cnt
        cnt = lax.fori_loop(0, ntile, body, jnp.zeros((SUBLANES, TQ), F32))
        return jnp.sum(cnt, axis=0, keepdims=True)

    def count_ge(x):
        return row_count(lambda t: t >= x)

    kf = float(top_k)
    nvalid = (qpos + 1).astype(F32)
    few = nvalid <= kf
    def zero_probe(j, c):
        ge, gt = c
        for r0 in range(0, TK, CH):
            t = sc_ref[j, r0:r0 + CH, :]
            ge = ge + _fold8(jnp.where(t >= 0.0, 1.0, 0.0), jnp.sum)
            gt = gt + _fold8(jnp.where(t > 0.0, 1.0, 0.0), jnp.sum)
        return ge, gt

    ge8, gt8 = lax.fori_loop(0, ntile, zero_probe,
                             (jnp.zeros((SUBLANES, TQ), F32), jnp.zeros((SUBLANES, TQ), F32)))
    c_ge0 = jnp.sum(ge8, axis=0, keepdims=True)
    c_gt0 = jnp.sum(gt8, axis=0, keepdims=True)
    at0 = jnp.logical_and(c_gt0 < kf, c_ge0 >= kf)
    pos = c_gt0 >= kf
    lo0 = jnp.where(few, -3.0e38, jnp.where(pos, 0.0, jnp.where(at0, 0.0, rmin)))
    hi0 = jnp.where(pos, rmax * 1.000001 + 1e-37, 0.0)
    clo0 = jnp.where(few, kf, jnp.where(jnp.logical_or(pos, at0), c_ge0, nvalid))
    chi0 = jnp.where(pos, 0.0, c_ge0)

    def settled(clo, chi):
        return jnp.logical_or(clo == kf, kf - chi == 1.0)

    done0 = jnp.where(jnp.logical_or(jnp.logical_or(few, at0), settled(clo0, chi0)), 1.0, 0.0)

    def search_cond(st):
        it, _, _, _, _, done = st
        return jnp.logical_and(it < SEARCH_CAP, jnp.min(done) < 0.5)

    def search_body(st):
        it, lo, hi, clo, chi, done = st
        mid = lo + (hi - lo) * 0.5
        stuck = jnp.logical_or(mid <= lo, mid >= hi)
        cx = count_ge(mid)
        ge = cx >= kf
        act = done < 0.5
        up = jnp.logical_and(act, ge)
        dn = jnp.logical_and(act, jnp.logical_not(ge))
        lo = jnp.where(up, mid, lo)
        clo = jnp.where(up, cx, clo)
        hi = jnp.where(dn, mid, hi)
        chi = jnp.where(dn, cx, chi)
        done = jnp.where(jnp.logical_or(stuck, settled(clo, chi)), 1.0, done)
        return it + 1, lo, hi, clo, chi, done

    _, lo, hi, clo, chi, _ = lax.while_loop(
        search_cond, search_body, (jnp.int32(0), lo0, hi0, clo0, chi0, done0))

    def below_hi_max(j, mx):
        for r0 in range(0, TK, CH):
            t = sc_ref[j, r0:r0 + CH, :]
            mx = jnp.maximum(mx, _fold8(jnp.where(t < hi, t, -jnp.inf), jnp.max))
        return mx

    top_below = jnp.max(lax.fori_loop(0, ntile, below_hi_max, jnp.full((SUBLANES, TQ), -jnp.inf, F32)),
                        axis=0, keepdims=True)
    last_one = jnp.logical_and(jnp.logical_and(jnp.logical_not(few), clo != kf), kf - chi == 1.0)
    theta = jnp.where(last_one, top_below, lo)
    c_theta = count_ge(theta)

    surplus = jnp.where(jnp.logical_and(jnp.logical_not(few), c_theta > kf), c_theta - kf, 0.0)

    @pl.when(jnp.max(surplus) > 0.5)
    def _():
        ra = lax.broadcasted_iota(jnp.int32, (TK, TK), 0)
        ca = lax.broadcasted_iota(jnp.int32, (TK, TK), 1)
        triu = jnp.where(ca >= ra, 1.0, 0.0).astype(BF16)

        def body(jj, later):
            j = ntile - 1 - jj
            cnt = jnp.zeros((SUBLANES, TQ), F32)
            for r0 in range(0, TK, CH):
                eqf = jnp.where(sc_ref[j, r0:r0 + CH, :] == theta, 1.0, 0.0)
                p_ref[0, r0:r0 + CH, :] = eqf.astype(BF16)
                cnt = cnt + _fold8(eqf, jnp.sum)
            from_end = _dot(triu, p_ref[0])
            for r0 in range(0, TK, CH):
                t = sc_ref[j, r0:r0 + CH, :]
                cut = jnp.where(from_end[r0:r0 + CH] + later <= surplus, -jnp.inf, t)
                sc_ref[j, r0:r0 + CH, :] = jnp.where(t == theta, cut, t)
            return later + jnp.sum(cnt, axis=0, keepdims=True)

        lax.fori_loop(0, ntile, body, jnp.zeros((1, TQ), F32))

    _flash_init(m_ref, l_ref, acc_ref)
    q = q_ref[...]
    for h in range(HA):
        qm_ref[h] = _half_masked(q[:, (h // 2) * LANES:(h // 2 + 1) * LANES], h % 2 == 1)
    maps = [(h // 2, h, h) for h in range(HA)]

    def prep(j, par):
        for r0 in range(0, TK, CH):
            mb_ref[par, r0:r0 + CH, :] = jnp.where(sc_ref[j, r0:r0 + CH, :] >= theta, 0.0, NEG)

    def k_tile(j):
        base = pl.multiple_of(j * TK, TK)
        return lambda p, rows: k_ref[pl.ds(base, rows), p * LANES:(p + 1) * LANES]

    def qk(j, s, rows):
        return _dot_nt(k_tile(j)(maps[s][0], rows), qm_ref[s])

    def scores(j, par, near, only, st, rows):
        _flash_scores(
            par, maps, k_tile(j),
            lambda h, r0: mb_ref[par, r0:r0 + CH, :],
            (lambda h, cc, r: _near_bias_t(bias_ref, h, i, j, cc, r)) if near else None,
            qm_ref, s_ref, m_ref, c_ref, only, st, rows)

    def values(j, par, only, part, rows):
        _flash_values(par, maps, lambda h, rows: vt_ref[j, h * DA:(h + 1) * DA, 0:rows],
                      s_ref, p_ref, m_ref, l_ref, c_ref, acc_ref, only, part, rows)

    half_last = lax.rem(i, TK // TQ) == 0
    _flash_sweep(jd, half_last, HA, qk, prep, lambda j, par, s, st, rows: scores(j, par, False, s, st, rows),
                 prep, lambda j, par, s, st, rows: scores(j, par, True, s, st, rows), values)

    for p in range(HA // 2):
        o_lo = acc_ref[2 * p] / l_ref[2 * p]
        o_hi = acc_ref[2 * p + 1] / l_ref[2 * p + 1]
        o_t = jnp.concatenate([o_lo, o_hi], axis=0)
        o_ref[:, p * LANES:(p + 1) * LANES] = o_t.T.astype(BF16)


def _dsa(q, k, vt, iq, ik2, iwt, bias, batch, seq):
    top_k = min(TOPK_MAX, seq // 4)
    nq = seq // TQ
    nt = seq // TK
    hd = HA * DA
    qrow = lambda b, i: (b * nq + i, 0)
    whole = lambda b, i: (b, 0)
    once = pl.Buffered(1)
    return pl.pallas_call(
        functools.partial(_dsa_kernel, top_k=top_k),
        grid=(batch, nq),
        in_specs=[
            pl.BlockSpec((TQ, hd), qrow),
            pl.BlockSpec((seq, hd), whole, pipeline_mode=once),
            pl.BlockSpec((nt, hd, TK), lambda b, i: (b, 0, 0), pipeline_mode=once),
            pl.BlockSpec((TQ, HI * DI), qrow),
            pl.BlockSpec((seq, LANES), whole, pipeline_mode=once),
            pl.BlockSpec((HI, TQ), lambda b, i: (0, b * nq + i)),
            pl.BlockSpec((HA, 2, LANES, LANES), lambda b, i: (0, 0, 0, 0)),
        ],
        out_specs=pl.BlockSpec((TQ, hd), qrow),
        out_shape=jax.ShapeDtypeStruct((batch * seq, hd), BF16),
        scratch_shapes=[pltpu.VMEM((nt, TK, TQ), F32)] + _flash_scratch(HA, DA, 2),
        compiler_params=_params("parallel", "arbitrary"),
        name="dsa",
    )(q, k, vt, iq, ik2, iwt, bias)


def _diff_kernel(q_ref, k_ref, vt_ref, lam_ref, g_ref, bias_ref, o_ref,
                 qm_ref, s_ref, p_ref, m_ref, l_ref, c_ref, acc_ref, nb_ref, *, lam_init):
    i = pl.program_id(1)
    t0 = i * TQ
    jd = t0 // TK
    _flash_init(m_ref, l_ref, acc_ref)
    q = q_ref[...]
    nmap = 2 * HB
    for s in range(nmap):
        qm_ref[s] = _half_masked(q[:, (s // 2) * LANES:(s // 2 + 1) * LANES], s % 2 == 1)
    maps = [(s // 2, s // 2, s // 2) for s in range(nmap)]
    qpos = t0 + lax.broadcasted_iota(jnp.int32, (1, TQ), 1)

    def prep(j, par):
        for r0 in range(0, TK, CH):
            kpos = j * TK + r0 + lax.broadcasted_iota(jnp.int32, (CH, TQ), 0)
            mask = jnp.where(kpos <= qpos, 0.0, NEG)
            for h in range(HB):
                nb_ref[par * HB + h, r0:r0 + CH, :] = mask + _near_bias_t(bias_ref, h, i, j, r0 // LANES, r0 % LANES)

    def k_tile(j):
        base = pl.multiple_of(j * TK, TK)
        return lambda p, rows: k_ref[pl.ds(base, rows), p * LANES:(p + 1) * LANES]

    def qk(j, s, rows):
        return _dot_nt(k_tile(j)(maps[s][0], rows), qm_ref[s])

    def scores(j, par, general, only, st, rows):
        _flash_scores(
            par, maps, k_tile(j),
            (lambda h, r0: nb_ref[par * HB + h, r0:r0 + CH, :]) if general else None, None,
            qm_ref, s_ref, m_ref, c_ref, only, st, rows)

    def values(j, par, only, part, rows):
        _flash_values(par, maps, lambda p, rows: vt_ref[j, p * LANES:(p + 1) * LANES, 0:rows],
                      s_ref, p_ref, m_ref, l_ref, c_ref, acc_ref, only, part, rows)

    half_last = lax.rem(i, TK // TQ) == 0
    _flash_sweep(jd, half_last, nmap, qk, lambda j, par: None,
                 lambda j, par, s, st, rows: scores(j, par, False, s, st, rows),
                 prep, lambda j, par, s, st, rows: scores(j, par, True, s, st, rows), values)

    lam = lam_ref[...]
    lam_full = (jnp.exp(jnp.sum(lam[0:1] * lam[1:2], axis=1, keepdims=True))
                - jnp.exp(jnp.sum(lam[2:3] * lam[3:4], axis=1, keepdims=True)) + lam_init)
    for h in range(HB):
        o0 = acc_ref[2 * h] / l_ref[2 * h]
        o1 = acc_ref[2 * h + 1] / l_ref[2 * h + 1]
        o = (o0 - lam_full * o1).T
        y = o * lax.rsqrt(jnp.mean(o * o, axis=-1, keepdims=True) + LN_EPS) * g_ref[...]
        o_ref[:, h * LANES:(h + 1) * LANES] = (y * (1.0 - lam_init)).astype(BF16)


def _diff(q, k, vt, lam, g, bias, batch, seq, lam_init):
    nq = seq // TQ
    nt = seq // TK
    hb = HB * 2 * DB
    qrow = lambda b, i: (b * nq + i, 0)
    whole = lambda b, i: (b, 0)
    once = pl.Buffered(1)
    return pl.pallas_call(
        functools.partial(_diff_kernel, lam_init=lam_init),
        grid=(batch, nq),
        in_specs=[
            pl.BlockSpec((TQ, hb), qrow),
            pl.BlockSpec((seq, hb), whole, pipeline_mode=once),
            pl.BlockSpec((nt, hb, TK), lambda b, i: (b, 0, 0), pipeline_mode=once),
            pl.BlockSpec((4, DB), lambda b, i: (0, 0)),
            pl.BlockSpec((1, 2 * DB), lambda b, i: (0, 0)),
            pl.BlockSpec((HB, 2, LANES, LANES), lambda b, i: (0, 0, 0, 0)),
        ],
        out_specs=pl.BlockSpec((TQ, hb), qrow),
        out_shape=jax.ShapeDtypeStruct((batch * seq, hb), BF16),
        scratch_shapes=_flash_scratch(2 * HB, 2 * DB, 2 * HB),
        compiler_params=_params("parallel", "arbitrary"),
        name="diff",
    )(q, k, vt, lam, g, bias)


def _mix_out_kernel(oa_ref, ob_ref, wa_ref, wb_ref, x_ref, g_ref, b_ref, wi_ref, wo_ref, g2_ref, b2_ref,
                    o_ref, x1_ref, xb_ref, acc_ref):
    m = _dot(oa_ref[...], wa_ref[...]) + _dot(ob_ref[...], wb_ref[...])
    x1_ref[...] = _layer_norm(ALPHA * x_ref[...] + m, g_ref[...], b_ref[...])
    _ffn_kernel(x1_ref, wi_ref, wo_ref, g2_ref, b2_ref, o_ref, xb_ref, acc_ref)


def _mix_out(oa, ob, wa, wb, x, g, b, ffn_w_in, ffn_w_out, g_all, b_all, layer):
    t, d = x.shape
    row = lambda i: (i, 0)
    const = lambda i: (0, 0)
    once = pl.Buffered(1)
    specs, scratch = _ffn_specs(d, ffn_w_out.shape[2], layer, 1)
    return pl.pallas_call(
        _mix_out_kernel,
        grid=(t // TM_FFN,),
        in_specs=[
            pl.BlockSpec((TM_FFN, oa.shape[1]), row),
            pl.BlockSpec((TM_FFN, ob.shape[1]), row),
            pl.BlockSpec(wa.shape, const, pipeline_mode=once),
            pl.BlockSpec(wb.shape, const, pipeline_mode=once),
            pl.BlockSpec((TM_FFN, d), row),
            pl.BlockSpec((1, d), const),
            pl.BlockSpec((1, d), const),
        ] + specs,
        out_specs=pl.BlockSpec((TM_FFN, d), row),
        out_shape=jax.ShapeDtypeStruct((t, d), F32),
        scratch_shapes=[pltpu.VMEM((TM_FFN, d), F32)] + scratch,
        compiler_params=_params("parallel"),
        name="mix_out_ffn",
    )(oa, ob, wa, wb, x, g, b, ffn_w_in, ffn_w_out, g_all, b_all)


def _conv_in_kernel(x_ref, w_ref, o_ref):
    xb = x_ref[...].astype(BF16)
    d = o_ref.shape[1]
    a = _dot(xb, w_ref[:, 0:d])
    gate = _dot(xb, w_ref[:, d:2 * d])
    o_ref[...] = a * jax.nn.sigmoid(gate)


def _conv_in(x, w):
    t, d = x.shape
    return pl.pallas_call(
        _conv_in_kernel,
        grid=(t // TM_CONV,),
        in_specs=[pl.BlockSpec((TM_CONV, d), lambda i: (i, 0)),
                  pl.BlockSpec((d, 2 * d), lambda i: (0, 0))],
        out_specs=pl.BlockSpec((TM_CONV, d), lambda i: (i, 0)),
        out_shape=jax.ShapeDtypeStruct((t, d), F32),
        compiler_params=_params("parallel"),
        name="conv_in",
    )(x, w)


def _conv_out_kernel(h_ref, halo_ref, dw_ref, dwb_ref, cg_ref, cb_ref, w_ref, x_ref, g_ref, b_ref,
                     wi_ref, wo_ref, g2_ref, b2_ref, o_ref, buf_ref, cv_ref, x1_ref, xb_ref, acc_ref):
    i = pl.program_id(1)
    buf_ref[0:HALO, :] = jnp.where(i > 0, halo_ref[...], 0.0)
    buf_ref[HALO:, :] = h_ref[...]
    off = HALO - (CONV_W - 1)

    def chunk(c, carry):
        r0 = pl.multiple_of(c * CONV_ROWS, CONV_ROWS)
        nwin = CONV_ROWS + HALO
        for lt in range(dwb_ref.shape[1] // LANES):
            cols = slice(lt * LANES, (lt + 1) * LANES)
            win = buf_ref[pl.ds(r0, nwin), cols]
            acc = jnp.broadcast_to(dwb_ref[:, cols], (CONV_ROWS, LANES))
            for sub in range(SUBLANES):
                shifted = pltpu.roll(win, nwin - sub, axis=0) if sub else win
                for k in range(CONV_W):
                    if (off + k) % SUBLANES == sub:
                        a0 = off + k - sub
                        acc = acc + dw_ref[k:k + 1, cols] * shifted[a0:a0 + CONV_ROWS]
            cv_ref[pl.ds(r0, CONV_ROWS), cols] = acc
        return carry

    lax.fori_loop(0, TM_CONV // CONV_ROWS, chunk, 0)
    y = _layer_norm(cv_ref[...], cg_ref[...], cb_ref[...])
    u = (y * jax.nn.sigmoid(y)).astype(BF16)
    m = _dot(u, w_ref[...])
    x1_ref[...] = _layer_norm(ALPHA * x_ref[...] + m, g_ref[...], b_ref[...])
    _ffn_kernel(x1_ref, wi_ref, wo_ref, g2_ref, b2_ref, o_ref, xb_ref, acc_ref)


def _conv_out(h, dw, dwb, cg, cb, w, x, g, b, ffn_w_in, ffn_w_out, g_all, b_all, layer, batch, seq):
    t, d = x.shape
    specs, scratch = _ffn_specs(d, ffn_w_out.shape[2], layer, 1)
    nt = seq // TM_CONV
    per = TM_CONV // HALO
    row = lambda bb, i: (bb * nt + i, 0)
    halo = lambda bb, i: (jnp.maximum((bb * nt + i) * per - 1, 0), 0)
    const = lambda bb, i: (0, 0)
    return pl.pallas_call(
        _conv_out_kernel,
        grid=(batch, nt),
        in_specs=[
            pl.BlockSpec((TM_CONV, d), row),
            pl.BlockSpec((HALO, d), halo),
            pl.BlockSpec(dw.shape, const),
            pl.BlockSpec((1, d), const),
            pl.BlockSpec((1, d), const),
            pl.BlockSpec((1, d), const),
            pl.BlockSpec((d, d), const, pipeline_mode=pl.Buffered(1)),
            pl.BlockSpec((TM_CONV, d), row),
            pl.BlockSpec((1, d), const),
            pl.BlockSpec((1, d), const),
        ] + specs,
        out_specs=pl.BlockSpec((TM_CONV, d), row),
        out_shape=jax.ShapeDtypeStruct((t, d), F32),
        scratch_shapes=[pltpu.VMEM((TM_CONV + HALO, d), F32), pltpu.VMEM((TM_CONV, d), F32),
                        pltpu.VMEM((TM_CONV, d), F32)] + scratch,
        compiler_params=_params("parallel", "arbitrary"),
        name="conv_out_ffn",
    )(h, h, dw, dwb, cg, cb, w, x, g, b, ffn_w_in, ffn_w_out, g_all, b_all)


def _attention_mixer(x, w_in, ikg, ikb, lam, subg, w_out, bias, g, b, ffn_tail, batch, seq, layer_idx):
    c = [0]
    for n in (HA * DA, HA * DA, HA * DA, HI * DI, DI, HI, HB * 2 * DB, HB * 2 * DB, HB * 2 * DB):
        c.append(c[-1] + n)
    w_ik = w_in[:, c[4]:c[5]]
    w_iw = jnp.pad(w_in[:, c[5]:c[6]], ((0, 0), (0, LANES - HI)))
    w = jnp.concatenate([w_in[:, :c[4]], w_ik, w_ik, w_iw, w_in[:, c[6]:]], axis=1).astype(BF16)
    ikg2 = jnp.concatenate([ikg, ikg])[None]
    ikb2 = jnp.concatenate([ikb, ikb])[None]
    qa, ka, vat, iq, ik2, iwt, qb, kb, vbt = _mix_proj(x, w, ikg2, ikb2)
    o_a = _dsa(qa, ka, vat, iq, ik2, iwt, bias[:HA], batch, seq)
    lam_init = 0.8 - 0.6 * math.exp(-0.3 * layer_idx)
    o_b = _diff(qb, kb, vbt, lam, subg[None], bias[HA:], batch, seq, lam_init)
    wo = w_out.astype(BF16)
    return _mix_out(o_a, o_b, wo[:HA * DA], wo[HA * DA:], x, g, b, *ffn_tail, layer_idx)


def _conv_module(x, w_in, dw, dwb, cg, cb, w_out, g, b, ffn_tail, layer_idx, batch, seq):
    h = _conv_in(x, w_in.astype(BF16))
    dwp = jnp.pad(dw, ((0, HALO - CONV_W), (0, 0)))
    return _conv_out(h, dwp, dwb[None], cg[None], cb[None], w_out.astype(BF16), x, g, b,
                     *ffn_tail, layer_idx, batch, seq)


def kernel(x, ffn_in, ffn_out, ln_g, ln_b, rel_bias, mix_w_in, idx_k_g, idx_k_b, diff_lambda,
           diff_subln_g, mix_w_out, conv_w_in, conv_dw, conv_dw_b, conv_ln_g, conv_ln_b, conv_w_out):
    batch, seq, d = x.shape
    x = x.reshape(batch * seq, d)
    bias = _bias_tiles(rel_bias.T)
    ffn_in_b = ffn_in.astype(BF16)
    ffn_out_b = ffn_out.astype(BF16)
    g_all = ln_g.reshape(3 * DEPTH, 1, d)
    b_all = ln_b.reshape(3 * DEPTH, 1, d)
    ffn_tail = (ffn_in_b, ffn_out_b, g_all, b_all)
    for l in range(DEPTH):
        j = l // 2
        x = _ffn(x, ffn_in_b, ffn_out_b, g_all, b_all, l, 0)
        if l % 2 == 0:
            x = _attention_mixer(x, mix_w_in[j], idx_k_g[j], idx_k_b[j], diff_lambda[j], diff_subln_g[j],
                                 mix_w_out[j], bias, ln_g[l, 1][None], ln_b[l, 1][None], ffn_tail, batch, seq, l)
        else:
            x = _conv_module(x, conv_w_in[j], conv_dw[j], conv_dw_b[j], conv_ln_g[j], conv_ln_b[j],
                             conv_w_out[j], ln_g[l, 1][None], ln_b[l, 1][None], ffn_tail, l, batch, seq)
    return x.reshape(batch, seq, d)
```

```python
import functools
import math

import jax
import jax.numpy as jnp
from jax import lax
from jax.experimental import pallas as pl
from jax.experimental.pallas import tpu as pltpu

F32 = jnp.float32
BF16 = jnp.bfloat16

D_MODEL = 1024
DEPTH = 4
HA, DA = 8, 64
HI, DI = 8, 64
TOPK_MAX = 256
HB, DB = 4, 64
CONV_W = 31
D_FF = 2816
NUM_BUCKETS = 32
MAX_DISTANCE = 128
ALPHA = (2 * DEPTH) ** 0.25
LN_EPS = 1e-5

LANES = 128
SUBLANES = 8
BF16_ROWS = 16
VMEM_LIMIT = 56 * 1024 * 1024

TQ = 256
TK = 512
QB = TQ // LANES
KB = TK // LANES
CH = 64
TM_FFN = 512
TF_FFN = 256
TM_PROJ = TK
TM_CONV = 512
HALO = 32
CONV_ROWS = 128
NEG = -1e30
LOG2E = math.log2(math.e)
SEARCH_CAP = 1200


def _params(*sem):
    return pltpu.CompilerParams(dimension_semantics=sem, vmem_limit_bytes=VMEM_LIMIT)


def _layer_norm(v, g, b):
    mu = jnp.mean(v, axis=-1, keepdims=True)
    c = v - mu
    var = jnp.mean(c * c, axis=-1, keepdims=True)
    return c * lax.rsqrt(var + LN_EPS) * g + b


def _dot(a, b):
    return jnp.dot(a, b, preferred_element_type=F32)


def _dot_nt(a, b):
    return lax.dot_general(a, b, (((1,), (1,)), ((), ())), preferred_element_type=F32)


def _ffn_kernel(x_ref, wi_ref, wo_ref, g_ref, b_ref, o_ref, xb_ref, acc_ref):
    f = wo_ref.shape[0]
    xb_ref[...] = x_ref[...].astype(BF16)
    for c in range(f // TF_FFN):
        xb = xb_ref[...]
        a = _dot(xb, wi_ref[:, c * TF_FFN:(c + 1) * TF_FFN])
        u = _dot(xb, wi_ref[:, f + c * TF_FFN:f + (c + 1) * TF_FFN])
        h = (a * jax.nn.sigmoid(a) * u).astype(BF16)
        part = _dot(h, wo_ref[c * TF_FFN:(c + 1) * TF_FFN, :])
        if c == 0:
            acc_ref[...] = part
        else:
            acc_ref[...] += part
    y = ALPHA * x_ref[...] + 0.5 * acc_ref[...]
    o_ref[...] = _layer_norm(y, g_ref[...], b_ref[...])


def _ffn_specs(d, f, layer, half):
    once = pl.Buffered(1)
    ln = 3 * layer + 2 * half
    specs = [
        pl.BlockSpec((None, None, d, 2 * f), lambda *_: (layer, half, 0, 0), pipeline_mode=once),
        pl.BlockSpec((None, None, f, d), lambda *_: (layer, half, 0, 0), pipeline_mode=once),
        pl.BlockSpec((None, 1, d), lambda *_: (ln, 0, 0)),
        pl.BlockSpec((None, 1, d), lambda *_: (ln, 0, 0)),
    ]
    scratch = [pltpu.VMEM((TM_FFN, d), BF16), pltpu.VMEM((TM_FFN, d), F32)]
    return specs, scratch


def _ffn(x, w_in, w_out, g, b, layer, half):
    t, d = x.shape
    specs, scratch = _ffn_specs(d, w_out.shape[2], layer, half)
    return pl.pallas_call(
        _ffn_kernel,
        grid=(t // TM_FFN,),
        in_specs=[pl.BlockSpec((TM_FFN, d), lambda i: (i, 0))] + specs,
        out_specs=pl.BlockSpec((TM_FFN, d), lambda i: (i, 0)),
        out_shape=jax.ShapeDtypeStruct((t, d), F32),
        scratch_shapes=scratch,
        compiler_params=_params("parallel"),
        name="ffn",
    )(x, w_in, w_out, g, b)


_W3 = 3 * HA * DA
_C_IQ = _W3
_C_IK = _C_IQ + HI * DI
_C_IW = _C_IK + LANES
_C_B = _C_IW + LANES
_C_END = _C_B + 3 * HB * 2 * DB


def _mix_proj_kernel(x_ref, w_ref, ikg_ref, ikb_ref,
                     qa_ref, ka_ref, vat_ref, iq_ref, ik_ref, iwt_ref, qb_ref, kb_ref, vbt_ref):
    xb = x_ref[...].astype(BF16)
    hd = HA * DA
    qa_ref[...] = (_dot(xb, w_ref[:, 0:hd]) * (DA ** -0.5 * LOG2E)).astype(BF16)
    ka_ref[...] = _dot(xb, w_ref[:, hd:2 * hd]).astype(BF16)
    vat_ref[0] = _dot(xb, w_ref[:, 2 * hd:3 * hd]).T.astype(BF16)
    iq_ref[...] = (_dot(xb, w_ref[:, _C_IQ:_C_IK]) * (DI ** -0.5)).astype(BF16)
    ik2 = _dot(xb, w_ref[:, _C_IK:_C_IW])
    ik_ref[...] = _layer_norm(ik2, ikg_ref[...], ikb_ref[...]).astype(BF16)
    iw = _dot(xb, w_ref[:, _C_IW:_C_B]) * (HI ** -0.5)
    iwt_ref[...] = iw.T[0:HI, :]
    hb = HB * 2 * DB
    qb_ref[...] = (_dot(xb, w_ref[:, _C_B:_C_B + hb]) * (DB ** -0.5 * LOG2E)).astype(BF16)
    kb_ref[...] = _dot(xb, w_ref[:, _C_B + hb:_C_B + 2 * hb]).astype(BF16)
    vbt_ref[0] = _dot(xb, w_ref[:, _C_B + 2 * hb:_C_END]).T.astype(BF16)


def _mix_proj(x, w, ikg2, ikb2):
    t, d = x.shape
    nt = t // TM_PROJ
    row = lambda i: (i, 0)
    const = lambda i: (0, 0)
    tile = lambda i: (i, 0, 0)
    wide = lambda n, dt: jax.ShapeDtypeStruct((t, n), dt)
    hd, hb = HA * DA, HB * 2 * DB
    return pl.pallas_call(
        _mix_proj_kernel,
        grid=(nt,),
        in_specs=[
            pl.BlockSpec((TM_PROJ, d), row),
            pl.BlockSpec((d, _C_END), const),
            pl.BlockSpec((1, LANES), const),
            pl.BlockSpec((1, LANES), const),
        ],
        out_specs=[
            pl.BlockSpec((TM_PROJ, hd), row), pl.BlockSpec((TM_PROJ, hd), row),
            pl.BlockSpec((1, hd, TM_PROJ), tile), pl.BlockSpec((TM_PROJ, HI * DI), row),
            pl.BlockSpec((TM_PROJ, LANES), row), pl.BlockSpec((HI, TM_PROJ), lambda i: (0, i)),
            pl.BlockSpec((TM_PROJ, hb), row), pl.BlockSpec((TM_PROJ, hb), row),
            pl.BlockSpec((1, hb, TM_PROJ), tile),
        ],
        out_shape=[wide(hd, BF16), wide(hd, BF16), jax.ShapeDtypeStruct((nt, hd, TM_PROJ), BF16),
                   wide(HI * DI, BF16), wide(LANES, BF16), jax.ShapeDtypeStruct((HI, t), F32),
                   wide(hb, BF16), wide(hb, BF16), jax.ShapeDtypeStruct((nt, hb, TM_PROJ), BF16)],
        compiler_params=_params("parallel"),
        name="mix_proj",
    )(x, w, ikg2, ikb2)


def _bias_kernel(tbl_ref, o_ref):
    nh = o_ref.shape[0]
    kk = lax.broadcasted_iota(jnp.int32, (LANES, LANES), 0)
    qq = lax.broadcasted_iota(jnp.int32, (LANES, LANES), 1)
    max_exact = NUM_BUCKETS // 2
    for blk in range(2):
        n = jnp.maximum(qq - kk + blk * LANES, 0)
        nf = jnp.maximum(n, 1).astype(F32)
        large = max_exact + (jnp.log(nf / max_exact) / math.log(MAX_DISTANCE / max_exact)
                             * (NUM_BUCKETS - max_exact)).astype(jnp.int32)
        large = jnp.minimum(large, NUM_BUCKETS - 1)
        bucket = jnp.where(n < max_exact, n, large)
        for h in range(nh):
            acc = jnp.zeros((LANES, LANES), F32)
            for k in range(NUM_BUCKETS):
                acc = jnp.where(bucket == k, tbl_ref[h, k], acc)
            o_ref[h, blk] = (acc - tbl_ref[h, NUM_BUCKETS - 1]) * LOG2E


def _bias_tiles(rel_bias_t):
    nh = rel_bias_t.shape[0]
    return pl.pallas_call(
        _bias_kernel,
        in_specs=[pl.BlockSpec(memory_space=pltpu.SMEM)],
        out_shape=jax.ShapeDtypeStruct((nh, 2, LANES, LANES), F32),
        name="bias_tiles",
    )(rel_bias_t)


def _half_masked(x, upper):
    lane = lax.broadcasted_iota(jnp.int32, x.shape, 1)
    keep = (lane >= LANES // 2) if upper else (lane < LANES // 2)
    return jnp.where(keep, x, jnp.zeros_like(x))


def _near_bias_t(bias_ref, head, i, j, cc, r0):
    b0 = bias_ref[head, 0, r0:r0 + CH, :]
    b1 = bias_ref[head, 1, r0:r0 + CH, :]
    cols = []
    for qq in range(QB):
        delta = (i * QB + qq) - (j * KB + cc)
        cols.append(jnp.where(delta == 0, b0, jnp.where(delta == 1, b1, jnp.zeros_like(b0))))
    return jnp.concatenate(cols, axis=1)


def _fold8(x, op):
    return op(x.reshape(CH // SUBLANES, SUBLANES, x.shape[1]), axis=0)


def _flash_scores(par, maps, k_tile, mask_bias, near_bias, qm_ref, s_ref, m_ref, c_ref, only=None, st=None,
                  rows=TK):
    nmap = len(maps)
    cur = par * nmap
    prv = nmap - cur
    nch = LANES // CH
    for s in (range(nmap) if only is None else (only,)):
        pair, _, head = maps[s]
        if st is None or only is None:
            st = _dot_nt(k_tile(pair, rows), qm_ref[s])
        mx = jnp.full((SUBLANES, TQ), NEG, F32)
        for cc in range(rows // LANES):
            for hh in range(nch):
                r0 = cc * LANES + hh * CH
                piece = st[r0:r0 + CH]
                if mask_bias is not None:
                    piece = piece + mask_bias(head, r0)
                if near_bias is not None:
                    piece = piece + near_bias(head, cc, hh * CH)
                s_ref[cur + s, r0:r0 + CH, :] = piece
                mx = jnp.maximum(mx, _fold8(piece, jnp.max))
        m_prev = m_ref[prv + s]
        m_next = jnp.maximum(m_prev, jnp.max(mx, axis=0, keepdims=True))
        c_ref[cur + s] = jnp.exp2(m_prev - m_next)
        m_ref[cur + s] = m_next


def _flash_values(par, maps, vt_tile, s_ref, p_ref, m_ref, l_ref, c_ref, acc_ref, only=None, part=None, rows=TK):
    nmap = len(maps)
    cur = par * nmap
    for s in (range(nmap) if only is None else (only,)):
        if part != "pv":
            m_new = m_ref[cur + s]
            for r0 in range(0, rows, CH):
                p_ref[s, r0:r0 + CH, :] = jnp.exp2(s_ref[cur + s, r0:r0 + CH, :] - m_new).astype(BF16)
        if part == "probs":
            continue
        lhs = jnp.concatenate([vt_tile(maps[s][1], rows), jnp.ones((BF16_ROWS, rows), BF16)], axis=0)
        pv = _dot(lhs, p_ref[s, 0:rows, :])
        dv = acc_ref.shape[1]
        l_ref[s] = c_ref[cur + s] * l_ref[s] + pv[dv:dv + 1]
        acc_ref[s] = c_ref[cur + s] * acc_ref[s] + pv[:dv]


def _flash_sweep(jd, half_last, nmap, qk, prep_far, scores_far, prep_any, scores_any, values):
    def step(j_next, par_next, prep, scores, rows=TK):
        prep(j_next, par_next)
        st = qk(j_next, 0, rows)
        for s in range(nmap):
            values(j_next - 1, 1 - par_next, s, "probs", TK)
            st_after = qk(j_next, s + 1, rows) if s + 1 < nmap else None
            scores(j_next, par_next, s, st, rows)
            values(j_next - 1, 1 - par_next, s, "pv", TK)
            st = st_after

    prep_any(0, 0)
    for s in range(nmap):
        scores_any(0, 0, s, None, TK)

    def body(k, c):
        step(2 * k + 1, 1, prep_far, scores_far)
        step(2 * k + 2, 0, prep_far, scores_far)
        return c

    npair = jnp.maximum(jd - 2, 0) // 2
    lax.fori_loop(0, npair, body, 0)
    t = 2 * npair
    left = jd - t
    for r in range(1, 4):
        short = jnp.logical_and(left == r, half_last)

        @pl.when(jnp.logical_and(left >= r, jnp.logical_not(short)))
        def _(r=r):
            step(t + r, r % 2, prep_any, scores_any)

        @pl.when(short)
        def _(r=r):
            step(t + r, r % 2, prep_any, scores_any, TK // 2)

    for par in range(2):
        last_par = lax.rem(left, 2) == par

        @pl.when(jnp.logical_and(last_par, jnp.logical_not(half_last)))
        def _(par=par):
            values(jd, par, None, None, TK)

        @pl.when(jnp.logical_and(last_par, half_last))
        def _(par=par):
            values(jd, par, None, None, TK // 2)


def _flash_init(m_ref, l_ref, acc_ref):
    m_ref[...] = jnp.full(m_ref.shape, NEG, F32)
    l_ref[...] = jnp.zeros_like(l_ref)
    acc_ref[...] = jnp.zeros_like(acc_ref)


def _flash_scratch(nmap, dv):
    return [
        pltpu.VMEM((nmap, TQ, LANES), BF16),
        pltpu.VMEM((2 * nmap, TK, TQ), F32),
        pltpu.VMEM((nmap, TK, TQ), BF16),
        pltpu.VMEM((2 * nmap, 1, TQ), F32),
        pltpu.VMEM((nmap, 1, TQ), F32),
        pltpu.VMEM((2 * nmap, 1, TQ), F32),
        pltpu.VMEM((nmap, dv, TQ), F32),
        pltpu.VMEM((2, TK, TQ), F32),
    ]


def _dsa_kernel(q_ref, k_ref, vt_ref, iq_ref, ik_ref, iwt_ref, bias_ref, o_ref,
                sc_ref, qm_ref, s_ref, p_ref, m_ref, l_ref, c_ref, acc_ref, mb_ref, *, top_k):
    i = pl.program_id(1)
    t0 = i * TQ
    jd = t0 // TK
    ntile = jd + 1
    qpos = t0 + lax.broadcasted_iota(jnp.int32, (1, TQ), 1)

    iq = iq_ref[...]
    for h in range(HI):
        qm_ref[h] = _half_masked(iq[:, (h // 2) * LANES:(h // 2 + 1) * LANES], h % 2 == 1)
    iwt = iwt_ref[...]

    def score_tile(j, carry, diag):
        rmax, rmin = carry
        ik_t = ik_ref[pl.ds(pl.multiple_of(j * TK, TK), TK), :]
        for hp in range(HI // 2):
            da = _dot_nt(ik_t, qm_ref[2 * hp])
            db = _dot_nt(ik_t, qm_ref[2 * hp + 1])
            wa = iwt[2 * hp:2 * hp + 1, :]
            wb = iwt[2 * hp + 1:2 * hp + 2, :]
            for r0 in range(0, TK, CH):
                piece = wa * jnp.maximum(da[r0:r0 + CH], 0.0) + wb * jnp.maximum(db[r0:r0 + CH], 0.0)
                if hp > 0:
                    piece = piece + sc_ref[j, r0:r0 + CH, :]
                if hp == HI // 2 - 1:
                    if diag:
                        kpos = j * TK + r0 + lax.broadcasted_iota(jnp.int32, (CH, TQ), 0)
                        valid = kpos <= qpos
                        rmax = jnp.maximum(rmax, _fold8(jnp.where(valid, piece, -jnp.inf), jnp.max))
                        rmin = jnp.minimum(rmin, _fold8(jnp.where(valid, piece, jnp.inf), jnp.min))
                        piece = jnp.where(valid, piece, -jnp.inf)
                    else:
                        rmax = jnp.maximum(rmax, _fold8(piece, jnp.max))
                        rmin = jnp.minimum(rmin, _fold8(piece, jnp.min))
                sc_ref[j, r0:r0 + CH, :] = piece
        return rmax, rmin

    carry = lax.fori_loop(
        0, jd, lambda j, c: score_tile(j, c, False),
        (jnp.full((SUBLANES, TQ), -jnp.inf, F32), jnp.full((SUBLANES, TQ), jnp.inf, F32)))
    rmax8, rmin8 = score_tile(jd, carry, True)
    rmax = jnp.max(rmax8, axis=0, keepdims=True)
    rmin = jnp.min(rmin8, axis=0, keepdims=True)

    def row_count(pred):
        def body(j, cnt):
            for r0 in range(0, TK, CH):
                cnt = cnt + _fold8(jnp.where(pred(sc_ref[j, r0:r0 + CH, :]), 1.0, 0.0), jnp.sum)
            return cnt
        cnt = lax.fori_loop(0, ntile, body, jnp.zeros((SUBLANES, TQ), F32))
        return jnp.sum(cnt, axis=0, keepdims=True)

    def count_ge(x):
        return row_count(lambda t: t >= x)

    kf = float(top_k)
    nvalid = (qpos + 1).astype(F32)
    few = nvalid <= kf
    c_ge0 = count_ge(0.0)
    c_gt0 = row_count(lambda t: t > 0.0)
    at0 = jnp.logical_and(c_gt0 < kf, c_ge0 >= kf)
    pos = c_gt0 >= kf
    lo0 = jnp.where(few, -3.0e38, jnp.where(pos, 0.0, jnp.where(at0, 0.0, rmin)))
    hi0 = jnp.where(pos, rmax * 1.000001 + 1e-37, 0.0)
    clo0 = jnp.where(few, kf, jnp.where(jnp.logical_or(pos, at0), c_ge0, nvalid))
    chi0 = jnp.where(pos, 0.0, c_ge0)

    def settled(clo, chi):
        return jnp.logical_or(clo == kf, kf - chi == 1.0)

    done0 = jnp.where(jnp.logical_or(jnp.logical_or(few, at0), settled(clo0, chi0)), 1.0, 0.0)

    def search_cond(st):
        it, _, _, _, _, done = st
        return jnp.logical_and(it < SEARCH_CAP, jnp.min(done) < 0.5)

    def search_body(st):
        it, lo, hi, clo, chi, done = st
        mid = lo + (hi - lo) * 0.5
        stuck = jnp.logical_or(mid <= lo, mid >= hi)
        cx = count_ge(mid)
        ge = cx >= kf
        act = done < 0.5
        up = jnp.logical_and(act, ge)
        dn = jnp.logical_and(act, jnp.logical_not(ge))
        lo = jnp.where(up, mid, lo)
        clo = jnp.where(up, cx, clo)
        hi = jnp.where(dn, mid, hi)
        chi = jnp.where(dn, cx, chi)
        done = jnp.where(jnp.logical_or(stuck, settled(clo, chi)), 1.0, done)
        return it + 1, lo, hi, clo, chi, done

    _, lo, hi, clo, chi, _ = lax.while_loop(
        search_cond, search_body, (jnp.int32(0), lo0, hi0, clo0, chi0, done0))

    def below_hi_max(j, mx):
        for r0 in range(0, TK, CH):
            t = sc_ref[j, r0:r0 + CH, :]
            mx = jnp.maximum(mx, _fold8(jnp.where(t < hi, t, -jnp.inf), jnp.max))
        return mx

    top_below = jnp.max(lax.fori_loop(0, ntile, below_hi_max, jnp.full((SUBLANES, TQ), -jnp.inf, F32)),
                        axis=0, keepdims=True)
    last_one = jnp.logical_and(jnp.logical_and(jnp.logical_not(few), clo != kf), kf - chi == 1.0)
    theta = jnp.where(last_one, top_below, lo)
    c_theta = count_ge(theta)

    surplus = jnp.where(jnp.logical_and(jnp.logical_not(few), c_theta > kf), c_theta - kf, 0.0)

    @pl.when(jnp.max(surplus) > 0.5)
    def _():
        ra = lax.broadcasted_iota(jnp.int32, (TK, TK), 0)
        ca = lax.broadcasted_iota(jnp.int32, (TK, TK), 1)
        triu = jnp.where(ca >= ra, 1.0, 0.0).astype(BF16)

        def body(jj, later):
            j = ntile - 1 - jj
            cnt = jnp.zeros((SUBLANES, TQ), F32)
            for r0 in range(0, TK, CH):
                eqf = jnp.where(sc_ref[j, r0:r0 + CH, :] == theta, 1.0, 0.0)
                p_ref[0, r0:r0 + CH, :] = eqf.astype(BF16)
                cnt = cnt + _fold8(eqf, jnp.sum)
            from_end = _dot(triu, p_ref[0])
            for r0 in range(0, TK, CH):
                t = sc_ref[j, r0:r0 + CH, :]
                cut = jnp.where(from_end[r0:r0 + CH] + later <= surplus, -jnp.inf, t)
                sc_ref[j, r0:r0 + CH, :] = jnp.where(t == theta, cut, t)
            return later + jnp.sum(cnt, axis=0, keepdims=True)

        lax.fori_loop(0, ntile, body, jnp.zeros((1, TQ), F32))

    _flash_init(m_ref, l_ref, acc_ref)
    q = q_ref[...]
    for h in range(HA):
        qm_ref[h] = _half_masked(q[:, (h // 2) * LANES:(h // 2 + 1) * LANES], h % 2 == 1)
    maps = [(h // 2, h, h) for h in range(HA)]

    def prep(j, par):
        for r0 in range(0, TK, CH):
            mb_ref[par, r0:r0 + CH, :] = jnp.where(sc_ref[j, r0:r0 + CH, :] >= theta, 0.0, NEG)

    def k_tile(j):
        base = pl.multiple_of(j * TK, TK)
        return lambda p, rows: k_ref[pl.ds(base, rows), p * LANES:(p + 1) * LANES]

    def qk(j, s, rows):
        return _dot_nt(k_tile(j)(maps[s][0], rows), qm_ref[s])

    def scores(j, par, near, only, st, rows):
        _flash_scores(
            par, maps, k_tile(j),
            lambda h, r0: mb_ref[par, r0:r0 + CH, :],
            (lambda h, cc, r: _near_bias_t(bias_ref, h, i, j, cc, r)) if near else None,
            qm_ref, s_ref, m_ref, c_ref, only, st, rows)

    def values(j, par, only, part, rows):
        _flash_values(par, maps, lambda h, rows: vt_ref[j, h * DA:(h + 1) * DA, 0:rows],
                      s_ref, p_ref, m_ref, l_ref, c_ref, acc_ref, only, part, rows)

    half_last = lax.rem(i, TK // TQ) == 0
    _flash_sweep(jd, half_last, HA, qk, prep, lambda j, par, s, st, rows: scores(j, par, False, s, st, rows),
                 prep, lambda j, par, s, st, rows: scores(j, par, True, s, st, rows), values)

    for p in range(HA // 2):
        o_lo = acc_ref[2 * p] / l_ref[2 * p]
        o_hi = acc_ref[2 * p + 1] / l_ref[2 * p + 1]
        o_t = jnp.concatenate([o_lo, o_hi], axis=0)
        o_ref[:, p * LANES:(p + 1) * LANES] = o_t.T.astype(BF16)


def _dsa(q, k, vt, iq, ik2, iwt, bias, batch, seq):
    top_k = min(TOPK_MAX, seq // 4)
    nq = seq // TQ
    nt = seq // TK
    hd = HA * DA
    qrow = lambda b, i: (b * nq + i, 0)
    whole = lambda b, i: (b, 0)
    once = pl.Buffered(1)
    return pl.pallas_call(
        functools.partial(_dsa_kernel, top_k=top_k),
        grid=(batch, nq),
        in_specs=[
            pl.BlockSpec((TQ, hd), qrow),
            pl.BlockSpec((seq, hd), whole, pipeline_mode=once),
            pl.BlockSpec((nt, hd, TK), lambda b, i: (b, 0, 0), pipeline_mode=once),
            pl.BlockSpec((TQ, HI * DI), qrow),
            pl.BlockSpec((seq, LANES), whole, pipeline_mode=once),
            pl.BlockSpec((HI, TQ), lambda b, i: (0, b * nq + i)),
            pl.BlockSpec((HA, 2, LANES, LANES), lambda b, i: (0, 0, 0, 0)),
        ],
        out_specs=pl.BlockSpec((TQ, hd), qrow),
        out_shape=jax.ShapeDtypeStruct((batch * seq, hd), BF16),
        scratch_shapes=[pltpu.VMEM((nt, TK, TQ), F32)] + _flash_scratch(HA, DA),
        compiler_params=_params("parallel", "arbitrary"),
        name="dsa",
    )(q, k, vt, iq, ik2, iwt, bias)


def _diff_kernel(q_ref, k_ref, vt_ref, lam_ref, g_ref, bias_ref, o_ref,
                 qm_ref, s_ref, p_ref, m_ref, l_ref, c_ref, acc_ref, mb_ref, nb_ref, *, lam_init):
    i = pl.program_id(1)
    t0 = i * TQ
    jd = t0 // TK
    _flash_init(m_ref, l_ref, acc_ref)
    q = q_ref[...]
    nmap = 2 * HB
    for s in range(nmap):
        qm_ref[s] = _half_masked(q[:, (s // 2) * LANES:(s // 2 + 1) * LANES], s % 2 == 1)
    maps = [(s // 2, s // 2, s // 2) for s in range(nmap)]
    qpos = t0 + lax.broadcasted_iota(jnp.int32, (1, TQ), 1)

    def prep(j, par):
        for r0 in range(0, TK, CH):
            kpos = j * TK + r0 + lax.broadcasted_iota(jnp.int32, (CH, TQ), 0)
            mask = jnp.where(kpos <= qpos, 0.0, NEG)
            for h in range(HB):
                nb_ref[par * HB + h, r0:r0 + CH, :] = mask + _near_bias_t(bias_ref, h, i, j, r0 // LANES, r0 % LANES)

    def k_tile(j):
        base = pl.multiple_of(j * TK, TK)
        return lambda p, rows: k_ref[pl.ds(base, rows), p * LANES:(p + 1) * LANES]

    def qk(j, s, rows):
        return _dot_nt(k_tile(j)(maps[s][0], rows), qm_ref[s])

    def scores(j, par, general, only, st, rows):
        _flash_scores(
            par, maps, k_tile(j),
            (lambda h, r0: nb_ref[par * HB + h, r0:r0 + CH, :]) if general else None, None,
            qm_ref, s_ref, m_ref, c_ref, only, st, rows)

    def values(j, par, only, part, rows):
        _flash_values(par, maps, lambda p, rows: vt_ref[j, p * LANES:(p + 1) * LANES, 0:rows],
                      s_ref, p_ref, m_ref, l_ref, c_ref, acc_ref, only, part, rows)

    half_last = lax.rem(i, TK // TQ) == 0
    _flash_sweep(jd, half_last, nmap, qk, lambda j, par: None,
                 lambda j, par, s, st, rows: scores(j, par, False, s, st, rows),
                 prep, lambda j, par, s, st, rows: scores(j, par, True, s, st, rows), values)

    lam = lam_ref[...]
    lam_full = (jnp.exp(jnp.sum(lam[0:1] * lam[1:2], axis=1, keepdims=True))
                - jnp.exp(jnp.sum(lam[2:3] * lam[3:4], axis=1, keepdims=True)) + lam_init)
    for h in range(HB):
        o0 = acc_ref[2 * h] / l_ref[2 * h]
        o1 = acc_ref[2 * h + 1] / l_ref[2 * h + 1]
        o = (o0 - lam_full * o1).T
        y = o * lax.rsqrt(jnp.mean(o * o, axis=-1, keepdims=True) + LN_EPS) * g_ref[...]
        o_ref[:, h * LANES:(h + 1) * LANES] = (y * (1.0 - lam_init)).astype(BF16)


def _diff(q, k, vt, lam, g, bias, batch, seq, lam_init):
    nq = seq // TQ
    nt = seq // TK
    hb = HB * 2 * DB
    qrow = lambda b, i: (b * nq + i, 0)
    whole = lambda b, i: (b, 0)
    once = pl.Buffered(1)
    return pl.pallas_call(
        functools.partial(_diff_kernel, lam_init=lam_init),
        grid=(batch, nq),
        in_specs=[
            pl.BlockSpec((TQ, hb), qrow),
            pl.BlockSpec((seq, hb), whole, pipeline_mode=once),
            pl.BlockSpec((nt, hb, TK), lambda b, i: (b, 0, 0), pipeline_mode=once),
            pl.BlockSpec((4, DB), lambda b, i: (0, 0)),
            pl.BlockSpec((1, 2 * DB), lambda b, i: (0, 0)),
            pl.BlockSpec((HB, 2, LANES, LANES), lambda b, i: (0, 0, 0, 0)),
        ],
        out_specs=pl.BlockSpec((TQ, hb), qrow),
        out_shape=jax.ShapeDtypeStruct((batch * seq, hb), BF16),
        scratch_shapes=_flash_scratch(2 * HB, 2 * DB) + [pltpu.VMEM((2 * HB, TK, TQ), F32)],
        compiler_params=_params("parallel", "arbitrary"),
        name="diff",
    )(q, k, vt, lam, g, bias)


def _mix_out_kernel(oa_ref, ob_ref, wa_ref, wb_ref, x_ref, g_ref, b_ref, wi_ref, wo_ref, g2_ref, b2_ref,
                    o_ref, x1_ref, xb_ref, acc_ref):
    m = _dot(oa_ref[...], wa_ref[...]) + _dot(ob_ref[...], wb_ref[...])
    x1_ref[...] = _layer_norm(ALPHA * x_ref[...] + m, g_ref[...], b_ref[...])
    _ffn_kernel(x1_ref, wi_ref, wo_ref, g2_ref, b2_ref, o_ref, xb_ref, acc_ref)


def _mix_out(oa, ob, wa, wb, x, g, b, ffn_w_in, ffn_w_out, g_all, b_all, layer):
    t, d = x.shape
    row = lambda i: (i, 0)
    const = lambda i: (0, 0)
    once = pl.Buffered(1)
    specs, scratch = _ffn_specs(d, ffn_w_out.shape[2], layer, 1)
    return pl.pallas_call(
        _mix_out_kernel,
        grid=(t // TM_FFN,),
        in_specs=[
            pl.BlockSpec((TM_FFN, oa.shape[1]), row),
            pl.BlockSpec((TM_FFN, ob.shape[1]), row),
            pl.BlockSpec(wa.shape, const, pipeline_mode=once),
            pl.BlockSpec(wb.shape, const, pipeline_mode=once),
            pl.BlockSpec((TM_FFN, d), row),
            pl.BlockSpec((1, d), const),
            pl.BlockSpec((1, d), const),
        ] + specs,
        out_specs=pl.BlockSpec((TM_FFN, d), row),
        out_shape=jax.ShapeDtypeStruct((t, d), F32),
        scratch_shapes=[pltpu.VMEM((TM_FFN, d), F32)] + scratch,
        compiler_params=_params("parallel"),
        name="mix_out_ffn",
    )(oa, ob, wa, wb, x, g, b, ffn_w_in, ffn_w_out, g_all, b_all)


def _conv_in_kernel(x_ref, w_ref, o_ref):
    xb = x_ref[...].astype(BF16)
    d = o_ref.shape[1]
    a = _dot(xb, w_ref[:, 0:d])
    gate = _dot(xb, w_ref[:, d:2 * d])
    o_ref[...] = a * jax.nn.sigmoid(gate)


def _conv_in(x, w):
    t, d = x.shape
    return pl.pallas_call(
        _conv_in_kernel,
        grid=(t // TM_CONV,),
        in_specs=[pl.BlockSpec((TM_CONV, d), lambda i: (i, 0)),
                  pl.BlockSpec((d, 2 * d), lambda i: (0, 0))],
        out_specs=pl.BlockSpec((TM_CONV, d), lambda i: (i, 0)),
        out_shape=jax.ShapeDtypeStruct((t, d), F32),
        compiler_params=_params("parallel"),
        name="conv_in",
    )(x, w)


def _conv_out_kernel(h_ref, halo_ref, dw_ref, dwb_ref, cg_ref, cb_ref, w_ref, x_ref, g_ref, b_ref,
                     wi_ref, wo_ref, g2_ref, b2_ref, o_ref, buf_ref, cv_ref, x1_ref, xb_ref, acc_ref):
    i = pl.program_id(1)
    buf_ref[0:HALO, :] = jnp.where(i > 0, halo_ref[...], 0.0)
    buf_ref[HALO:, :] = h_ref[...]
    off = HALO - (CONV_W - 1)

    def chunk(c, carry):
        r0 = pl.multiple_of(c * CONV_ROWS, CONV_ROWS)
        nwin = CONV_ROWS + HALO
        for lt in range(dwb_ref.shape[1] // LANES):
            cols = slice(lt * LANES, (lt + 1) * LANES)
            win = buf_ref[pl.ds(r0, nwin), cols]
            acc = jnp.broadcast_to(dwb_ref[:, cols], (CONV_ROWS, LANES))
            for sub in range(SUBLANES):
                shifted = pltpu.roll(win, nwin - sub, axis=0) if sub else win
                for k in range(CONV_W):
                    if (off + k) % SUBLANES == sub:
                        a0 = off + k - sub
                        acc = acc + dw_ref[k:k + 1, cols] * shifted[a0:a0 + CONV_ROWS]
            cv_ref[pl.ds(r0, CONV_ROWS), cols] = acc
        return carry

    lax.fori_loop(0, TM_CONV // CONV_ROWS, chunk, 0)
    y = _layer_norm(cv_ref[...], cg_ref[...], cb_ref[...])
    u = (y * jax.nn.sigmoid(y)).astype(BF16)
    m = _dot(u, w_ref[...])
    x1_ref[...] = _layer_norm(ALPHA * x_ref[...] + m, g_ref[...], b_ref[...])
    _ffn_kernel(x1_ref, wi_ref, wo_ref, g2_ref, b2_ref, o_ref, xb_ref, acc_ref)


def _conv_out(h, dw, dwb, cg, cb, w, x, g, b, ffn_w_in, ffn_w_out, g_all, b_all, layer, batch, seq):
    t, d = x.shape
    specs, scratch = _ffn_specs(d, ffn_w_out.shape[2], layer, 1)
    nt = seq // TM_CONV
    per = TM_CONV // HALO
    row = lambda bb, i: (bb * nt + i, 0)
    halo = lambda bb, i: (jnp.maximum((bb * nt + i) * per - 1, 0), 0)
    const = lambda bb, i: (0, 0)
    return pl.pallas_call(
        _conv_out_kernel,
        grid=(batch, nt),
        in_specs=[
            pl.BlockSpec((TM_CONV, d), row),
            pl.BlockSpec((HALO, d), halo),
            pl.BlockSpec(dw.shape, const),
            pl.BlockSpec((1, d), const),
            pl.BlockSpec((1, d), const),
            pl.BlockSpec((1, d), const),
            pl.BlockSpec((d, d), const, pipeline_mode=pl.Buffered(1)),
            pl.BlockSpec((TM_CONV, d), row),
            pl.BlockSpec((1, d), const),
            pl.BlockSpec((1, d), const),
        ] + specs,
        out_specs=pl.BlockSpec((TM_CONV, d), row),
        out_shape=jax.ShapeDtypeStruct((t, d), F32),
        scratch_shapes=[pltpu.VMEM((TM_CONV + HALO, d), F32), pltpu.VMEM((TM_CONV, d), F32),
                        pltpu.VMEM((TM_CONV, d), F32)] + scratch,
        compiler_params=_params("parallel", "arbitrary"),
        name="conv_out_ffn",
    )(h, h, dw, dwb, cg, cb, w, x, g, b, ffn_w_in, ffn_w_out, g_all, b_all)


def _cast_kernel(w_ref, o_ref):
    o_ref[...] = w_ref[...].astype(BF16)


def _to_bf16(w):
    lead = w.shape[0] * w.shape[1]
    rows, cols = w.shape[2], w.shape[3]
    flat = w.reshape(lead, rows, cols)
    out = pl.pallas_call(
        _cast_kernel,
        grid=(lead, 2),
        in_specs=[pl.BlockSpec((None, rows // 2, cols), lambda i, j: (i, j, 0))],
        out_specs=pl.BlockSpec((None, rows // 2, cols), lambda i, j: (i, j, 0)),
        out_shape=jax.ShapeDtypeStruct(flat.shape, BF16),
        compiler_params=_params("parallel", "parallel"),
        name="to_bf16",
    )(flat)
    return out.reshape(w.shape)
def _attention_mixer(x, w_in, ikg, ikb, lam, subg, w_out, bias, g, b, ffn_tail, batch, seq, layer_idx):
    c = [0]
    for n in (HA * DA, HA * DA, HA * DA, HI * DI, DI, HI, HB * 2 * DB, HB * 2 * DB, HB * 2 * DB):
        c.append(c[-1] + n)
    w_ik = w_in[:, c[4]:c[5]]
    w_iw = jnp.pad(w_in[:, c[5]:c[6]], ((0, 0), (0, LANES - HI)))
    w = jnp.concatenate([w_in[:, :c[4]], w_ik, w_ik, w_iw, w_in[:, c[6]:]], axis=1).astype(BF16)
    ikg2 = jnp.concatenate([ikg, ikg])[None]
    ikb2 = jnp.concatenate([ikb, ikb])[None]
    qa, ka, vat, iq, ik2, iwt, qb, kb, vbt = _mix_proj(x, w, ikg2, ikb2)
    o_a = _dsa(qa, ka, vat, iq, ik2, iwt, bias[:HA], batch, seq)
    lam_init = 0.8 - 0.6 * math.exp(-0.3 * layer_idx)
    o_b = _diff(qb, kb, vbt, lam, subg[None], bias[HA:], batch, seq, lam_init)
    wo = w_out.astype(BF16)
    return _mix_out(o_a, o_b, wo[:HA * DA], wo[HA * DA:], x, g, b, *ffn_tail, layer_idx)


def _conv_module(x, w_in, dw, dwb, cg, cb, w_out, g, b, ffn_tail, layer_idx, batch, seq):
    h = _conv_in(x, w_in.astype(BF16))
    dwp = jnp.pad(dw, ((0, HALO - CONV_W), (0, 0)))
    return _conv_out(h, dwp, dwb[None], cg[None], cb[None], w_out.astype(BF16), x, g, b,
                     *ffn_tail, layer_idx, batch, seq)


def kernel(x, ffn_in, ffn_out, ln_g, ln_b, rel_bias, mix_w_in, idx_k_g, idx_k_b, diff_lambda,
           diff_subln_g, mix_w_out, conv_w_in, conv_dw, conv_dw_b, conv_ln_g, conv_ln_b, conv_w_out):
    batch, seq, d = x.shape
    x = x.reshape(batch * seq, d)
    bias = _bias_tiles(rel_bias.T)
    ffn_in_b = _to_bf16(ffn_in)
    ffn_out_b = _to_bf16(ffn_out)
    g_all = ln_g.reshape(3 * DEPTH, 1, d)
    b_all = ln_b.reshape(3 * DEPTH, 1, d)
    ffn_tail = (ffn_in_b, ffn_out_b, g_all, b_all)
    for l in range(DEPTH):
        j = l // 2
        x = _ffn(x, ffn_in_b, ffn_out_b, g_all, b_all, l, 0)
        if l % 2 == 0:
            x = _attention_mixer(x, mix_w_in[j], idx_k_g[j], idx_k_b[j], diff_lambda[j], diff_subln_g[j],
                                 mix_w_out[j], bias, ln_g[l, 1][None], ln_b[l, 1][None], ffn_tail, batch, seq, l)
        else:
            x = _conv_module(x, conv_w_in[j], conv_dw[j], conv_dw_b[j], conv_ln_g[j], conv_ln_b[j],
                             conv_w_out[j], ln_g[l, 1][None], ln_b[l, 1][None], ffn_tail, l, batch, seq)
    return x.reshape(batch, seq, d)
```

```python
import functools
import math

import jax
import jax.numpy as jnp
from jax import lax
from jax.experimental import pallas as pl
from jax.experimental.pallas import tpu as pltpu

F32 = jnp.float32
BF16 = jnp.bfloat16

D_MODEL = 1024
DEPTH = 4
HA, DA = 8, 64
HI, DI = 8, 64
TOPK_MAX = 256
HB, DB = 4, 64
CONV_W = 31
D_FF = 2816
NUM_BUCKETS = 32
MAX_DISTANCE = 128
ALPHA = (2 * DEPTH) ** 0.25
LN_EPS = 1e-5

LANES = 128
SUBLANES = 8
BF16_ROWS = 16
VMEM_LIMIT = 56 * 1024 * 1024

TQ = 256
TK = 512
QB = TQ // LANES
KB = TK // LANES
CH = 64
TM_FFN = 512
TF_FFN = 256
TM_PROJ = TK
TM_CONV = 512
HALO = 32
CONV_ROWS = 128
NEG = -1e30
LOG2E = math.log2(math.e)
SEARCH_CAP = 1200


def _params(*sem):
    return pltpu.CompilerParams(dimension_semantics=sem, vmem_limit_bytes=VMEM_LIMIT)


def _layer_norm(v, g, b):
    mu = jnp.mean(v, axis=-1, keepdims=True)
    c = v - mu
    var = jnp.mean(c * c, axis=-1, keepdims=True)
    return c * lax.rsqrt(var + LN_EPS) * g + b


def _dot(a, b):
    return jnp.dot(a, b, preferred_element_type=F32)


def _dot_nt(a, b):
    return lax.dot_general(a, b, (((1,), (1,)), ((), ())), preferred_element_type=F32)


def _ffn_kernel(x_ref, wi_ref, wo_ref, g_ref, b_ref, o_ref, xb_ref, acc_ref):
    f = wo_ref.shape[0]
    xb_ref[...] = x_ref[...].astype(BF16)
    for c in range(f // TF_FFN):
        xb = xb_ref[...]
        a = _dot(xb, wi_ref[:, c * TF_FFN:(c + 1) * TF_FFN])
        u = _dot(xb, wi_ref[:, f + c * TF_FFN:f + (c + 1) * TF_FFN])
        h = (a * jax.nn.sigmoid(a) * u).astype(BF16)
        part = _dot(h, wo_ref[c * TF_FFN:(c + 1) * TF_FFN, :])
        if c == 0:
            acc_ref[...] = part
        else:
            acc_ref[...] += part
    y = ALPHA * x_ref[...] + 0.5 * acc_ref[...]
    o_ref[...] = _layer_norm(y, g_ref[...], b_ref[...])


def _ffn_specs(d, f, layer, half):
    once = pl.Buffered(1)
    ln = 3 * layer + 2 * half
    specs = [
        pl.BlockSpec((None, None, d, 2 * f), lambda *_: (layer, half, 0, 0), pipeline_mode=once),
        pl.BlockSpec((None, None, f, d), lambda *_: (layer, half, 0, 0), pipeline_mode=once),
        pl.BlockSpec((None, 1, d), lambda *_: (ln, 0, 0)),
        pl.BlockSpec((None, 1, d), lambda *_: (ln, 0, 0)),
    ]
    scratch = [pltpu.VMEM((TM_FFN, d), BF16), pltpu.VMEM((TM_FFN, d), F32)]
    return specs, scratch


def _ffn(x, w_in, w_out, g, b, layer, half):
    t, d = x.shape
    specs, scratch = _ffn_specs(d, w_out.shape[2], layer, half)
    return pl.pallas_call(
        _ffn_kernel,
        grid=(t // TM_FFN,),
        in_specs=[pl.BlockSpec((TM_FFN, d), lambda i: (i, 0))] + specs,
        out_specs=pl.BlockSpec((TM_FFN, d), lambda i: (i, 0)),
        out_shape=jax.ShapeDtypeStruct((t, d), F32),
        scratch_shapes=scratch,
        compiler_params=_params("parallel"),
        name="ffn",
    )(x, w_in, w_out, g, b)


_W3 = 3 * HA * DA
_C_IQ = _W3
_C_IK = _C_IQ + HI * DI
_C_IW = _C_IK + LANES
_C_B = _C_IW + LANES
_C_END = _C_B + 3 * HB * 2 * DB


def _mix_proj_kernel(x_ref, w_ref, ikg_ref, ikb_ref,
                     qa_ref, ka_ref, vat_ref, iq_ref, ik_ref, iwt_ref, qb_ref, kb_ref, vbt_ref):
    xb = x_ref[...].astype(BF16)
    hd = HA * DA
    qa_ref[...] = (_dot(xb, w_ref[:, 0:hd]) * (DA ** -0.5 * LOG2E)).astype(BF16)
    ka_ref[...] = _dot(xb, w_ref[:, hd:2 * hd]).astype(BF16)
    vat_ref[0] = _dot(xb, w_ref[:, 2 * hd:3 * hd]).T.astype(BF16)
    iq_ref[...] = (_dot(xb, w_ref[:, _C_IQ:_C_IK]) * (DI ** -0.5)).astype(BF16)
    ik2 = _dot(xb, w_ref[:, _C_IK:_C_IW])
    ik_ref[...] = _layer_norm(ik2, ikg_ref[...], ikb_ref[...]).astype(BF16)
    iw = _dot(xb, w_ref[:, _C_IW:_C_B]) * (HI ** -0.5)
    iwt_ref[...] = iw.T[0:HI, :]
    hb = HB * 2 * DB
    qb_ref[...] = (_dot(xb, w_ref[:, _C_B:_C_B + hb]) * (DB ** -0.5 * LOG2E)).astype(BF16)
    kb_ref[...] = _dot(xb, w_ref[:, _C_B + hb:_C_B + 2 * hb]).astype(BF16)
    vbt_ref[0] = _dot(xb, w_ref[:, _C_B + 2 * hb:_C_END]).T.astype(BF16)


def _mix_proj(x, w, ikg2, ikb2):
    t, d = x.shape
    nt = t // TM_PROJ
    row = lambda i: (i, 0)
    const = lambda i: (0, 0)
    tile = lambda i: (i, 0, 0)
    wide = lambda n, dt: jax.ShapeDtypeStruct((t, n), dt)
    hd, hb = HA * DA, HB * 2 * DB
    return pl.pallas_call(
        _mix_proj_kernel,
        grid=(nt,),
        in_specs=[
            pl.BlockSpec((TM_PROJ, d), row),
            pl.BlockSpec((d, _C_END), const),
            pl.BlockSpec((1, LANES), const),
            pl.BlockSpec((1, LANES), const),
        ],
        out_specs=[
            pl.BlockSpec((TM_PROJ, hd), row), pl.BlockSpec((TM_PROJ, hd), row),
            pl.BlockSpec((1, hd, TM_PROJ), tile), pl.BlockSpec((TM_PROJ, HI * DI), row),
            pl.BlockSpec((TM_PROJ, LANES), row), pl.BlockSpec((HI, TM_PROJ), lambda i: (0, i)),
            pl.BlockSpec((TM_PROJ, hb), row), pl.BlockSpec((TM_PROJ, hb), row),
            pl.BlockSpec((1, hb, TM_PROJ), tile),
        ],
        out_shape=[wide(hd, BF16), wide(hd, BF16), jax.ShapeDtypeStruct((nt, hd, TM_PROJ), BF16),
                   wide(HI * DI, BF16), wide(LANES, BF16), jax.ShapeDtypeStruct((HI, t), F32),
                   wide(hb, BF16), wide(hb, BF16), jax.ShapeDtypeStruct((nt, hb, TM_PROJ), BF16)],
        compiler_params=_params("parallel"),
        name="mix_proj",
    )(x, w, ikg2, ikb2)


def _bias_kernel(tbl_ref, o_ref):
    nh = o_ref.shape[0]
    kk = lax.broadcasted_iota(jnp.int32, (LANES, LANES), 0)
    qq = lax.broadcasted_iota(jnp.int32, (LANES, LANES), 1)
    max_exact = NUM_BUCKETS // 2
    for blk in range(2):
        n = jnp.maximum(qq - kk + blk * LANES, 0)
        nf = jnp.maximum(n, 1).astype(F32)
        large = max_exact + (jnp.log(nf / max_exact) / math.log(MAX_DISTANCE / max_exact)
                             * (NUM_BUCKETS - max_exact)).astype(jnp.int32)
        large = jnp.minimum(large, NUM_BUCKETS - 1)
        bucket = jnp.where(n < max_exact, n, large)
        for h in range(nh):
            acc = jnp.zeros((LANES, LANES), F32)
            for k in range(NUM_BUCKETS):
                acc = jnp.where(bucket == k, tbl_ref[h, k], acc)
            o_ref[h, blk] = (acc - tbl_ref[h, NUM_BUCKETS - 1]) * LOG2E


def _bias_tiles(rel_bias_t):
    nh = rel_bias_t.shape[0]
    return pl.pallas_call(
        _bias_kernel,
        in_specs=[pl.BlockSpec(memory_space=pltpu.SMEM)],
        out_shape=jax.ShapeDtypeStruct((nh, 2, LANES, LANES), F32),
        name="bias_tiles",
    )(rel_bias_t)


def _half_masked(x, upper):
    lane = lax.broadcasted_iota(jnp.int32, x.shape, 1)
    keep = (lane >= LANES // 2) if upper else (lane < LANES // 2)
    return jnp.where(keep, x, jnp.zeros_like(x))


def _near_bias_t(bias_ref, head, i, j, cc, r0):
    b0 = bias_ref[head, 0, r0:r0 + CH, :]
    b1 = bias_ref[head, 1, r0:r0 + CH, :]
    cols = []
    for qq in range(QB):
        delta = (i * QB + qq) - (j * KB + cc)
        cols.append(jnp.where(delta == 0, b0, jnp.where(delta == 1, b1, jnp.zeros_like(b0))))
    return jnp.concatenate(cols, axis=1)


def _fold8(x, op):
    return op(x.reshape(CH // SUBLANES, SUBLANES, x.shape[1]), axis=0)


def _flash_scores(par, maps, k_tile, mask_bias, near_bias, qm_ref, s_ref, m_ref, c_ref, only=None, st=None,
                  rows=TK):
    nmap = len(maps)
    cur = par * nmap
    prv = nmap - cur
    nch = LANES // CH
    for s in (range(nmap) if only is None else (only,)):
        pair, _, head = maps[s]
        if st is None or only is None:
            st = _dot_nt(k_tile(pair, rows), qm_ref[s])
        mx = jnp.full((SUBLANES, TQ), NEG, F32)
        for cc in range(rows // LANES):
            for hh in range(nch):
                r0 = cc * LANES + hh * CH
                piece = st[r0:r0 + CH]
                if mask_bias is not None:
                    piece = piece + mask_bias(head, r0)
                if near_bias is not None:
                    piece = piece + near_bias(head, cc, hh * CH)
                s_ref[cur + s, r0:r0 + CH, :] = piece
                mx = jnp.maximum(mx, _fold8(piece, jnp.max))
        m_prev = m_ref[prv + s]
        m_next = jnp.maximum(m_prev, jnp.max(mx, axis=0, keepdims=True))
        c_ref[cur + s] = jnp.exp2(m_prev - m_next)
        m_ref[cur + s] = m_next


def _flash_values(par, maps, vt_tile, s_ref, p_ref, m_ref, l_ref, c_ref, acc_ref, only=None, part=None, rows=TK):
    nmap = len(maps)
    cur = par * nmap
    for s in (range(nmap) if only is None else (only,)):
        if part != "pv":
            m_new = m_ref[cur + s]
            for r0 in range(0, rows, CH):
                p_ref[s, r0:r0 + CH, :] = jnp.exp2(s_ref[cur + s, r0:r0 + CH, :] - m_new).astype(BF16)
        if part == "probs":
            continue
        lhs = jnp.concatenate([vt_tile(maps[s][1], rows), jnp.ones((BF16_ROWS, rows), BF16)], axis=0)
        pv = _dot(lhs, p_ref[s, 0:rows, :])
        dv = acc_ref.shape[1]
        l_ref[s] = c_ref[cur + s] * l_ref[s] + pv[dv:dv + 1]
        acc_ref[s] = c_ref[cur + s] * acc_ref[s] + pv[:dv]


def _flash_sweep(jd, half_last, nmap, qk, prep_far, scores_far, prep_any, scores_any, values):
    def step(j_next, par_next, prep, scores, rows=TK):
        prep(j_next, par_next)
        st = qk(j_next, 0, rows)
        for s in range(nmap):
            values(j_next - 1, 1 - par_next, s, "probs", TK)
            st_after = qk(j_next, s + 1, rows) if s + 1 < nmap else None
            scores(j_next, par_next, s, st, rows)
            values(j_next - 1, 1 - par_next, s, "pv", TK)
            st = st_after

    prep_any(0, 0)
    for s in range(nmap):
        scores_any(0, 0, s, None, TK)

    def body(k, c):
        step(2 * k + 1, 1, prep_far, scores_far)
        step(2 * k + 2, 0, prep_far, scores_far)
        return c

    npair = jnp.maximum(jd - 2, 0) // 2
    lax.fori_loop(0, npair, body, 0)
    t = 2 * npair
    left = jd - t
    for r in range(1, 4):
        short = jnp.logical_and(left == r, half_last)

        @pl.when(jnp.logical_and(left >= r, jnp.logical_not(short)))
        def _(r=r):
            step(t + r, r % 2, prep_any, scores_any)

        @pl.when(short)
        def _(r=r):
            step(t + r, r % 2, prep_any, scores_any, TK // 2)

    for par in range(2):
        last_par = lax.rem(left, 2) == par

        @pl.when(jnp.logical_and(last_par, jnp.logical_not(half_last)))
        def _(par=par):
            values(jd, par, None, None, TK)

        @pl.when(jnp.logical_and(last_par, half_last))
        def _(par=par):
            values(jd, par, None, None, TK // 2)


def _flash_init(m_ref, l_ref, acc_ref):
    m_ref[...] = jnp.full(m_ref.shape, NEG, F32)
    l_ref[...] = jnp.zeros_like(l_ref)
    acc_ref[...] = jnp.zeros_like(acc_ref)


def _flash_scratch(nmap, dv):
    return [
        pltpu.VMEM((nmap, TQ, LANES), BF16),
        pltpu.VMEM((2 * nmap, TK, TQ), F32),
        pltpu.VMEM((nmap, TK, TQ), BF16),
        pltpu.VMEM((2 * nmap, 1, TQ), F32),
        pltpu.VMEM((nmap, 1, TQ), F32),
        pltpu.VMEM((2 * nmap, 1, TQ), F32),
        pltpu.VMEM((nmap, dv, TQ), F32),
        pltpu.VMEM((2, TK, TQ), F32),
    ]


def _dsa_kernel(q_ref, k_ref, vt_ref, iq_ref, ik_ref, iwt_ref, bias_ref, o_ref,
                sc_ref, qm_ref, s_ref, p_ref, m_ref, l_ref, c_ref, acc_ref, mb_ref, *, top_k):
    i = pl.program_id(1)
    t0 = i * TQ
    jd = t0 // TK
    ntile = jd + 1
    qpos = t0 + lax.broadcasted_iota(jnp.int32, (1, TQ), 1)

    iq = iq_ref[...]
    for h in range(HI):
        qm_ref[h] = _half_masked(iq[:, (h // 2) * LANES:(h // 2 + 1) * LANES], h % 2 == 1)
    iwt = iwt_ref[...]

    def score_tile(j, carry, diag):
        rmax, rmin = carry
        ik_t = ik_ref[pl.ds(pl.multiple_of(j * TK, TK), TK), :]
        for hp in range(HI // 2):
            da = _dot_nt(ik_t, qm_ref[2 * hp])
            db = _dot_nt(ik_t, qm_ref[2 * hp + 1])
            wa = iwt[2 * hp:2 * hp + 1, :]
            wb = iwt[2 * hp + 1:2 * hp + 2, :]
            for r0 in range(0, TK, CH):
                piece = wa * jnp.maximum(da[r0:r0 + CH], 0.0) + wb * jnp.maximum(db[r0:r0 + CH], 0.0)
                if hp > 0:
                    piece = piece + sc_ref[j, r0:r0 + CH, :]
                if hp == HI // 2 - 1:
                    if diag:
                        kpos = j * TK + r0 + lax.broadcasted_iota(jnp.int32, (CH, TQ), 0)
                        valid = kpos <= qpos
                        rmax = jnp.maximum(rmax, _fold8(jnp.where(valid, piece, -jnp.inf), jnp.max))
                        rmin = jnp.minimum(rmin, _fold8(jnp.where(valid, piece, jnp.inf), jnp.min))
                        piece = jnp.where(valid, piece, -jnp.inf)
                    else:
                        rmax = jnp.maximum(rmax, _fold8(piece, jnp.max))
                        rmin = jnp.minimum(rmin, _fold8(piece, jnp.min))
                sc_ref[j, r0:r0 + CH, :] = piece
        return rmax, rmin

    carry = lax.fori_loop(
        0, jd, lambda j, c: score_tile(j, c, False),
        (jnp.full((SUBLANES, TQ), -jnp.inf, F32), jnp.full((SUBLANES, TQ), jnp.inf, F32)))
    rmax8, rmin8 = score_tile(jd, carry, True)
    rmax = jnp.max(rmax8, axis=0, keepdims=True)
    rmin = jnp.min(rmin8, axis=0, keepdims=True)

    def row_count(pred):
        def body(j, cnt):
            for r0 in range(0, TK, CH):
                cnt = cnt + _fold8(jnp.where(pred(sc_ref[j, r0:r0 + CH, :]), 1.0, 0.0), jnp.sum)
            return cnt
        cnt = lax.fori_loop(0, ntile, body, jnp.zeros((SUBLANES, TQ), F32))
        return jnp.sum(cnt, axis=0, keepdims=True)

    def count_ge(x):
        return row_count(lambda t: t >= x)

    kf = float(top_k)
    nvalid = (qpos + 1).astype(F32)
    few = nvalid <= kf
    c_ge0 = count_ge(0.0)
    c_gt0 = row_count(lambda t: t > 0.0)
    at0 = jnp.logical_and(c_gt0 < kf, c_ge0 >= kf)
    pos = c_gt0 >= kf
    lo0 = jnp.where(few, -3.0e38, jnp.where(pos, 0.0, jnp.where(at0, 0.0, rmin)))
    hi0 = jnp.where(pos, rmax * 1.000001 + 1e-37, 0.0)
    clo0 = jnp.where(few, kf, jnp.where(jnp.logical_or(pos, at0), c_ge0, nvalid))
    chi0 = jnp.where(pos, 0.0, c_ge0)

    def settled(clo, chi):
        return jnp.logical_or(clo == kf, kf - chi == 1.0)

    done0 = jnp.where(jnp.logical_or(jnp.logical_or(few, at0), settled(clo0, chi0)), 1.0, 0.0)

    def search_cond(st):
        it, _, _, _, _, done = st
        return jnp.logical_and(it < SEARCH_CAP, jnp.min(done) < 0.5)

    def search_body(st):
        it, lo, hi, clo, chi, done = st
        for _ in range(2):
            mid = lo + (hi - lo) * 0.5
            stuck = jnp.logical_or(mid <= lo, mid >= hi)
            cx = count_ge(mid)
            ge = cx >= kf
            act = done < 0.5
            up = jnp.logical_and(act, ge)
            dn = jnp.logical_and(act, jnp.logical_not(ge))
            lo = jnp.where(up, mid, lo)
            clo = jnp.where(up, cx, clo)
            hi = jnp.where(dn, mid, hi)
            chi = jnp.where(dn, cx, chi)
            done = jnp.where(jnp.logical_or(stuck, settled(clo, chi)), 1.0, done)
        return it + 1, lo, hi, clo, chi, done

    _, lo, hi, clo, chi, _ = lax.while_loop(
        search_cond, search_body, (jnp.int32(0), lo0, hi0, clo0, chi0, done0))

    def below_hi_max(j, mx):
        for r0 in range(0, TK, CH):
            t = sc_ref[j, r0:r0 + CH, :]
            mx = jnp.maximum(mx, _fold8(jnp.where(t < hi, t, -jnp.inf), jnp.max))
        return mx

    top_below = jnp.max(lax.fori_loop(0, ntile, below_hi_max, jnp.full((SUBLANES, TQ), -jnp.inf, F32)),
                        axis=0, keepdims=True)
    last_one = jnp.logical_and(jnp.logical_and(jnp.logical_not(few), clo != kf), kf - chi == 1.0)
    theta = jnp.where(last_one, top_below, lo)
    c_theta = count_ge(theta)

    surplus = jnp.where(jnp.logical_and(jnp.logical_not(few), c_theta > kf), c_theta - kf, 0.0)

    @pl.when(jnp.max(surplus) > 0.5)
    def _():
        ra = lax.broadcasted_iota(jnp.int32, (TK, TK), 0)
        ca = lax.broadcasted_iota(jnp.int32, (TK, TK), 1)
        triu = jnp.where(ca >= ra, 1.0, 0.0).astype(BF16)

        def body(jj, later):
            j = ntile - 1 - jj
            cnt = jnp.zeros((SUBLANES, TQ), F32)
            for r0 in range(0, TK, CH):
                eqf = jnp.where(sc_ref[j, r0:r0 + CH, :] == theta, 1.0, 0.0)
                p_ref[0, r0:r0 + CH, :] = eqf.astype(BF16)
                cnt = cnt + _fold8(eqf, jnp.sum)
            from_end = _dot(triu, p_ref[0])
            for r0 in range(0, TK, CH):
                t = sc_ref[j, r0:r0 + CH, :]
                cut = jnp.where(from_end[r0:r0 + CH] + later <= surplus, -jnp.inf, t)
                sc_ref[j, r0:r0 + CH, :] = jnp.where(t == theta, cut, t)
            return later + jnp.sum(cnt, axis=0, keepdims=True)

        lax.fori_loop(0, ntile, body, jnp.zeros((1, TQ), F32))

    _flash_init(m_ref, l_ref, acc_ref)
    q = q_ref[...]
    for h in range(HA):
        qm_ref[h] = _half_masked(q[:, (h // 2) * LANES:(h // 2 + 1) * LANES], h % 2 == 1)
    maps = [(h // 2, h, h) for h in range(HA)]

    def prep(j, par):
        for r0 in range(0, TK, CH):
            mb_ref[par, r0:r0 + CH, :] = jnp.where(sc_ref[j, r0:r0 + CH, :] >= theta, 0.0, NEG)

    def k_tile(j):
        base = pl.multiple_of(j * TK, TK)
        return lambda p, rows: k_ref[pl.ds(base, rows), p * LANES:(p + 1) * LANES]

    def qk(j, s, rows):
        return _dot_nt(k_tile(j)(maps[s][0], rows), qm_ref[s])

    def scores(j, par, near, only, st, rows):
        _flash_scores(
            par, maps, k_tile(j),
            lambda h, r0: mb_ref[par, r0:r0 + CH, :],
            (lambda h, cc, r: _near_bias_t(bias_ref, h, i, j, cc, r)) if near else None,
            qm_ref, s_ref, m_ref, c_ref, only, st, rows)

    def values(j, par, only, part, rows):
        _flash_values(par, maps, lambda h, rows: vt_ref[j, h * DA:(h + 1) * DA, 0:rows],
                      s_ref, p_ref, m_ref, l_ref, c_ref, acc_ref, only, part, rows)

    half_last = lax.rem(i, TK // TQ) == 0
    _flash_sweep(jd, half_last, HA, qk, prep, lambda j, par, s, st, rows: scores(j, par, False, s, st, rows),
                 prep, lambda j, par, s, st, rows: scores(j, par, True, s, st, rows), values)

    for p in range(HA // 2):
        o_lo = acc_ref[2 * p] / l_ref[2 * p]
        o_hi = acc_ref[2 * p + 1] / l_ref[2 * p + 1]
        o_t = jnp.concatenate([o_lo, o_hi], axis=0)
        o_ref[:, p * LANES:(p + 1) * LANES] = o_t.T.astype(BF16)


def _dsa(q, k, vt, iq, ik2, iwt, bias, batch, seq):
    top_k = min(TOPK_MAX, seq // 4)
    nq = seq // TQ
    nt = seq // TK
    hd = HA * DA
    qrow = lambda b, i: (b * nq + i, 0)
    whole = lambda b, i: (b, 0)
    once = pl.Buffered(1)
    return pl.pallas_call(
        functools.partial(_dsa_kernel, top_k=top_k),
        grid=(batch, nq),
        in_specs=[
            pl.BlockSpec((TQ, hd), qrow),
            pl.BlockSpec((seq, hd), whole, pipeline_mode=once),
            pl.BlockSpec((nt, hd, TK), lambda b, i: (b, 0, 0), pipeline_mode=once),
            pl.BlockSpec((TQ, HI * DI), qrow),
            pl.BlockSpec((seq, LANES), whole, pipeline_mode=once),
            pl.BlockSpec((HI, TQ), lambda b, i: (0, b * nq + i)),
            pl.BlockSpec((HA, 2, LANES, LANES), lambda b, i: (0, 0, 0, 0)),
        ],
        out_specs=pl.BlockSpec((TQ, hd), qrow),
        out_shape=jax.ShapeDtypeStruct((batch * seq, hd), BF16),
        scratch_shapes=[pltpu.VMEM((nt, TK, TQ), F32)] + _flash_scratch(HA, DA),
        compiler_params=_params("parallel", "arbitrary"),
        name="dsa",
    )(q, k, vt, iq, ik2, iwt, bias)


def _diff_kernel(q_ref, k_ref, vt_ref, lam_ref, g_ref, bias_ref, o_ref,
                 qm_ref, s_ref, p_ref, m_ref, l_ref, c_ref, acc_ref, mb_ref, nb_ref, *, lam_init):
    i = pl.program_id(1)
    t0 = i * TQ
    jd = t0 // TK
    _flash_init(m_ref, l_ref, acc_ref)
    q = q_ref[...]
    nmap = 2 * HB
    for s in range(nmap):
        qm_ref[s] = _half_masked(q[:, (s // 2) * LANES:(s // 2 + 1) * LANES], s % 2 == 1)
    maps = [(s // 2, s // 2, s // 2) for s in range(nmap)]
    qpos = t0 + lax.broadcasted_iota(jnp.int32, (1, TQ), 1)

    def prep(j, par):
        for r0 in range(0, TK, CH):
            kpos = j * TK + r0 + lax.broadcasted_iota(jnp.int32, (CH, TQ), 0)
            mask = jnp.where(kpos <= qpos, 0.0, NEG)
            for h in range(HB):
                nb_ref[par * HB + h, r0:r0 + CH, :] = mask + _near_bias_t(bias_ref, h, i, j, r0 // LANES, r0 % LANES)

    def k_tile(j):
        base = pl.multiple_of(j * TK, TK)
        return lambda p, rows: k_ref[pl.ds(base, rows), p * LANES:(p + 1) * LANES]

    def qk(j, s, rows):
        return _dot_nt(k_tile(j)(maps[s][0], rows), qm_ref[s])

    def scores(j, par, general, only, st, rows):
        _flash_scores(
            par, maps, k_tile(j),
            (lambda h, r0: nb_ref[par * HB + h, r0:r0 + CH, :]) if general else None, None,
            qm_ref, s_ref, m_ref, c_ref, only, st, rows)

    def values(j, par, only, part, rows):
        _flash_values(par, maps, lambda p, rows: vt_ref[j, p * LANES:(p + 1) * LANES, 0:rows],
                      s_ref, p_ref, m_ref, l_ref, c_ref, acc_ref, only, part, rows)

    half_last = lax.rem(i, TK // TQ) == 0
    _flash_sweep(jd, half_last, nmap, qk, lambda j, par: None,
                 lambda j, par, s, st, rows: scores(j, par, False, s, st, rows),
                 prep, lambda j, par, s, st, rows: scores(j, par, True, s, st, rows), values)

    lam = lam_ref[...]
    lam_full = (jnp.exp(jnp.sum(lam[0:1] * lam[1:2], axis=1, keepdims=True))
                - jnp.exp(jnp.sum(lam[2:3] * lam[3:4], axis=1, keepdims=True)) + lam_init)
    for h in range(HB):
        o0 = acc_ref[2 * h] / l_ref[2 * h]
        o1 = acc_ref[2 * h + 1] / l_ref[2 * h + 1]
        o = (o0 - lam_full * o1).T
        y = o * lax.rsqrt(jnp.mean(o * o, axis=-1, keepdims=True) + LN_EPS) * g_ref[...]
        o_ref[:, h * LANES:(h + 1) * LANES] = (y * (1.0 - lam_init)).astype(BF16)


def _diff(q, k, vt, lam, g, bias, batch, seq, lam_init):
    nq = seq // TQ
    nt = seq // TK
    hb = HB * 2 * DB
    qrow = lambda b, i: (b * nq + i, 0)
    whole = lambda b, i: (b, 0)
    once = pl.Buffered(1)
    return pl.pallas_call(
        functools.partial(_diff_kernel, lam_init=lam_init),
        grid=(batch, nq),
        in_specs=[
            pl.BlockSpec((TQ, hb), qrow),
            pl.BlockSpec((seq, hb), whole, pipeline_mode=once),
            pl.BlockSpec((nt, hb, TK), lambda b, i: (b, 0, 0), pipeline_mode=once),
            pl.BlockSpec((4, DB), lambda b, i: (0, 0)),
            pl.BlockSpec((1, 2 * DB), lambda b, i: (0, 0)),
            pl.BlockSpec((HB, 2, LANES, LANES), lambda b, i: (0, 0, 0, 0)),
        ],
        out_specs=pl.BlockSpec((TQ, hb), qrow),
        out_shape=jax.ShapeDtypeStruct((batch * seq, hb), BF16),
        scratch_shapes=_flash_scratch(2 * HB, 2 * DB) + [pltpu.VMEM((2 * HB, TK, TQ), F32)],
        compiler_params=_params("parallel", "arbitrary"),
        name="diff",
    )(q, k, vt, lam, g, bias)


def _mix_out_kernel(oa_ref, ob_ref, wa_ref, wb_ref, x_ref, g_ref, b_ref, wi_ref, wo_ref, g2_ref, b2_ref,
                    o_ref, x1_ref, xb_ref, acc_ref):
    m = _dot(oa_ref[...], wa_ref[...]) + _dot(ob_ref[...], wb_ref[...])
    x1_ref[...] = _layer_norm(ALPHA * x_ref[...] + m, g_ref[...], b_ref[...])
    _ffn_kernel(x1_ref, wi_ref, wo_ref, g2_ref, b2_ref, o_ref, xb_ref, acc_ref)


def _mix_out(oa, ob, wa, wb, x, g, b, ffn_w_in, ffn_w_out, g_all, b_all, layer):
    t, d = x.shape
    row = lambda i: (i, 0)
    const = lambda i: (0, 0)
    once = pl.Buffered(1)
    specs, scratch = _ffn_specs(d, ffn_w_out.shape[2], layer, 1)
    return pl.pallas_call(
        _mix_out_kernel,
        grid=(t // TM_FFN,),
        in_specs=[
            pl.BlockSpec((TM_FFN, oa.shape[1]), row),
            pl.BlockSpec((TM_FFN, ob.shape[1]), row),
            pl.BlockSpec(wa.shape, const, pipeline_mode=once),
            pl.BlockSpec(wb.shape, const, pipeline_mode=once),
            pl.BlockSpec((TM_FFN, d), row),
            pl.BlockSpec((1, d), const),
            pl.BlockSpec((1, d), const),
        ] + specs,
        out_specs=pl.BlockSpec((TM_FFN, d), row),
        out_shape=jax.ShapeDtypeStruct((t, d), F32),
        scratch_shapes=[pltpu.VMEM((TM_FFN, d), F32)] + scratch,
        compiler_params=_params("parallel"),
        name="mix_out_ffn",
    )(oa, ob, wa, wb, x, g, b, ffn_w_in, ffn_w_out, g_all, b_all)


def _conv_in_kernel(x_ref, w_ref, o_ref):
    xb = x_ref[...].astype(BF16)
    d = o_ref.shape[1]
    a = _dot(xb, w_ref[:, 0:d])
    gate = _dot(xb, w_ref[:, d:2 * d])
    o_ref[...] = a * jax.nn.sigmoid(gate)


def _conv_in(x, w):
    t, d = x.shape
    return pl.pallas_call(
        _conv_in_kernel,
        grid=(t // TM_CONV,),
        in_specs=[pl.BlockSpec((TM_CONV, d), lambda i: (i, 0)),
                  pl.BlockSpec((d, 2 * d), lambda i: (0, 0))],
        out_specs=pl.BlockSpec((TM_CONV, d), lambda i: (i, 0)),
        out_shape=jax.ShapeDtypeStruct((t, d), F32),
        compiler_params=_params("parallel"),
        name="conv_in",
    )(x, w)


def _conv_out_kernel(h_ref, halo_ref, dw_ref, dwb_ref, cg_ref, cb_ref, w_ref, x_ref, g_ref, b_ref,
                     wi_ref, wo_ref, g2_ref, b2_ref, o_ref, buf_ref, cv_ref, x1_ref, xb_ref, acc_ref):
    i = pl.program_id(1)
    buf_ref[0:HALO, :] = jnp.where(i > 0, halo_ref[...], 0.0)
    buf_ref[HALO:, :] = h_ref[...]
    off = HALO - (CONV_W - 1)

    def chunk(c, carry):
        r0 = pl.multiple_of(c * CONV_ROWS, CONV_ROWS)
        nwin = CONV_ROWS + HALO
        for lt in range(dwb_ref.shape[1] // LANES):
            cols = slice(lt * LANES, (lt + 1) * LANES)
            win = buf_ref[pl.ds(r0, nwin), cols]
            acc = jnp.broadcast_to(dwb_ref[:, cols], (CONV_ROWS, LANES))
            for sub in range(SUBLANES):
                shifted = pltpu.roll(win, nwin - sub, axis=0) if sub else win
                for k in range(CONV_W):
                    if (off + k) % SUBLANES == sub:
                        a0 = off + k - sub
                        acc = acc + dw_ref[k:k + 1, cols] * shifted[a0:a0 + CONV_ROWS]
            cv_ref[pl.ds(r0, CONV_ROWS), cols] = acc
        return carry

    lax.fori_loop(0, TM_CONV // CONV_ROWS, chunk, 0)
    y = _layer_norm(cv_ref[...], cg_ref[...], cb_ref[...])
    u = (y * jax.nn.sigmoid(y)).astype(BF16)
    m = _dot(u, w_ref[...])
    x1_ref[...] = _layer_norm(ALPHA * x_ref[...] + m, g_ref[...], b_ref[...])
    _ffn_kernel(x1_ref, wi_ref, wo_ref, g2_ref, b2_ref, o_ref, xb_ref, acc_ref)


def _conv_out(h, dw, dwb, cg, cb, w, x, g, b, ffn_w_in, ffn_w_out, g_all, b_all, layer, batch, seq):
    t, d = x.shape
    specs, scratch = _ffn_specs(d, ffn_w_out.shape[2], layer, 1)
    nt = seq // TM_CONV
    per = TM_CONV // HALO
    row = lambda bb, i: (bb * nt + i, 0)
    halo = lambda bb, i: (jnp.maximum((bb * nt + i) * per - 1, 0), 0)
    const = lambda bb, i: (0, 0)
    return pl.pallas_call(
        _conv_out_kernel,
        grid=(batch, nt),
        in_specs=[
            pl.BlockSpec((TM_CONV, d), row),
            pl.BlockSpec((HALO, d), halo),
            pl.BlockSpec(dw.shape, const),
            pl.BlockSpec((1, d), const),
            pl.BlockSpec((1, d), const),
            pl.BlockSpec((1, d), const),
            pl.BlockSpec((d, d), const, pipeline_mode=pl.Buffered(1)),
            pl.BlockSpec((TM_CONV, d), row),
            pl.BlockSpec((1, d), const),
            pl.BlockSpec((1, d), const),
        ] + specs,
        out_specs=pl.BlockSpec((TM_CONV, d), row),
        out_shape=jax.ShapeDtypeStruct((t, d), F32),
        scratch_shapes=[pltpu.VMEM((TM_CONV + HALO, d), F32), pltpu.VMEM((TM_CONV, d), F32),
                        pltpu.VMEM((TM_CONV, d), F32)] + scratch,
        compiler_params=_params("parallel", "arbitrary"),
        name="conv_out_ffn",
    )(h, h, dw, dwb, cg, cb, w, x, g, b, ffn_w_in, ffn_w_out, g_all, b_all)


def _attention_mixer(x, w_in, ikg, ikb, lam, subg, w_out, bias, g, b, ffn_tail, batch, seq, layer_idx):
    c = [0]
    for n in (HA * DA, HA * DA, HA * DA, HI * DI, DI, HI, HB * 2 * DB, HB * 2 * DB, HB * 2 * DB):
        c.append(c[-1] + n)
    w_ik = w_in[:, c[4]:c[5]]
    w_iw = jnp.pad(w_in[:, c[5]:c[6]], ((0, 0), (0, LANES - HI)))
    w = jnp.concatenate([w_in[:, :c[4]], w_ik, w_ik, w_iw, w_in[:, c[6]:]], axis=1).astype(BF16)
    ikg2 = jnp.concatenate([ikg, ikg])[None]
    ikb2 = jnp.concatenate([ikb, ikb])[None]
    qa, ka, vat, iq, ik2, iwt, qb, kb, vbt = _mix_proj(x, w, ikg2, ikb2)
    o_a = _dsa(qa, ka, vat, iq, ik2, iwt, bias[:HA], batch, seq)
    lam_init = 0.8 - 0.6 * math.exp(-0.3 * layer_idx)
    o_b = _diff(qb, kb, vbt, lam, subg[None], bias[HA:], batch, seq, lam_init)
    wo = w_out.astype(BF16)
    return _mix_out(o_a, o_b, wo[:HA * DA], wo[HA * DA:], x, g, b, *ffn_tail, layer_idx)


def _conv_module(x, w_in, dw, dwb, cg, cb, w_out, g, b, ffn_tail, layer_idx, batch, seq):
    h = _conv_in(x, w_in.astype(BF16))
    dwp = jnp.pad(dw, ((0, HALO - CONV_W), (0, 0)))
    return _conv_out(h, dwp, dwb[None], cg[None], cb[None], w_out.astype(BF16), x, g, b,
                     *ffn_tail, layer_idx, batch, seq)


def kernel(x, ffn_in, ffn_out, ln_g, ln_b, rel_bias, mix_w_in, idx_k_g, idx_k_b, diff_lambda,
           diff_subln_g, mix_w_out, conv_w_in, conv_dw, conv_dw_b, conv_ln_g, conv_ln_b, conv_w_out):
    batch, seq, d = x.shape
    x = x.reshape(batch * seq, d)
    bias = _bias_tiles(rel_bias.T)
    ffn_in_b = ffn_in.astype(BF16)
    ffn_out_b = ffn_out.astype(BF16)
    g_all = ln_g.reshape(3 * DEPTH, 1, d)
    b_all = ln_b.reshape(3 * DEPTH, 1, d)
    ffn_tail = (ffn_in_b, ffn_out_b, g_all, b_all)
    for l in range(DEPTH):
        j = l // 2
        x = _ffn(x, ffn_in_b, ffn_out_b, g_all, b_all, l, 0)
        if l % 2 == 0:
            x = _attention_mixer(x, mix_w_in[j], idx_k_g[j], idx_k_b[j], diff_lambda[j], diff_subln_g[j],
                                 mix_w_out[j], bias, ln_g[l, 1][None], ln_b[l, 1][None], ffn_tail, batch, seq, l)
        else:
            x = _conv_module(x, conv_w_in[j], conv_dw[j], conv_dw_b[j], conv_ln_g[j], conv_ln_b[j],
                             conv_w_out[j], ln_g[l, 1][None], ln_b[l, 1][None], ffn_tail, l, batch, seq)
    return x.reshape(batch, seq, d)
```

```python
import functools
import math

import jax
import jax.numpy as jnp
from jax import lax
from jax.experimental import pallas as pl
from jax.experimental.pallas import tpu as pltpu

F32 = jnp.float32
BF16 = jnp.bfloat16

D_MODEL = 1024
DEPTH = 4
HA, DA = 8, 64
HI, DI = 8, 64
TOPK_MAX = 256
HB, DB = 4, 64
CONV_W = 31
D_FF = 2816
NUM_BUCKETS = 32
MAX_DISTANCE = 128
ALPHA = (2 * DEPTH) ** 0.25
LN_EPS = 1e-5

LANES = 128
SUBLANES = 8
BF16_ROWS = 16
VMEM_LIMIT = 56 * 1024 * 1024

TQ = 256
TK = 512
QB = TQ // LANES
KB = TK // LANES
CH = 64
TM_FFN = 512
TF_FFN = 256
TM_PROJ = TK
TM_CONV = 512
HALO = 32
CONV_ROWS = 128
NEG = -1e30
LOG2E = math.log2(math.e)
SEARCH_CAP = 1200


def _params(*sem):
    return pltpu.CompilerParams(dimension_semantics=sem, vmem_limit_bytes=VMEM_LIMIT)


def _layer_norm(v, g, b):
    mu = jnp.mean(v, axis=-1, keepdims=True)
    c = v - mu
    var = jnp.mean(c * c, axis=-1, keepdims=True)
    return c * lax.rsqrt(var + LN_EPS) * g + b


def _dot(a, b):
    return jnp.dot(a, b, preferred_element_type=F32)


def _dot_nt(a, b):
    return lax.dot_general(a, b, (((1,), (1,)), ((), ())), preferred_element_type=F32)


def _ffn_kernel(x_ref, wi_ref, wo_ref, g_ref, b_ref, o_ref, xb_ref, acc_ref):
    f = wo_ref.shape[0]
    xb_ref[...] = x_ref[...].astype(BF16)
    for c in range(f // TF_FFN):
        xb = xb_ref[...]
        a = _dot(xb, wi_ref[:, c * TF_FFN:(c + 1) * TF_FFN])
        u = _dot(xb, wi_ref[:, f + c * TF_FFN:f + (c + 1) * TF_FFN])
        h = (a * jax.nn.sigmoid(a) * u).astype(BF16)
        part = _dot(h, wo_ref[c * TF_FFN:(c + 1) * TF_FFN, :])
        if c == 0:
            acc_ref[...] = part
        else:
            acc_ref[...] += part
    y = ALPHA * x_ref[...] + 0.5 * acc_ref[...]
    o_ref[...] = _layer_norm(y, g_ref[...], b_ref[...])


def _ffn_specs(d, f, layer, half):
    once = pl.Buffered(1)
    ln = 3 * layer + 2 * half
    specs = [
        pl.BlockSpec((None, None, d, 2 * f), lambda *_: (layer, half, 0, 0), pipeline_mode=once),
        pl.BlockSpec((None, None, f, d), lambda *_: (layer, half, 0, 0), pipeline_mode=once),
        pl.BlockSpec((None, 1, d), lambda *_: (ln, 0, 0)),
        pl.BlockSpec((None, 1, d), lambda *_: (ln, 0, 0)),
    ]
    scratch = [pltpu.VMEM((TM_FFN, d), BF16), pltpu.VMEM((TM_FFN, d), F32)]
    return specs, scratch


def _ffn(x, w_in, w_out, g, b, layer, half):
    t, d = x.shape
    specs, scratch = _ffn_specs(d, w_out.shape[2], layer, half)
    return pl.pallas_call(
        _ffn_kernel,
        grid=(t // TM_FFN,),
        in_specs=[pl.BlockSpec((TM_FFN, d), lambda i: (i, 0))] + specs,
        out_specs=pl.BlockSpec((TM_FFN, d), lambda i: (i, 0)),
        out_shape=jax.ShapeDtypeStruct((t, d), F32),
        scratch_shapes=scratch,
        compiler_params=_params("parallel"),
        name="ffn",
    )(x, w_in, w_out, g, b)


_W3 = 3 * HA * DA
_C_IQ = _W3
_C_IK = _C_IQ + HI * DI
_C_IW = _C_IK + LANES
_C_B = _C_IW + LANES
_C_END = _C_B + 3 * HB * 2 * DB


def _mix_proj_kernel(x_ref, w_ref, ikg_ref, ikb_ref,
                     qa_ref, ka_ref, vat_ref, iq_ref, ik_ref, iwt_ref, qb_ref, kb_ref, vbt_ref):
    xb = x_ref[...].astype(BF16)
    hd = HA * DA
    qa_ref[...] = (_dot(xb, w_ref[:, 0:hd]) * (DA ** -0.5 * LOG2E)).astype(BF16)
    ka_ref[...] = _dot(xb, w_ref[:, hd:2 * hd]).astype(BF16)
    vat_ref[0] = _dot(xb, w_ref[:, 2 * hd:3 * hd]).T.astype(BF16)
    iq_ref[...] = (_dot(xb, w_ref[:, _C_IQ:_C_IK]) * (DI ** -0.5)).astype(BF16)
    ik2 = _dot(xb, w_ref[:, _C_IK:_C_IW])
    ik_ref[...] = _layer_norm(ik2, ikg_ref[...], ikb_ref[...]).astype(BF16)
    iw = _dot(xb, w_ref[:, _C_IW:_C_B]) * (HI ** -0.5)
    iwt_ref[...] = iw.T[0:HI, :]
    hb = HB * 2 * DB
    qb_ref[...] = (_dot(xb, w_ref[:, _C_B:_C_B + hb]) * (DB ** -0.5 * LOG2E)).astype(BF16)
    kb_ref[...] = _dot(xb, w_ref[:, _C_B + hb:_C_B + 2 * hb]).astype(BF16)
    vbt_ref[0] = _dot(xb, w_ref[:, _C_B + 2 * hb:_C_END]).T.astype(BF16)


def _mix_proj(x, w, ikg2, ikb2):
    t, d = x.shape
    nt = t // TM_PROJ
    row = lambda i: (i, 0)
    const = lambda i: (0, 0)
    tile = lambda i: (i, 0, 0)
    wide = lambda n, dt: jax.ShapeDtypeStruct((t, n), dt)
    hd, hb = HA * DA, HB * 2 * DB
    return pl.pallas_call(
        _mix_proj_kernel,
        grid=(nt,),
        in_specs=[
            pl.BlockSpec((TM_PROJ, d), row),
            pl.BlockSpec((d, _C_END), const),
            pl.BlockSpec((1, LANES), const),
            pl.BlockSpec((1, LANES), const),
        ],
        out_specs=[
            pl.BlockSpec((TM_PROJ, hd), row), pl.BlockSpec((TM_PROJ, hd), row),
            pl.BlockSpec((1, hd, TM_PROJ), tile), pl.BlockSpec((TM_PROJ, HI * DI), row),
            pl.BlockSpec((TM_PROJ, LANES), row), pl.BlockSpec((HI, TM_PROJ), lambda i: (0, i)),
            pl.BlockSpec((TM_PROJ, hb), row), pl.BlockSpec((TM_PROJ, hb), row),
            pl.BlockSpec((1, hb, TM_PROJ), tile),
        ],
        out_shape=[wide(hd, BF16), wide(hd, BF16), jax.ShapeDtypeStruct((nt, hd, TM_PROJ), BF16),
                   wide(HI * DI, BF16), wide(LANES, BF16), jax.ShapeDtypeStruct((HI, t), F32),
                   wide(hb, BF16), wide(hb, BF16), jax.ShapeDtypeStruct((nt, hb, TM_PROJ), BF16)],
        compiler_params=_params("parallel"),
        name="mix_proj",
    )(x, w, ikg2, ikb2)


def _bias_kernel(tbl_ref, o_ref):
    nh = o_ref.shape[0]
    kk = lax.broadcasted_iota(jnp.int32, (LANES, LANES), 0)
    qq = lax.broadcasted_iota(jnp.int32, (LANES, LANES), 1)
    max_exact = NUM_BUCKETS // 2
    for blk in range(2):
        n = jnp.maximum(qq - kk + blk * LANES, 0)
        nf = jnp.maximum(n, 1).astype(F32)
        large = max_exact + (jnp.log(nf / max_exact) / math.log(MAX_DISTANCE / max_exact)
                             * (NUM_BUCKETS - max_exact)).astype(jnp.int32)
        large = jnp.minimum(large, NUM_BUCKETS - 1)
        bucket = jnp.where(n < max_exact, n, large)
        for h in range(nh):
            acc = jnp.zeros((LANES, LANES), F32)
            for k in range(NUM_BUCKETS):
                acc = jnp.where(bucket == k, tbl_ref[h, k], acc)
            o_ref[h, blk] = (acc - tbl_ref[h, NUM_BUCKETS - 1]) * LOG2E


def _bias_tiles(rel_bias_t):
    nh = rel_bias_t.shape[0]
    return pl.pallas_call(
        _bias_kernel,
        in_specs=[pl.BlockSpec(memory_space=pltpu.SMEM)],
        out_shape=jax.ShapeDtypeStruct((nh, 2, LANES, LANES), F32),
        name="bias_tiles",
    )(rel_bias_t)


def _half_masked(x, upper):
    lane = lax.broadcasted_iota(jnp.int32, x.shape, 1)
    keep = (lane >= LANES // 2) if upper else (lane < LANES // 2)
    return jnp.where(keep, x, jnp.zeros_like(x))


def _near_bias_t(bias_ref, head, i, j, cc, r0):
    b0 = bias_ref[head, 0, r0:r0 + CH, :]
    b1 = bias_ref[head, 1, r0:r0 + CH, :]
    cols = []
    for qq in range(QB):
        delta = (i * QB + qq) - (j * KB + cc)
        cols.append(jnp.where(delta == 0, b0, jnp.where(delta == 1, b1, jnp.zeros_like(b0))))
    return jnp.concatenate(cols, axis=1)


def _fold8(x, op):
    return op(x.reshape(CH // SUBLANES, SUBLANES, x.shape[1]), axis=0)


def _flash_scores(par, maps, k_tile, mask_bias, near_bias, qm_ref, s_ref, m_ref, c_ref, only=None, st=None,
                  rows=TK):
    nmap = len(maps)
    cur = par * nmap
    prv = nmap - cur
    nch = LANES // CH
    for s in (range(nmap) if only is None else (only,)):
        pair, _, head = maps[s]
        if st is None or only is None:
            st = _dot_nt(k_tile(pair, rows), qm_ref[s])
        mx = jnp.full((SUBLANES, TQ), NEG, F32)
        for cc in range(rows // LANES):
            for hh in range(nch):
                r0 = cc * LANES + hh * CH
                piece = st[r0:r0 + CH]
                if mask_bias is not None:
                    piece = piece + mask_bias(head, r0)
                if near_bias is not None:
                    piece = piece + near_bias(head, cc, hh * CH)
                s_ref[cur + s, r0:r0 + CH, :] = piece
                mx = jnp.maximum(mx, _fold8(piece, jnp.max))
        m_prev = m_ref[prv + s]
        m_next = jnp.maximum(m_prev, jnp.max(mx, axis=0, keepdims=True))
        c_ref[cur + s] = jnp.exp2(m_prev - m_next)
        m_ref[cur + s] = m_next


def _flash_values(par, maps, vt_tile, s_ref, p_ref, m_ref, l_ref, c_ref, acc_ref, only=None, part=None, rows=TK):
    nmap = len(maps)
    cur = par * nmap
    for s in (range(nmap) if only is None else (only,)):
        if part != "pv":
            m_new = m_ref[cur + s]
            for r0 in range(0, rows, CH):
                p_ref[s, r0:r0 + CH, :] = jnp.exp2(s_ref[cur + s, r0:r0 + CH, :] - m_new).astype(BF16)
        if part == "probs":
            continue
        lhs = jnp.concatenate([vt_tile(maps[s][1], rows), jnp.ones((BF16_ROWS, rows), BF16)], axis=0)
        pv = _dot(lhs, p_ref[s, 0:rows, :])
        dv = acc_ref.shape[1]
        l_ref[s] = c_ref[cur + s] * l_ref[s] + pv[dv:dv + 1]
        acc_ref[s] = c_ref[cur + s] * acc_ref[s] + pv[:dv]


def _flash_sweep(jd, half_last, nmap, qk, prep_far, scores_far, prep_any, scores_any, values):
    def step(j_next, par_next, prep, scores, rows=TK):
        prep(j_next, par_next)
        st = qk(j_next, 0, rows)
        for s in range(nmap):
            values(j_next - 1, 1 - par_next, s, "probs", TK)
            st_after = qk(j_next, s + 1, rows) if s + 1 < nmap else None
            scores(j_next, par_next, s, st, rows)
            values(j_next - 1, 1 - par_next, s, "pv", TK)
            st = st_after

    prep_any(0, 0)
    for s in range(nmap):
        scores_any(0, 0, s, None, TK)

    def body(k, c):
        step(2 * k + 1, 1, prep_far, scores_far)
        step(2 * k + 2, 0, prep_far, scores_far)
        return c

    npair = jnp.maximum(jd - 2, 0) // 2
    lax.fori_loop(0, npair, body, 0)
    t = 2 * npair
    left = jd - t
    for r in range(1, 4):
        short = jnp.logical_and(left == r, half_last)

        @pl.when(jnp.logical_and(left >= r, jnp.logical_not(short)))
        def _(r=r):
            step(t + r, r % 2, prep_any, scores_any)

        @pl.when(short)
        def _(r=r):
            step(t + r, r % 2, prep_any, scores_any, TK // 2)

    for par in range(2):
        last_par = lax.rem(left, 2) == par

        @pl.when(jnp.logical_and(last_par, jnp.logical_not(half_last)))
        def _(par=par):
            values(jd, par, None, None, TK)

        @pl.when(jnp.logical_and(last_par, half_last))
        def _(par=par):
            values(jd, par, None, None, TK // 2)


def _flash_init(m_ref, l_ref, acc_ref):
    m_ref[...] = jnp.full(m_ref.shape, NEG, F32)
    l_ref[...] = jnp.zeros_like(l_ref)
    acc_ref[...] = jnp.zeros_like(acc_ref)


def _flash_scratch(nmap, dv):
    return [
        pltpu.VMEM((nmap, TQ, LANES), BF16),
        pltpu.VMEM((2 * nmap, TK, TQ), F32),
        pltpu.VMEM((nmap, TK, TQ), BF16),
        pltpu.VMEM((2 * nmap, 1, TQ), F32),
        pltpu.VMEM((nmap, 1, TQ), F32),
        pltpu.VMEM((2 * nmap, 1, TQ), F32),
        pltpu.VMEM((nmap, dv, TQ), F32),
        pltpu.VMEM((2, TK, TQ), F32),
    ]


def _dsa_kernel(q_ref, k_ref, vt_ref, iq_ref, ik_ref, iwt_ref, bias_ref, o_ref,
                sc_ref, qm_ref, s_ref, p_ref, m_ref, l_ref, c_ref, acc_ref, mb_ref, *, top_k):
    i = pl.program_id(1)
    t0 = i * TQ
    jd = t0 // TK
    ntile = jd + 1
    qpos = t0 + lax.broadcasted_iota(jnp.int32, (1, TQ), 1)

    iq = iq_ref[...]
    for h in range(HI):
        qm_ref[h] = _half_masked(iq[:, (h // 2) * LANES:(h // 2 + 1) * LANES], h % 2 == 1)
    iwt = iwt_ref[...]

    def score_tile(j, carry, diag):
        rmax, rmin = carry
        ik_t = ik_ref[pl.ds(pl.multiple_of(j * TK, TK), TK), :]
        for hp in range(HI // 2):
            da = _dot_nt(ik_t, qm_ref[2 * hp])
            db = _dot_nt(ik_t, qm_ref[2 * hp + 1])
            wa = iwt[2 * hp:2 * hp + 1, :]
            wb = iwt[2 * hp + 1:2 * hp + 2, :]
            for r0 in range(0, TK, CH):
                piece = wa * jnp.maximum(da[r0:r0 + CH], 0.0) + wb * jnp.maximum(db[r0:r0 + CH], 0.0)
                if hp > 0:
                    piece = piece + sc_ref[j, r0:r0 + CH, :]
                if hp == HI // 2 - 1:
                    if diag:
                        kpos = j * TK + r0 + lax.broadcasted_iota(jnp.int32, (CH, TQ), 0)
                        valid = kpos <= qpos
                        rmax = jnp.maximum(rmax, _fold8(jnp.where(valid, piece, -jnp.inf), jnp.max))
                        rmin = jnp.minimum(rmin, _fold8(jnp.where(valid, piece, jnp.inf), jnp.min))
                        piece = jnp.where(valid, piece, -jnp.inf)
                    else:
                        rmax = jnp.maximum(rmax, _fold8(piece, jnp.max))
                        rmin = jnp.minimum(rmin, _fold8(piece, jnp.min))
                sc_ref[j, r0:r0 + CH, :] = piece
        return rmax, rmin

    carry = lax.fori_loop(
        0, jd, lambda j, c: score_tile(j, c, False),
        (jnp.full((SUBLANES, TQ), -jnp.inf, F32), jnp.full((SUBLANES, TQ), jnp.inf, F32)))
    rmax8, rmin8 = score_tile(jd, carry, True)
    rmax = jnp.max(rmax8, axis=0, keepdims=True)
    rmin = jnp.min(rmin8, axis=0, keepdims=True)

    def row_count(pred):
        def body(j, cnt):
            for r0 in range(0, TK, CH):
                cnt = cnt + _fold8(jnp.where(pred(sc_ref[j, r0:r0 + CH, :]), 1.0, 0.0), jnp.sum)
            return cnt
        cnt = lax.fori_loop(0, ntile, body, jnp.zeros((SUBLANES, TQ), F32))
        return jnp.sum(cnt, axis=0, keepdims=True)

    def count_ge(x):
        return row_count(lambda t: t >= x)

    kf = float(top_k)
    nvalid = (qpos + 1).astype(F32)
    few = nvalid <= kf
    c_ge0 = count_ge(0.0)
    c_gt0 = row_count(lambda t: t > 0.0)
    at0 = jnp.logical_and(c_gt0 < kf, c_ge0 >= kf)
    pos = c_gt0 >= kf
    lo0 = jnp.where(few, -3.0e38, jnp.where(pos, 0.0, jnp.where(at0, 0.0, rmin)))
    hi0 = jnp.where(pos, rmax * 1.000001 + 1e-37, 0.0)
    clo0 = jnp.where(few, kf, jnp.where(jnp.logical_or(pos, at0), c_ge0, nvalid))
    chi0 = jnp.where(pos, 0.0, c_ge0)

    def settled(clo, chi):
        return jnp.logical_or(clo == kf, kf - chi == 1.0)

    done0 = jnp.where(jnp.logical_or(jnp.logical_or(few, at0), settled(clo0, chi0)), 1.0, 0.0)

    def search_cond(st):
        it, _, _, _, _, done = st
        return jnp.logical_and(it < SEARCH_CAP, jnp.min(done) < 0.5)

    def search_body(st):
        it, lo, hi, clo, chi, done = st
        for _ in range(3):
            mid = lo + (hi - lo) * 0.5
            stuck = jnp.logical_or(mid <= lo, mid >= hi)
            cx = count_ge(mid)
            ge = cx >= kf
            act = done < 0.5
            up = jnp.logical_and(act, ge)
            dn = jnp.logical_and(act, jnp.logical_not(ge))
            lo = jnp.where(up, mid, lo)
            clo = jnp.where(up, cx, clo)
            hi = jnp.where(dn, mid, hi)
            chi = jnp.where(dn, cx, chi)
            done = jnp.where(jnp.logical_or(stuck, settled(clo, chi)), 1.0, done)
        return it + 1, lo, hi, clo, chi, done

    _, lo, hi, clo, chi, _ = lax.while_loop(
        search_cond, search_body, (jnp.int32(0), lo0, hi0, clo0, chi0, done0))

    def below_hi_max(j, mx):
        for r0 in range(0, TK, CH):
            t = sc_ref[j, r0:r0 + CH, :]
            mx = jnp.maximum(mx, _fold8(jnp.where(t < hi, t, -jnp.inf), jnp.max))
        return mx

    top_below = jnp.max(lax.fori_loop(0, ntile, below_hi_max, jnp.full((SUBLANES, TQ), -jnp.inf, F32)),
                        axis=0, keepdims=True)
    last_one = jnp.logical_and(jnp.logical_and(jnp.logical_not(few), clo != kf), kf - chi == 1.0)
    theta = jnp.where(last_one, top_below, lo)
    c_theta = count_ge(theta)

    surplus = jnp.where(jnp.logical_and(jnp.logical_not(few), c_theta > kf), c_theta - kf, 0.0)

    @pl.when(jnp.max(surplus) > 0.5)
    def _():
        ra = lax.broadcasted_iota(jnp.int32, (TK, TK), 0)
        ca = lax.broadcasted_iota(jnp.int32, (TK, TK), 1)
        triu = jnp.where(ca >= ra, 1.0, 0.0).astype(BF16)

        def body(jj, later):
            j = ntile - 1 - jj
            cnt = jnp.zeros((SUBLANES, TQ), F32)
            for r0 in range(0, TK, CH):
                eqf = jnp.where(sc_ref[j, r0:r0 + CH, :] == theta, 1.0, 0.0)
                p_ref[0, r0:r0 + CH, :] = eqf.astype(BF16)
                cnt = cnt + _fold8(eqf, jnp.sum)
            from_end = _dot(triu, p_ref[0])
            for r0 in range(0, TK, CH):
                t = sc_ref[j, r0:r0 + CH, :]
                cut = jnp.where(from_end[r0:r0 + CH] + later <= surplus, -jnp.inf, t)
                sc_ref[j, r0:r0 + CH, :] = jnp.where(t == theta, cut, t)
            return later + jnp.sum(cnt, axis=0, keepdims=True)

        lax.fori_loop(0, ntile, body, jnp.zeros((1, TQ), F32))

    _flash_init(m_ref, l_ref, acc_ref)
    q = q_ref[...]
    for h in range(HA):
        qm_ref[h] = _half_masked(q[:, (h // 2) * LANES:(h // 2 + 1) * LANES], h % 2 == 1)
    maps = [(h // 2, h, h) for h in range(HA)]

    def prep(j, par):
        for r0 in range(0, TK, CH):
            mb_ref[par, r0:r0 + CH, :] = jnp.where(sc_ref[j, r0:r0 + CH, :] >= theta, 0.0, NEG)

    def k_tile(j):
        base = pl.multiple_of(j * TK, TK)
        return lambda p, rows: k_ref[pl.ds(base, rows), p * LANES:(p + 1) * LANES]

    def qk(j, s, rows):
        return _dot_nt(k_tile(j)(maps[s][0], rows), qm_ref[s])

    def scores(j, par, near, only, st, rows):
        _flash_scores(
            par, maps, k_tile(j),
            lambda h, r0: mb_ref[par, r0:r0 + CH, :],
            (lambda h, cc, r: _near_bias_t(bias_ref, h, i, j, cc, r)) if near else None,
            qm_ref, s_ref, m_ref, c_ref, only, st, rows)

    def values(j, par, only, part, rows):
        _flash_values(par, maps, lambda h, rows: vt_ref[j, h * DA:(h + 1) * DA, 0:rows],
                      s_ref, p_ref, m_ref, l_ref, c_ref, acc_ref, only, part, rows)

    half_last = lax.rem(i, TK // TQ) == 0
    _flash_sweep(jd, half_last, HA, qk, prep, lambda j, par, s, st, rows: scores(j, par, False, s, st, rows),
                 prep, lambda j, par, s, st, rows: scores(j, par, True, s, st, rows), values)

    for p in range(HA // 2):
        o_lo = acc_ref[2 * p] / l_ref[2 * p]
        o_hi = acc_ref[2 * p + 1] / l_ref[2 * p + 1]
        o_t = jnp.concatenate([o_lo, o_hi], axis=0)
        o_ref[:, p * LANES:(p + 1) * LANES] = o_t.T.astype(BF16)


def _dsa(q, k, vt, iq, ik2, iwt, bias, batch, seq):
    top_k = min(TOPK_MAX, seq // 4)
    nq = seq // TQ
    nt = seq // TK
    hd = HA * DA
    qrow = lambda b, i: (b * nq + i, 0)
    whole = lambda b, i: (b, 0)
    once = pl.Buffered(1)
    return pl.pallas_call(
        functools.partial(_dsa_kernel, top_k=top_k),
        grid=(batch, nq),
        in_specs=[
            pl.BlockSpec((TQ, hd), qrow),
            pl.BlockSpec((seq, hd), whole, pipeline_mode=once),
            pl.BlockSpec((nt, hd, TK), lambda b, i: (b, 0, 0), pipeline_mode=once),
            pl.BlockSpec((TQ, HI * DI), qrow),
            pl.BlockSpec((seq, LANES), whole, pipeline_mode=once),
            pl.BlockSpec((HI, TQ), lambda b, i: (0, b * nq + i)),
            pl.BlockSpec((HA, 2, LANES, LANES), lambda b, i: (0, 0, 0, 0)),
        ],
        out_specs=pl.BlockSpec((TQ, hd), qrow),
        out_shape=jax.ShapeDtypeStruct((batch * seq, hd), BF16),
        scratch_shapes=[pltpu.VMEM((nt, TK, TQ), F32)] + _flash_scratch(HA, DA),
        compiler_params=_params("parallel", "arbitrary"),
        name="dsa",
    )(q, k, vt, iq, ik2, iwt, bias)


def _diff_kernel(q_ref, k_ref, vt_ref, lam_ref, g_ref, bias_ref, o_ref,
                 qm_ref, s_ref, p_ref, m_ref, l_ref, c_ref, acc_ref, mb_ref, nb_ref, *, lam_init):
    i = pl.program_id(1)
    t0 = i * TQ
    jd = t0 // TK
    _flash_init(m_ref, l_ref, acc_ref)
    q = q_ref[...]
    nmap = 2 * HB
    for s in range(nmap):
        qm_ref[s] = _half_masked(q[:, (s // 2) * LANES:(s // 2 + 1) * LANES], s % 2 == 1)
    maps = [(s // 2, s // 2, s // 2) for s in range(nmap)]
    qpos = t0 + lax.broadcasted_iota(jnp.int32, (1, TQ), 1)

    def prep(j, par):
        for r0 in range(0, TK, CH):
            kpos = j * TK + r0 + lax.broadcasted_iota(jnp.int32, (CH, TQ), 0)
            mask = jnp.where(kpos <= qpos, 0.0, NEG)
            for h in range(HB):
                nb_ref[par * HB + h, r0:r0 + CH, :] = mask + _near_bias_t(bias_ref, h, i, j, r0 // LANES, r0 % LANES)

    def k_tile(j):
        base = pl.multiple_of(j * TK, TK)
        return lambda p, rows: k_ref[pl.ds(base, rows), p * LANES:(p + 1) * LANES]

    def qk(j, s, rows):
        return _dot_nt(k_tile(j)(maps[s][0], rows), qm_ref[s])

    def scores(j, par, general, only, st, rows):
        _flash_scores(
            par, maps, k_tile(j),
            (lambda h, r0: nb_ref[par * HB + h, r0:r0 + CH, :]) if general else None, None,
            qm_ref, s_ref, m_ref, c_ref, only, st, rows)

    def values(j, par, only, part, rows):
        _flash_values(par, maps, lambda p, rows: vt_ref[j, p * LANES:(p + 1) * LANES, 0:rows],
                      s_ref, p_ref, m_ref, l_ref, c_ref, acc_ref, only, part, rows)

    half_last = lax.rem(i, TK // TQ) == 0
    _flash_sweep(jd, half_last, nmap, qk, lambda j, par: None,
                 lambda j, par, s, st, rows: scores(j, par, False, s, st, rows),
                 prep, lambda j, par, s, st, rows: scores(j, par, True, s, st, rows), values)

    lam = lam_ref[...]
    lam_full = (jnp.exp(jnp.sum(lam[0:1] * lam[1:2], axis=1, keepdims=True))
                - jnp.exp(jnp.sum(lam[2:3] * lam[3:4], axis=1, keepdims=True)) + lam_init)
    for h in range(HB):
        o0 = acc_ref[2 * h] / l_ref[2 * h]
        o1 = acc_ref[2 * h + 1] / l_ref[2 * h + 1]
        o = (o0 - lam_full * o1).T
        y = o * lax.rsqrt(jnp.mean(o * o, axis=-1, keepdims=True) + LN_EPS) * g_ref[...]
        o_ref[:, h * LANES:(h + 1) * LANES] = (y * (1.0 - lam_init)).astype(BF16)


def _diff(q, k, vt, lam, g, bias, batch, seq, lam_init):
    nq = seq // TQ
    nt = seq // TK
    hb = HB * 2 * DB
    qrow = lambda b, i: (b * nq + i, 0)
    whole = lambda b, i: (b, 0)
    once = pl.Buffered(1)
    return pl.pallas_call(
        functools.partial(_diff_kernel, lam_init=lam_init),
        grid=(batch, nq),
        in_specs=[
            pl.BlockSpec((TQ, hb), qrow),
            pl.BlockSpec((seq, hb), whole, pipeline_mode=once),
            pl.BlockSpec((nt, hb, TK), lambda b, i: (b, 0, 0), pipeline_mode=once),
            pl.BlockSpec((4, DB), lambda b, i: (0, 0)),
            pl.BlockSpec((1, 2 * DB), lambda b, i: (0, 0)),
            pl.BlockSpec((HB, 2, LANES, LANES), lambda b, i: (0, 0, 0, 0)),
        ],
        out_specs=pl.BlockSpec((TQ, hb), qrow),
        out_shape=jax.ShapeDtypeStruct((batch * seq, hb), BF16),
        scratch_shapes=_flash_scratch(2 * HB, 2 * DB) + [pltpu.VMEM((2 * HB, TK, TQ), F32)],
        compiler_params=_params("parallel", "arbitrary"),
        name="diff",
    )(q, k, vt, lam, g, bias)


def _mix_out_kernel(oa_ref, ob_ref, wa_ref, wb_ref, x_ref, g_ref, b_ref, wi_ref, wo_ref, g2_ref, b2_ref,
                    o_ref, x1_ref, xb_ref, acc_ref):
    m = _dot(oa_ref[...], wa_ref[...]) + _dot(ob_ref[...], wb_ref[...])
    x1_ref[...] = _layer_norm(ALPHA * x_ref[...] + m, g_ref[...], b_ref[...])
    _ffn_kernel(x1_ref, wi_ref, wo_ref, g2_ref, b2_ref, o_ref, xb_ref, acc_ref)


def _mix_out(oa, ob, wa, wb, x, g, b, ffn_w_in, ffn_w_out, g_all, b_all, layer):
    t, d = x.shape
    row = lambda i: (i, 0)
    const = lambda i: (0, 0)
    once = pl.Buffered(1)
    specs, scratch = _ffn_specs(d, ffn_w_out.shape[2], layer, 1)
    return pl.pallas_call(
        _mix_out_kernel,
        grid=(t // TM_FFN,),
        in_specs=[
            pl.BlockSpec((TM_FFN, oa.shape[1]), row),
            pl.BlockSpec((TM_FFN, ob.shape[1]), row),
            pl.BlockSpec(wa.shape, const, pipeline_mode=once),
            pl.BlockSpec(wb.shape, const, pipeline_mode=once),
            pl.BlockSpec((TM_FFN, d), row),
            pl.BlockSpec((1, d), const),
            pl.BlockSpec((1, d), const),
        ] + specs,
        out_specs=pl.BlockSpec((TM_FFN, d), row),
        out_shape=jax.ShapeDtypeStruct((t, d), F32),
        scratch_shapes=[pltpu.VMEM((TM_FFN, d), F32)] + scratch,
        compiler_params=_params("parallel"),
        name="mix_out_ffn",
    )(oa, ob, wa, wb, x, g, b, ffn_w_in, ffn_w_out, g_all, b_all)


def _conv_in_kernel(x_ref, w_ref, o_ref):
    xb = x_ref[...].astype(BF16)
    d = o_ref.shape[1]
    a = _dot(xb, w_ref[:, 0:d])
    gate = _dot(xb, w_ref[:, d:2 * d])
    o_ref[...] = a * jax.nn.sigmoid(gate)


def _conv_in(x, w):
    t, d = x.shape
    return pl.pallas_call(
        _conv_in_kernel,
        grid=(t // TM_CONV,),
        in_specs=[pl.BlockSpec((TM_CONV, d), lambda i: (i, 0)),
                  pl.BlockSpec((d, 2 * d), lambda i: (0, 0))],
        out_specs=pl.BlockSpec((TM_CONV, d), lambda i: (i, 0)),
        out_shape=jax.ShapeDtypeStruct((t, d), F32),
        compiler_params=_params("parallel"),
        name="conv_in",
    )(x, w)


def _conv_out_kernel(h_ref, halo_ref, dw_ref, dwb_ref, cg_ref, cb_ref, w_ref, x_ref, g_ref, b_ref,
                     wi_ref, wo_ref, g2_ref, b2_ref, o_ref, buf_ref, cv_ref, x1_ref, xb_ref, acc_ref):
    i = pl.program_id(1)
    buf_ref[0:HALO, :] = jnp.where(i > 0, halo_ref[...], 0.0)
    buf_ref[HALO:, :] = h_ref[...]
    off = HALO - (CONV_W - 1)

    def chunk(c, carry):
        r0 = pl.multiple_of(c * CONV_ROWS, CONV_ROWS)
        nwin = CONV_ROWS + HALO
        for lt in range(dwb_ref.shape[1] // LANES):
            cols = slice(lt * LANES, (lt + 1) * LANES)
            win = buf_ref[pl.ds(r0, nwin), cols]
            acc = jnp.broadcast_to(dwb_ref[:, cols], (CONV_ROWS, LANES))
            for sub in range(SUBLANES):
                shifted = pltpu.roll(win, nwin - sub, axis=0) if sub else win
                for k in range(CONV_W):
                    if (off + k) % SUBLANES == sub:
                        a0 = off + k - sub
                        acc = acc + dw_ref[k:k + 1, cols] * shifted[a0:a0 + CONV_ROWS]
            cv_ref[pl.ds(r0, CONV_ROWS), cols] = acc
        return carry

    lax.fori_loop(0, TM_CONV // CONV_ROWS, chunk, 0)
    y = _layer_norm(cv_ref[...], cg_ref[...], cb_ref[...])
    u = (y * jax.nn.sigmoid(y)).astype(BF16)
    m = _dot(u, w_ref[...])
    x1_ref[...] = _layer_norm(ALPHA * x_ref[...] + m, g_ref[...], b_ref[...])
    _ffn_kernel(x1_ref, wi_ref, wo_ref, g2_ref, b2_ref, o_ref, xb_ref, acc_ref)


def _conv_out(h, dw, dwb, cg, cb, w, x, g, b, ffn_w_in, ffn_w_out, g_all, b_all, layer, batch, seq):
    t, d = x.shape
    specs, scratch = _ffn_specs(d, ffn_w_out.shape[2], layer, 1)
    nt = seq // TM_CONV
    per = TM_CONV // HALO
    row = lambda bb, i: (bb * nt + i, 0)
    halo = lambda bb, i: (jnp.maximum((bb * nt + i) * per - 1, 0), 0)
    const = lambda bb, i: (0, 0)
    return pl.pallas_call(
        _conv_out_kernel,
        grid=(batch, nt),
        in_specs=[
            pl.BlockSpec((TM_CONV, d), row),
            pl.BlockSpec((HALO, d), halo),
            pl.BlockSpec(dw.shape, const),
            pl.BlockSpec((1, d), const),
            pl.BlockSpec((1, d), const),
            pl.BlockSpec((1, d), const),
            pl.BlockSpec((d, d), const, pipeline_mode=pl.Buffered(1)),
            pl.BlockSpec((TM_CONV, d), row),
            pl.BlockSpec((1, d), const),
            pl.BlockSpec((1, d), const),
        ] + specs,
        out_specs=pl.BlockSpec((TM_CONV, d), row),
        out_shape=jax.ShapeDtypeStruct((t, d), F32),
        scratch_shapes=[pltpu.VMEM((TM_CONV + HALO, d), F32), pltpu.VMEM((TM_CONV, d), F32),
                        pltpu.VMEM((TM_CONV, d), F32)] + scratch,
        compiler_params=_params("parallel", "arbitrary"),
        name="conv_out_ffn",
    )(h, h, dw, dwb, cg, cb, w, x, g, b, ffn_w_in, ffn_w_out, g_all, b_all)


def _attention_mixer(x, w_in, ikg, ikb, lam, subg, w_out, bias, g, b, ffn_tail, batch, seq, layer_idx):
    c = [0]
    for n in (HA * DA, HA * DA, HA * DA, HI * DI, DI, HI, HB * 2 * DB, HB * 2 * DB, HB * 2 * DB):
        c.append(c[-1] + n)
    w_ik = w_in[:, c[4]:c[5]]
    w_iw = jnp.pad(w_in[:, c[5]:c[6]], ((0, 0), (0, LANES - HI)))
    w = jnp.concatenate([w_in[:, :c[4]], w_ik, w_ik, w_iw, w_in[:, c[6]:]], axis=1).astype(BF16)
    ikg2 = jnp.concatenate([ikg, ikg])[None]
    ikb2 = jnp.concatenate([ikb, ikb])[None]
    qa, ka, vat, iq, ik2, iwt, qb, kb, vbt = _mix_proj(x, w, ikg2, ikb2)
    o_a = _dsa(qa, ka, vat, iq, ik2, iwt, bias[:HA], batch, seq)
    lam_init = 0.8 - 0.6 * math.exp(-0.3 * layer_idx)
    o_b = _diff(qb, kb, vbt, lam, subg[None], bias[HA:], batch, seq, lam_init)
    wo = w_out.astype(BF16)
    return _mix_out(o_a, o_b, wo[:HA * DA], wo[HA * DA:], x, g, b, *ffn_tail, layer_idx)


def _conv_module(x, w_in, dw, dwb, cg, cb, w_out, g, b, ffn_tail, layer_idx, batch, seq):
    h = _conv_in(x, w_in.astype(BF16))
    dwp = jnp.pad(dw, ((0, HALO - CONV_W), (0, 0)))
    return _conv_out(h, dwp, dwb[None], cg[None], cb[None], w_out.astype(BF16), x, g, b,
                     *ffn_tail, layer_idx, batch, seq)


def kernel(x, ffn_in, ffn_out, ln_g, ln_b, rel_bias, mix_w_in, idx_k_g, idx_k_b, diff_lambda,
           diff_subln_g, mix_w_out, conv_w_in, conv_dw, conv_dw_b, conv_ln_g, conv_ln_b, conv_w_out):
    batch, seq, d = x.shape
    x = x.reshape(batch * seq, d)
    bias = _bias_tiles(rel_bias.T)
    ffn_in_b = ffn_in.astype(BF16)
    ffn_out_b = ffn_out.astype(BF16)
    g_all = ln_g.reshape(3 * DEPTH, 1, d)
    b_all = ln_b.reshape(3 * DEPTH, 1, d)
    ffn_tail = (ffn_in_b, ffn_out_b, g_all, b_all)
    for l in range(DEPTH):
        j = l // 2
        x = _ffn(x, ffn_in_b, ffn_out_b, g_all, b_all, l, 0)
        if l % 2 == 0:
            x = _attention_mixer(x, mix_w_in[j], idx_k_g[j], idx_k_b[j], diff_lambda[j], diff_subln_g[j],
                                 mix_w_out[j], bias, ln_g[l, 1][None], ln_b[l, 1][None], ffn_tail, batch, seq, l)
        else:
            x = _conv_module(x, conv_w_in[j], conv_dw[j], conv_dw_b[j], conv_ln_g[j], conv_ln_b[j],
                             conv_w_out[j], ln_g[l, 1][None], ln_b[l, 1][None], ffn_tail, l, batch, seq)
    return x.reshape(batch, seq, d)
```
